```python
import jax, jax.numpy as jnp
from jax import lax
import numpy as np

D_MODEL = 2048
BATCH = 32
SEQ = 256
DEPTH = 2
DEC_BATCH = 8
DEC_SEQ = 4096
PAST_LEN = 512

GRID_W = 64
H_A = 4
DK_A = 64
DV_A = 128
GATE_RANK = 16
GATE_TEMP = 16.0
H_B = 4
DK_B = 64
DV_B = 128
H_C = 8
D_C = 64
WIN_H = 8
WIN_W = 16
COL_BLOCK = 16
PATCH_W = 32
Q_BLOCK = 128
CHUNK = 64
N_EXPERTS = 16
EXPERT_FF = 1024
EC_FACTOR = 2
ROPE_BASE = 10000.0
EPS = 1e-6
NEG_INF = -1e30
QK_A = H_A * DK_A
V_A = H_A * DV_A
QK_B = H_B * DK_B
V_B = H_B * DV_B
W_C = H_C * D_C
IN_SPLITS = (QK_A, QK_A, V_A, V_A, GATE_RANK, GATE_RANK, QK_B, QK_B, V_B, V_B, W_C, W_C, W_C, D_MODEL, D_MODEL, D_MODEL)
IN_WIDTH = sum(IN_SPLITS)

kernel_name = 'hybrid_gla_retnet_natten_ec_diffusion_step'


def rms_norm(x, g):
    xf = x.astype(jnp.float32)
    y = xf * lax.rsqrt(jnp.mean(xf * xf, axis=-1, keepdims=True) + EPS)
    return (y * g.astype(jnp.float32)).astype(x.dtype)


def head_rms_norm(o, g):
    return o * lax.rsqrt(jnp.mean(o * o, axis=-1, keepdims=True) + EPS) * g.astype(jnp.float32)


def axial_rope(x):
    n, d = x.shape[1], x.shape[-1]
    half, quarter = d // 2, d // 4
    t = jnp.arange(n)
    row = (t // GRID_W).astype(jnp.float32)
    col = (t % GRID_W).astype(jnp.float32)
    inv = ROPE_BASE ** (-jnp.arange(quarter, dtype=jnp.float32) / quarter)
    ang = jnp.concatenate([row[:, None] * inv, col[:, None] * inv], axis=-1)[None, :, None, :]
    cos, sin = jnp.cos(ang), jnp.sin(ang)
    x1, x2 = x[..., :half], x[..., half:]
    return jnp.concatenate([x1 * cos - x2 * sin, x1 * sin + x2 * cos], axis=-1)


def gla_scan(q, k, v, log_a, s0):
    B, N, H, dk = q.shape
    dv = v.shape[-1]
    nc = N // CHUNK
    q, k, log_a = (t.reshape(B, nc, CHUNK, H, dk) for t in (q, k, log_a))
    v = v.reshape(B, nc, CHUNK, H, dv)
    b = jnp.cumsum(log_a, axis=2)
    b_last = b[:, :, -1]
    q_in = q * jnp.exp(b)
    att = jnp.einsum('bcthk,bcshk->bchts', q_in, k * jnp.exp(-b))
    att = jnp.where(jnp.tril(jnp.ones((CHUNK, CHUNK), bool)), att, 0.0)
    o_intra = jnp.einsum('bchts,bcshv->bcthv', att, v)
    upd = jnp.einsum('bcshk,bcshv->bchkv', k * jnp.exp(b_last[:, :, None] - b), v)

    def step(S, inp):
        dec, u = inp
        return dec[..., None] * S + u, S

    s_fin, s_prev = lax.scan(step, s0, (jnp.moveaxis(jnp.exp(b_last), 1, 0), jnp.moveaxis(upd, 1, 0)))
    o_inter = jnp.einsum('bcthk,bchkv->bcthv', q_in, jnp.moveaxis(s_prev, 0, 1))
    return (o_intra + o_inter).reshape(B, N, H, dv), s_fin


def retention_scan(q, k, v, log_gamma, s0):
    B, N, H, dk = q.shape
    dv = v.shape[-1]
    nc = N // CHUNK
    q = q.reshape(B, nc, CHUNK, H, dk)
    k = k.reshape(B, nc, CHUNK, H, dk)
    v = v.reshape(B, nc, CHUNK, H, dv)
    lg = log_gamma.astype(jnp.float32)[:, None]
    pos = jnp.arange(CHUNK, dtype=jnp.float32)
    diff = pos[:, None] - pos[None, :]
    decay_mat = jnp.where(diff >= 0, jnp.exp(lg[:, :, None] * jnp.maximum(diff, 0.0)), 0.0)
    q_dec = jnp.exp(lg * (pos + 1.0)).T
    k_dec = jnp.exp(lg * (CHUNK - 1.0 - pos)).T
    chunk_dec = jnp.exp(lg[:, 0] * CHUNK)
    att = jnp.einsum('bcthk,bcshk->bchts', q, k) * decay_mat
    o_intra = jnp.einsum('bchts,bcshv->bcthv', att, v)
    upd = jnp.einsum('bcshk,bcshv->bchkv', k * k_dec[:, :, None], v)

    def step(S, u):
        return chunk_dec[None, :, None, None] * S + u, S

    s_fin, s_prev = lax.scan(step, s0, jnp.moveaxis(upd, 1, 0))
    o_inter = jnp.einsum('bcthk,bchkv->bcthv', q * q_dec[:, :, None], jnp.moveaxis(s_prev, 0, 1))
    return (o_intra + o_inter).reshape(B, N, H, dv), s_fin


def context_attention(q, k, v):
    B, L, H, d = q.shape
    qb = jnp.moveaxis(q.reshape(B, L // Q_BLOCK, Q_BLOCK, H, d), 1, 0)

    def block(qi):
        p = jax.nn.softmax(jnp.einsum('bqhd,bkhd->bhqk', qi, k), axis=-1)
        return jnp.einsum('bhqk,bkhd->bqhd', p, v)

    return jnp.moveaxis(lax.map(block, qb), 0, 1).reshape(B, L, H, d)


def neighbourhood_attention(q, k, v, k_ctx, v_ctx, rpb):
    B, N, H, d = q.shape
    rows = N // GRID_W
    kh = min(WIN_H, rows)
    ncb = GRID_W // COL_BLOCK
    qc = np.arange(GRID_W)
    win_c = np.clip(qc - WIN_W // 2, 0, GRID_W - WIN_W)
    patch_c = np.minimum(win_c[::COL_BLOCK], GRID_W - PATCH_W)
    kcol = patch_c[:, None] + np.arange(PATCH_W)
    qcol = qc.reshape(ncb, COL_BLOCK)
    ws = win_c[qcol][:, :, None]
    col_ok = (kcol[:, None, :] >= ws) & (kcol[:, None, :] < ws + WIN_W)
    coff = np.clip(kcol[:, None, :] - qcol[:, :, None], -(WIN_W - 1), WIN_W - 1) + WIN_W - 1
    bias_col = rpb[:, :, coff]
    kg = k.reshape(B, rows, GRID_W, H, d)
    vg = v.reshape(B, rows, GRID_W, H, d)
    qg = q.reshape(B, rows, ncb, COL_BLOCK, H, d)
    n_win = kh * PATCH_W

    def row_block(r):
        r0 = jnp.clip(r - WIN_H // 2, 0, rows - kh)
        kp = lax.dynamic_slice_in_dim(kg, r0, kh, axis=1)[:, :, kcol]
        vp = lax.dynamic_slice_in_dim(vg, r0, kh, axis=1)[:, :, kcol]
        qr = lax.dynamic_index_in_dim(qg, r, axis=1, keepdims=False)
        roff = r0 + jnp.arange(kh) - r + WIN_H - 1
        bias = jnp.transpose(bias_col[:, roff], (0, 2, 3, 1, 4))
        s_win = jnp.einsum('bjqhd,bijphd->bhjqip', qr, kp) + bias[None]
        s_win = jnp.where(col_ok[:, :, None, :], s_win, NEG_INF).reshape(B, H, ncb, COL_BLOCK, n_win)
        s_ctx = jnp.einsum('bjqhd,blhd->bhjql', qr, k_ctx)
        p = jax.nn.softmax(jnp.concatenate([s_win, s_ctx], axis=-1), axis=-1)
        p_win = p[..., :n_win].reshape(B, H, ncb, COL_BLOCK, kh, PATCH_W)
        return (jnp.einsum('bhjqip,bijphd->bjqhd', p_win, vp)
                + jnp.einsum('bhjql,blhd->bjqhd', p[..., n_win:], v_ctx))

    o = lax.map(row_block, jnp.arange(rows))
    return jnp.moveaxis(o, 0, 1).reshape(B, N, H, d)


def expert_choice_ffn(xn, w_router, w_gate, w_up, w_down):
    B, N, D = xn.shape
    n = B * N
    cap = EC_FACTOR * n // N_EXPERTS
    xt = xn.reshape(n, D)
    aff = jax.nn.softmax(jnp.einsum('nd,de->ne', xt, w_router).astype(jnp.float32), axis=-1)
    g, idx = lax.top_k(aff.T, cap)
    xe = xt[idx]
    h = jax.nn.silu(jnp.einsum('ecd,edf->ecf', xe, w_gate)) * jnp.einsum('ecd,edf->ecf', xe, w_up)
    ye = jnp.einsum('ecf,efd->ecd', h, w_down) * g[..., None].astype(xn.dtype)
    out = jnp.zeros_like(xt).at[idx.reshape(-1)].add(ye.reshape(-1, D))
    return out.reshape(B, N, D)


def token_mixer(xn, lw, ctx):
    latent = ctx is not None
    B, N, _ = xn.shape
    f32 = jnp.float32
    dt = xn.dtype
    split_at = [int(s) for s in np.cumsum(IN_SPLITS)[:-1]]
    (a_q, a_k, a_v, a_r, a_zf, a_zb, b_q, b_k, b_v, b_g, c_q, c_k, c_v, m_a, m_b, m_c) = jnp.split(
        jnp.einsum('bnd,de->bne', xn, lw['w_in']), split_at, axis=-1)
    heads = lambda t, h: t.astype(f32).reshape(B, N, h, -1)
    flip = lambda t: jnp.flip(t, axis=1)
    if latent:
        k_ctx, v_ctx, sa_f0, sa_b0, sb_f0, sb_b0 = [t.astype(f32) for t in ctx]
    else:
        sa_f0 = sa_b0 = jnp.zeros((B, H_A, DK_A, DV_A), f32)
        sb_f0 = sb_b0 = jnp.zeros((B, H_B, DK_B, DV_B), f32)
    q = heads(a_q, H_A) * DK_A ** -0.5
    k = heads(a_k, H_A)
    v = heads(a_v, H_A)
    lg_f = heads(jax.nn.log_sigmoid(a_zf.astype(f32) @ lw['gla_w_gf'].astype(f32) + lw['gla_b_gf'].astype(f32)), H_A) / GATE_TEMP
    lg_b = heads(jax.nn.log_sigmoid(a_zb.astype(f32) @ lw['gla_w_gb'].astype(f32) + lw['gla_b_gb'].astype(f32)), H_A) / GATE_TEMP
    oa_f, sa_f = gla_scan(q, k, v, lg_f, sa_f0)
    oa_b, sa_b = gla_scan(flip(q), flip(k), flip(v), flip(lg_b), sa_b0)
    o_a = head_rms_norm(oa_f + flip(oa_b), lw['gla_gn']) * jax.nn.silu(heads(a_r, H_A))
    q = heads(b_q, H_B)
    k = heads(b_k, H_B) * DK_B ** -0.5
    v = heads(b_v, H_B)
    if latent:
        q, k = axial_rope(q), axial_rope(k)
    ob_f, sb_f = retention_scan(q, k, v, lw['ret_ld_f'], sb_f0)
    ob_b, sb_b = retention_scan(flip(q), flip(k), flip(v), lw['ret_ld_b'], sb_b0)
    o_b = head_rms_norm(ob_f + flip(ob_b), lw['ret_gn']) * jax.nn.silu(heads(b_g, H_B))
    q = heads(c_q, H_C) * D_C ** -0.5
    k = heads(c_k, H_C)
    v = heads(c_v, H_C)
    if latent:
        o_c = neighbourhood_attention(q, k, v, k_ctx, v_ctx, lw['nat_rpb'].astype(f32))
    else:
        o_c = context_attention(q, k, v)
    flat = lambda o: o.reshape(B, N, -1).astype(dt)
    merged = (jax.nn.sigmoid(m_a) * (flat(o_a) @ lw['w_br_a'])
              + jax.nn.sigmoid(m_b) * (flat(o_b) @ lw['w_br_b'])
              + jax.nn.sigmoid(m_c) * (flat(o_c) @ lw['w_br_c']))
    out = merged @ lw['w_out']
    if latent:
        return out, None
    return out, (k.astype(dt), v.astype(dt), sa_f.astype(dt), sa_b.astype(dt), sb_f.astype(dt), sb_b.astype(dt))


def trunk_layer(x, cond, lw, ctx):
    mod = jax.nn.silu(cond) @ lw['w_ada'] + lw['b_ada']
    sh1, sc1, g1, sh2, sc2, g2 = [m[:, None, :] for m in jnp.split(mod, 6, axis=-1)]
    h = rms_norm(x, lw['norm1']) * (1 + sc1) + sh1
    mix, ctx_out = token_mixer(h, lw, ctx)
    x = x + g1 * mix
    h = rms_norm(x, lw['norm2']) * (1 + sc2) + sh2
    x = x + g2 * expert_choice_ffn(h, lw['w_router'], lw['w_gate'], lw['w_up'], lw['w_down'])
    return x, ctx_out


def setup_inputs(seed: int = 0) -> dict:
    key = jax.random.key(seed)
    ks = iter(jax.random.split(key, 40))
    f32 = jnp.float32
    nrm = lambda shape, scale: jax.random.normal(next(ks), shape, f32) * scale
    L = DEPTH
    base_ld = jnp.log1p(-(2.0 ** (-5.0 - jnp.arange(H_B, dtype=f32))))
    inp = {}
    inp['x_prompt'] = nrm((BATCH, SEQ, D_MODEL), 1.0)
    inp['x_sample'] = nrm((DEC_BATCH, DEC_SEQ, D_MODEL), 1.0)
    inp['cache_nat_k'] = nrm((DEC_BATCH, L, PAST_LEN, H_C, D_C), 1.0)
    inp['cache_nat_v'] = nrm((DEC_BATCH, L, PAST_LEN, H_C, D_C), 1.0)
    inp['state_gla_fwd'] = nrm((DEC_BATCH, L, H_A, DK_A, DV_A), 1.0)
    inp['state_gla_bwd'] = nrm((DEC_BATCH, L, H_A, DK_A, DV_A), 1.0)
    inp['state_ret_fwd'] = nrm((DEC_BATCH, L, H_B, DK_B, DV_B), 1.0)
    inp['state_ret_bwd'] = nrm((DEC_BATCH, L, H_B, DK_B, DV_B), 1.0)
    inp['c'] = nrm((DEC_BATCH, D_MODEL), 1.0)
    inp['c_ctx'] = nrm((D_MODEL,), 1.0)
    inp['norm1'] = 1.0 + nrm((L, D_MODEL), 0.05)
    inp['norm2'] = 1.0 + nrm((L, D_MODEL), 0.05)
    inp['w_ada'] = nrm((L, D_MODEL, 6 * D_MODEL), 0.5 * D_MODEL ** -0.5)
    inp['b_ada'] = nrm((L, 6 * D_MODEL), 0.02)
    inp['w_in'] = nrm((L, D_MODEL, IN_WIDTH), D_MODEL ** -0.5)
    inp['gla_w_gf'] = nrm((L, GATE_RANK, QK_A), GATE_RANK ** -0.5)
    inp['gla_b_gf'] = nrm((L, QK_A), 0.1)
    inp['gla_w_gb'] = nrm((L, GATE_RANK, QK_A), GATE_RANK ** -0.5)
    inp['gla_b_gb'] = nrm((L, QK_A), 0.1)
    inp['gla_gn'] = 1.0 + nrm((L, DV_A), 0.05)
    inp['ret_log_decay_f'] = base_ld[None] * (1.0 + nrm((L, H_B), 0.05))
    inp['ret_log_decay_b'] = base_ld[None] * (1.0 + nrm((L, H_B), 0.05))
    inp['ret_gn'] = 1.0 + nrm((L, DV_B), 0.05)
    inp['nat_rpb'] = nrm((L, H_C, 2 * WIN_H - 1, 2 * WIN_W - 1), 0.1)
    inp['w_br_a'] = nrm((L, V_A, D_MODEL), V_A ** -0.5)
    inp['w_br_b'] = nrm((L, V_B, D_MODEL), V_B ** -0.5)
    inp['w_br_c'] = nrm((L, W_C, D_MODEL), W_C ** -0.5)
    inp['w_out'] = nrm((L, D_MODEL, D_MODEL), D_MODEL ** -0.5)
    inp['w_router'] = nrm((L, D_MODEL, N_EXPERTS), D_MODEL ** -0.5)
    inp['w_gate'] = nrm((L, N_EXPERTS, D_MODEL, EXPERT_FF), D_MODEL ** -0.5)
    inp['w_up'] = nrm((L, N_EXPERTS, D_MODEL, EXPERT_FF), D_MODEL ** -0.5)
    inp['w_down'] = nrm((L, N_EXPERTS, EXPERT_FF, D_MODEL), EXPERT_FF ** -0.5)
    inp['final_norm'] = 1.0 + nrm((D_MODEL,), 0.05)
    return inp


def reference(x_prompt, x_sample, cache_nat_k, cache_nat_v, state_gla_fwd, state_gla_bwd, state_ret_fwd,
              state_ret_bwd, c, c_ctx, norm1, norm2, w_ada, b_ada, w_in, gla_w_gf, gla_b_gf, gla_w_gb, gla_b_gb,
              gla_gn, ret_log_decay_f, ret_log_decay_b, ret_gn, nat_rpb, w_br_a, w_br_b, w_br_c, w_out, w_router,
              w_gate, w_up, w_down, final_norm):
    xp = x_prompt
    xs = x_sample
    nk, nv, gf, gb, rf, rb = [], [], [], [], [], []
    for l in range(DEPTH):
        lw = {'norm1': norm1[l], 'norm2': norm2[l], 'w_ada': w_ada[l], 'b_ada': b_ada[l], 'w_in': w_in[l],
              'gla_w_gf': gla_w_gf[l], 'gla_b_gf': gla_b_gf[l], 'gla_w_gb': gla_w_gb[l], 'gla_b_gb': gla_b_gb[l],
              'gla_gn': gla_gn[l], 'ret_ld_f': ret_log_decay_f[l], 'ret_ld_b': ret_log_decay_b[l],
              'ret_gn': ret_gn[l], 'nat_rpb': nat_rpb[l], 'w_br_a': w_br_a[l], 'w_br_b': w_br_b[l],
              'w_br_c': w_br_c[l], 'w_out': w_out[l], 'w_router': w_router[l], 'w_gate': w_gate[l],
              'w_up': w_up[l], 'w_down': w_down[l]}
        xp, ctx_t = trunk_layer(xp, c_ctx[None, :], lw, None)
        nk.append(ctx_t[0])
        nv.append(ctx_t[1])
        gf.append(ctx_t[2])
        gb.append(ctx_t[3])
        rf.append(ctx_t[4])
        rb.append(ctx_t[5])
        cached = (cache_nat_k[:, l], cache_nat_v[:, l], state_gla_fwd[:, l], state_gla_bwd[:, l],
                  state_ret_fwd[:, l], state_ret_bwd[:, l])
        xs, _ = trunk_layer(xs, c, lw, cached)
    y_prompt = rms_norm(xp, final_norm)
    y_sample = rms_norm(xs, final_norm)
    return (y_prompt, y_sample, jnp.stack(nk, axis=1), jnp.stack(nv, axis=1), jnp.stack(gf, axis=1),
            jnp.stack(gb, axis=1), jnp.stack(rf, axis=1), jnp.stack(rb, axis=1))
```

```python
import functools

import numpy as np
import jax
import jax.numpy as jnp
from jax import lax
from jax.experimental import pallas as pl
from jax.experimental.pallas import tpu as pltpu

F32 = jnp.float32
I32 = jnp.int32
MXU = jnp.bfloat16

GRID_W = 64
H_A, DK_A, DV_A = 4, 64, 128
GATE_RANK, GATE_TEMP = 16, 16.0
H_B, DK_B, DV_B = 4, 64, 128
H_C, D_C = 8, 64
WIN_H, WIN_W = 8, 16
CHUNK = 64
N_EXPERTS, EC_FACTOR = 16, 2
ROPE_BASE = 10000.0
EPS = 1e-6
NEG_INF = -1e30

QK_A, V_A = H_A * DK_A, H_A * DV_A
QK_B, V_B = H_B * DK_B, H_B * DV_B
W_C = H_C * D_C
C_AQ, C_AK, C_AV, C_AR = 0, 256, 512, 1024
C_BQ, C_BK, C_BV, C_BG = 1536, 1792, 2048, 2560
C_CQ, C_CK, C_CV, C_Z = 3072, 3584, 4096, 4608
C1 = 4736
MIX_W = 4640

LANES = 128
SUPER = 256
NCH = SUPER // CHUNK
Q_ROWS = 4
K_ROWS = Q_ROWS + WIN_H
VMEM_LIMIT = 56 * 1024 * 1024


def _cparams(sem):
    return pltpu.CompilerParams(dimension_semantics=sem, vmem_limit_bytes=VMEM_LIMIT)


def _mm(a, b):
    return jnp.dot(a.astype(MXU), b.astype(MXU), preferred_element_type=F32)


def _mm_nt(a, b):
    return lax.dot_general(a.astype(MXU), b.astype(MXU), (((1,), (1,)), ((), ())), preferred_element_type=F32)


def _iota(shape, dim):
    return lax.broadcasted_iota(I32, shape, dim)


def _sigmoid(x):
    return 1.0 / (1.0 + jnp.exp(-x))


def _silu(x):
    return x * _sigmoid(x)


def _resident(shape):
    nd = len(shape)
    return pl.BlockSpec(shape, lambda *_: (0,) * nd, pipeline_mode=pl.Buffered(1))


def _ada_kernel(cond_ref, w_ref, b_ref, o_ref):
    o_ref[0] = _mm(_silu(cond_ref[...]), w_ref[0]) + b_ref[0]


def _ada(cond, w_ada, b_ada):
    L, D, D6 = w_ada.shape
    R = cond.shape[0]
    tn = 1024 if D6 % 1024 == 0 else D
    assert D6 % tn == 0
    return pl.pallas_call(
        _ada_kernel,
        grid=(L, D6 // tn),
        in_specs=[pl.BlockSpec((R, D), lambda l, j: (0, 0)),
                  pl.BlockSpec((1, D, tn), lambda l, j: (l, 0, j)),
                  pl.BlockSpec((1, 1, tn), lambda l, j: (l, 0, j))],
        out_specs=pl.BlockSpec((1, R, tn), lambda l, j: (l, 0, j)),
        out_shape=jax.ShapeDtypeStruct((L, R, D6), F32),
        compiler_params=_cparams(("arbitrary", "arbitrary")),
        name="ada",
    )(cond, w_ada, b_ada.reshape(L, 1, D6))


def _norm_mod(x, g, shift, scale):
    y = x * lax.rsqrt(jnp.mean(x * x, axis=-1, keepdims=True) + EPS) * g
    return y * (1.0 + scale) + shift


def _tile_batch_map(tm, n_per_batch, nb):
    if nb == 1:
        return lambda i: (0, 0, 0)
    assert n_per_batch % tm == 0
    per = n_per_batch // tm
    return lambda i: (i // per, 0, 0)


def _col_chunks(width, step):
    return [(s, min(step, width - s)) for s in range(0, width, step)]


def _in_kernel(*refs, has_res):
    if has_res:
        xa_ref, xb_ref, modp_ref, mod_ref, g_ref, w_ref, p_ref, x_ref = refs
        x = xa_ref[...] + modp_ref[0, 5:6, :] * xb_ref[...]
        x_ref[...] = x
    else:
        xa_ref, mod_ref, g_ref, w_ref, p_ref = refs
        x = xa_ref[...]
    h = _norm_mod(x, g_ref[...], mod_ref[0, 0:1, :], mod_ref[0, 1:2, :]).astype(MXU)
    for s, w in _col_chunks(w_ref.shape[1], 512):
        p_ref[:, s:s + w] = jnp.dot(h, w_ref[:, s:s + w], preferred_element_type=F32)


def _in_proj(xa, xb, mod_prev, mod, g, w1, n_per_batch):
    T, D = xa.shape
    nb = mod.shape[0]
    tm = min(256, T)
    bmap = _tile_batch_map(tm, n_per_batch, nb)
    tile = pl.BlockSpec((tm, D), lambda i: (i, 0))
    mspec = pl.BlockSpec((1, 6, D), bmap)
    has_res = xb is not None
    ins = [xa] + ([xb, mod_prev] if has_res else []) + [mod, g.reshape(1, D), w1]
    in_specs = [tile] + ([tile, mspec] if has_res else []) + [mspec, _resident((1, D)), _resident(w1.shape)]
    p_shape = jax.ShapeDtypeStruct((T, w1.shape[1]), F32)
    p_spec = pl.BlockSpec((tm, w1.shape[1]), lambda i: (i, 0))
    out = pl.pallas_call(
        functools.partial(_in_kernel, has_res=has_res),
        grid=(T // tm,),
        in_specs=in_specs,
        out_specs=(p_spec, tile) if has_res else p_spec,
        out_shape=(p_shape, jax.ShapeDtypeStruct((T, D), F32)) if has_res else p_shape,
        compiler_params=_cparams(("arbitrary",)),
        name="in_proj",
    )(*ins)
    return out if has_res else (out, xa)


def _split3(x):
    p1 = x.astype(MXU)
    r1 = x - p1.astype(F32)
    p2 = r1.astype(MXU)
    p3 = (r1 - p2.astype(F32)).astype(MXU)
    return p1, p2, p3


def _chunk_masks():
    row, col = _iota((SUPER, SUPER), 0), _iota((SUPER, SUPER), 1)
    same = (row // CHUNK) == (col // CHUNK)
    return same, same & (col <= row), same & (col >= row)


def _head_select(h):
    sel = (_iota((DK_A, LANES), 1) == _iota((DK_A, LANES), 0) + DK_A * h).astype(MXU)
    rep = (_iota((LANES, SUPER), 0) == (_iota((LANES, SUPER), 1) % CHUNK) + DK_A * h).astype(MXU)
    return sel, rep


def _recur_step(q_in, k_dec, att, v, state, dec_fn, same, fwd, sel, rep):
    o_intra = _mm(att, v)
    kdec_t = _mm_nt(sel, k_dec)
    kblk = jnp.where(same, jnp.concatenate([kdec_t] * NCH, axis=0), 0.0)
    upd = _mm(kblk, v)
    order = range(NCH) if fwd else range(NCH - 1, -1, -1)
    prev = [None] * NCH
    for c in order:
        prev[c] = state
        state = dec_fn(c) * state + upd[c * CHUNK:(c + 1) * CHUNK]
    q_blk = jnp.where(same, _mm(q_in, rep), 0.0)
    o_inter = _mm(q_blk, jnp.concatenate(prev, axis=0))
    return o_intra + o_inter, state


def _head_norm_gate(o, g, r):
    return o * lax.rsqrt(jnp.mean(o * o, axis=-1, keepdims=True) + EPS) * g * _silu(r)


def _gla_kernel(q_ref, k_ref, v_ref, r_ref, z_ref, wf_ref, bf_ref, wb_ref, bb_ref, gn_ref, s0f_ref, s0b_ref,
                o_ref, sf_ref, sb_ref, acc_ref, *, n_sc):
    same, tri_f, tri_b = _chunk_masks()
    sels = [_head_select(h) for h in range(2)]
    lane_head = _iota((1, LANES), 1) // DK_A

    def run(sc, states, fwd):
        off = pl.multiple_of(sc * SUPER, SUPER)
        rows = pl.ds(off, SUPER)
        q = q_ref[rows, :] * (DK_A ** -0.5)
        k = k_ref[rows, :]
        x = _mm(z_ref[rows, :], (wf_ref if fwd else wb_ref)[...]) + (bf_ref if fwd else bb_ref)[...]
        la = (jnp.minimum(x, 0.0) - jnp.log1p(jnp.exp(-jnp.abs(x)))) * (1.0 / GATE_TEMP)
        tri = tri_f if fwd else tri_b
        hi = la.astype(MXU)
        lo = la - hi.astype(F32)
        trim = tri.astype(MXU)
        b = _mm(trim, hi) + _mm(trim, lo)
        last = CHUNK - 1 if fwd else 0
        tot_rows = [b[c * CHUNK + last:c * CHUNK + last + 1, :] for c in range(NCH)]
        b_last = jnp.concatenate([jnp.broadcast_to(t, (CHUNK, LANES)) for t in tot_rows], axis=0)
        tot8 = jnp.concatenate(tot_rows + [jnp.zeros((8 - NCH, LANES), F32)], axis=0)
        parts = _split3(tot8)
        q_in = q * jnp.exp(b)
        k_out = k * jnp.exp(-b)
        k_dec = k * jnp.exp(b_last - b)
        outs, new_states = [], []
        for h in range(2):
            sel, rep = sels[h]
            tot_t = sum(_mm_nt(sel, p) for p in parts)
            dec_fn = lambda c, tot_t=tot_t: jnp.exp(jnp.broadcast_to(tot_t[:, c:c + 1], (DK_A, DV_A)))
            qh = jnp.where(lane_head == h, q_in, 0.0)
            att = jnp.where(tri, _mm_nt(qh, k_out), 0.0)
            v = v_ref[rows, h * DV_A:(h + 1) * DV_A]
            o, s = _recur_step(qh, k_dec, att, v, states[h], dec_fn, same, fwd, sel, rep)
            outs.append(o)
            new_states.append(s)
        return rows, outs, tuple(new_states)

    def fwd_body(i, states):
        rows, outs, states = run(i, states, True)
        for h in range(2):
            acc_ref[rows, h * DV_A:(h + 1) * DV_A] = outs[h]
        return states

    def bwd_body(i, states):
        rows, outs, states = run(n_sc - 1 - i, states, False)
        for h in range(2):
            cols = slice(h * DV_A, (h + 1) * DV_A)
            o = _head_norm_gate(acc_ref[rows, cols] + outs[h], gn_ref[...], r_ref[rows, cols])
            o_ref[rows, cols] = o.astype(o_ref.dtype)
        return states

    sf = lax.fori_loop(0, n_sc, fwd_body, (s0f_ref[0, 0], s0f_ref[0, 1]))
    sb = lax.fori_loop(0, n_sc, bwd_body, (s0b_ref[0, 0], s0b_ref[0, 1]))
    for h in range(2):
        sf_ref[0, h] = sf[h]
        sb_ref[0, h] = sb[h]


def _pair_specs(n, col_q, col_k, col_v, col_r):
    return [pl.BlockSpec((n, LANES), lambda b, p: (b, col_q // LANES + p)),
            pl.BlockSpec((n, LANES), lambda b, p: (b, col_k // LANES + p)),
            pl.BlockSpec((n, 2 * DV_A), lambda b, p: (b, col_v // (2 * DV_A) + p)),
            pl.BlockSpec((n, 2 * DV_A), lambda b, p: (b, col_r // (2 * DV_A) + p))]


def _state_spec():
    return pl.BlockSpec((1, 2, DK_A, DV_A), lambda b, p: (b, p, 0, 0))


def _gla(P, B, n, wf, bf, wb, bb, gn, s0f, s0b):
    T = B * n
    assert n % SUPER == 0
    pair_w = pl.BlockSpec((LANES, LANES), lambda b, p: (0, p))
    pair_b = pl.BlockSpec((1, LANES), lambda b, p: (0, p))
    st_shape = jax.ShapeDtypeStruct((B, H_A, DK_A, DV_A), F32)
    return pl.pallas_call(
        functools.partial(_gla_kernel, n_sc=n // SUPER),
        grid=(B, H_A // 2),
        in_specs=_pair_specs(n, C_AQ, C_AK, C_AV, C_AR)
        + [pl.BlockSpec((n, LANES), lambda b, p: (b, C_Z // LANES)),
           pair_w, pair_b, pair_w, pair_b, _resident((1, DV_A)), _state_spec(), _state_spec()],
        out_specs=(pl.BlockSpec((n, 2 * DV_A), lambda b, p: (b, p)), _state_spec(), _state_spec()),
        out_shape=(jax.ShapeDtypeStruct((T, V_A), MXU), st_shape, st_shape),
        scratch_shapes=[pltpu.VMEM((n, 2 * DV_A), F32)],
        compiler_params=_cparams(("arbitrary", "arbitrary")),
        name="gla",
    )(P, P, P, P, P, wf, bf, wb, bb, gn, s0f, s0b)


def _ret_kernel(*refs, n_sc, rope):
    if rope:
        (q_ref, k_ref, v_ref, g_ref, ldf_ref, ldb_ref, gn_ref, s0f_ref, s0b_ref, cos_ref, sin_ref,
         o_ref, sf_ref, sb_ref, acc_ref) = refs
    else:
        (q_ref, k_ref, v_ref, g_ref, ldf_ref, ldb_ref, gn_ref, s0f_ref, s0b_ref,
         o_ref, sf_ref, sb_ref, acc_ref) = refs
    same, tri_f, tri_b = _chunk_masks()
    sels = [_head_select(h) for h in range(2)]
    lane_head = _iota((1, LANES), 1) // DK_B
    half = DK_B // 2
    first_half = (_iota((1, LANES), 1) % DK_B) < half
    diff = (_iota((SUPER, SUPER), 0) - _iota((SUPER, SUPER), 1)).astype(F32)
    pos = (_iota((SUPER, LANES), 0) % CHUNK).astype(F32)

    def rotate(x, rows):
        swapped = jnp.where(first_half, pltpu.roll(x, LANES - half, 1), pltpu.roll(x, half, 1))
        return x * cos_ref[rows, :] + swapped * sin_ref[rows, :]

    def run(sc, states, fwd):
        off = pl.multiple_of(sc * SUPER, SUPER)
        rows = pl.ds(off, SUPER)
        q = q_ref[rows, :]
        k = k_ref[rows, :] * (DK_B ** -0.5)
        if rope:
            q, k = rotate(q, rows), rotate(k, rows)
        ld = (ldf_ref if fwd else ldb_ref)[0]
        lg_lane = jnp.where(lane_head == 0, ld[0:1, :LANES], ld[1:2, :LANES])
        if fwd:
            q_in = q * jnp.exp(lg_lane * (pos + 1.0))
            k_dec = k * jnp.exp(lg_lane * (CHUNK - 1.0 - pos))
        else:
            q_in = q * jnp.exp(lg_lane * (CHUNK - pos))
            k_dec = k * jnp.exp(lg_lane * pos)
        tri = tri_f if fwd else tri_b
        outs, new_states = [], []
        for h in range(2):
            sel, rep = sels[h]
            lg = ld[h:h + 1, :]
            decay = jnp.where(tri, jnp.exp(lg * jnp.abs(diff)), 0.0)
            qh = jnp.where(lane_head == h, q, 0.0)
            att = _mm_nt(qh, k) * decay
            chunk_dec = jnp.exp(lg[:, :DV_B] * float(CHUNK))
            v = v_ref[rows, h * DV_B:(h + 1) * DV_B]
            o, s = _recur_step(q_in, k_dec, att, v, states[h], lambda c: chunk_dec, same, fwd, sel, rep)
            outs.append(o)
            new_states.append(s)
        return rows, outs, tuple(new_states)

    def fwd_body(i, states):
        rows, outs, states = run(i, states, True)
        for h in range(2):
            acc_ref[rows, h * DV_B:(h + 1) * DV_B] = outs[h]
        return states

    def bwd_body(i, states):
        rows, outs, states = run(n_sc - 1 - i, states, False)
        for h in range(2):
            cols = slice(h * DV_B, (h + 1) * DV_B)
            o = _head_norm_gate(acc_ref[rows, cols] + outs[h], gn_ref[...], g_ref[rows, cols])
            o_ref[rows, cols] = o.astype(o_ref.dtype)
        return states

    sf = lax.fori_loop(0, n_sc, fwd_body, (s0f_ref[0, 0], s0f_ref[0, 1]))
    sb = lax.fori_loop(0, n_sc, bwd_body, (s0b_ref[0, 0], s0b_ref[0, 1]))
    for h in range(2):
        sf_ref[0, h] = sf[h]
        sb_ref[0, h] = sb[h]


def _ret(P, B, n, ldf, ldb, gn, s0f, s0b, rope_tabs):
    T = B * n
    rope = rope_tabs is not None
    ld_spec = pl.BlockSpec((1, 2, 2 * LANES), lambda b, p: (p, 0, 0))
    st_shape = jax.ShapeDtypeStruct((B, H_B, DK_B, DV_B), F32)
    ins = [P, P, P, P, ldf, ldb, gn, s0f, s0b] + (list(rope_tabs) if rope else [])
    return pl.pallas_call(
        functools.partial(_ret_kernel, n_sc=n // SUPER, rope=rope),
        grid=(B, H_B // 2),
        in_specs=_pair_specs(n, C_BQ, C_BK, C_BV, C_BG)
        + [ld_spec, ld_spec, _resident((1, DV_B)), _state_spec(), _state_spec()]
        + ([_resident((n, LANES))] * 2 if rope else []),
        out_specs=(pl.BlockSpec((n, 2 * DV_B), lambda b, p: (b, p)), _state_spec(), _state_spec()),
        out_shape=(jax.ShapeDtypeStruct((T, V_B), MXU), st_shape, st_shape),
        scratch_shapes=[pltpu.VMEM((n, 2 * DV_B), F32)],
        compiler_params=_cparams(("arbitrary", "arbitrary")),
        name="retention",
    )(*ins)


def _attn_ctx_kernel(q_ref, k_ref, v_ref, o_ref):
    lane_head = _iota((1, LANES), 1) // D_C
    q = q_ref[...] * (D_C ** -0.5)
    k = k_ref[...]
    v = v_ref[...]
    out = jnp.zeros(q.shape, F32)
    for h in range(2):
        s = _mm_nt(jnp.where(lane_head == h, q, 0.0), k)
        e = jnp.exp(s - jnp.max(s, axis=-1, keepdims=True))
        p = e / jnp.sum(e, axis=-1, keepdims=True)
        out = jnp.where(lane_head == h, _mm(p, v), out)
    o_ref[...] = out.astype(o_ref.dtype)


def _attn_ctx(P, B, n):
    spec = lambda col: pl.BlockSpec((n, LANES), lambda b, p: (b, col // LANES + p))
    return pl.pallas_call(
        _attn_ctx_kernel,
        grid=(B, H_C // 2),
        in_specs=[spec(C_CQ), spec(C_CK), spec(C_CV)],
        out_specs=pl.BlockSpec((n, LANES), lambda b, p: (b, p)),
        out_shape=jax.ShapeDtypeStruct((B * n, W_C), MXU),
        compiler_params=_cparams(("arbitrary", "arbitrary")),
        name="attn_ctx",
    )(P, P, P)


def _attn_lat_kernel(q_ref, k_ref, v_ref, kc_ref, vc_ref, bias_ref, o_ref, *, n_blk, key_start_max):
    lane_head = _iota((1, LANES), 1) // D_C
    nq, nk = Q_ROWS * GRID_W, K_ROWS * GRID_W
    kc = kc_ref[0]
    vc = vc_ref[0]

    def body(i, carry):
        qrows = pl.ds(pl.multiple_of(i * nq, nq), nq)
        kstart = jnp.clip(i * Q_ROWS - WIN_H // 2, 0, key_start_max)
        krows = pl.ds(pl.multiple_of(kstart * GRID_W, GRID_W), nk)
        variant = jnp.where(i == 0, 0, jnp.where(i == n_blk - 1, 2, 1))
        q = q_ref[qrows, :] * (D_C ** -0.5)
        k = k_ref[krows, :]
        v = v_ref[krows, :]
        out = jnp.zeros((nq, LANES), F32)
        for h in range(2):
            qh = jnp.where(lane_head == h, q, 0.0)
            s_win = _mm_nt(qh, k) + bias_ref[h, variant]
            s_ctx = _mm_nt(qh, kc)
            m = jnp.maximum(jnp.max(s_win, axis=-1, keepdims=True), jnp.max(s_ctx, axis=-1, keepdims=True))
            e_win = jnp.exp(s_win - m)
            e_ctx = jnp.exp(s_ctx - m)
            inv = 1.0 / (jnp.sum(e_win, axis=-1, keepdims=True) + jnp.sum(e_ctx, axis=-1, keepdims=True))
            o = _mm(e_win * inv, v) + _mm(e_ctx * inv, vc)
            out = jnp.where(lane_head == h, o, out)
        o_ref[qrows, :] = out.astype(o_ref.dtype)
        return carry

    lax.fori_loop(0, n_blk, body, 0)


def _window_bias(rpb, rows):
    n_blk = rows // Q_ROWS
    kh = min(WIN_H, rows)
    qc = np.arange(GRID_W)
    win_c = np.clip(qc - WIN_W // 2, 0, GRID_W - WIN_W)
    kc = np.arange(GRID_W)
    col_ok = (kc[None, :] >= win_c[:, None]) & (kc[None, :] < win_c[:, None] + WIN_W)
    coff = np.clip(kc[None, :] - qc[:, None], -(WIN_W - 1), WIN_W - 1) + WIN_W - 1
    tabs = []
    for blk in (0, min(1, n_blk - 1), n_blk - 1):
        r = blk * Q_ROWS + np.arange(Q_ROWS)
        kstart = int(np.clip(blk * Q_ROWS - WIN_H // 2, 0, rows - K_ROWS))
        kr = kstart + np.arange(K_ROWS)
        r0 = np.clip(r - WIN_H // 2, 0, rows - kh)
        row_ok = (kr[None, :] >= r0[:, None]) & (kr[None, :] < r0[:, None] + kh)
        roff = np.clip(kr[None, :] - r[:, None] + WIN_H - 1, 0, 2 * WIN_H - 2)
        ok = row_ok[:, None, :, None] & col_ok[None, :, None, :]
        bias = rpb[:, roff[:, None, :, None], coff[None, :, None, :]]
        tab = jnp.where(ok[None], bias, NEG_INF)
        tabs.append(tab.reshape(rpb.shape[0], Q_ROWS * GRID_W, K_ROWS * GRID_W))
    return jnp.stack(tabs, axis=1)


def _attn_lat(P, B, n, k_ctx, v_ctx, bias):
    rows = n // GRID_W
    assert rows % Q_ROWS == 0 and rows >= K_ROWS
    L_ctx = k_ctx.shape[1]
    spec = lambda col: pl.BlockSpec((n, LANES), lambda p, b: (b, col // LANES + p))
    cspec = pl.BlockSpec((1, L_ctx, LANES), lambda p, b: (b, 0, p))
    nq, nk = Q_ROWS * GRID_W, K_ROWS * GRID_W
    return pl.pallas_call(
        functools.partial(_attn_lat_kernel, n_blk=rows // Q_ROWS, key_start_max=rows - K_ROWS),
        grid=(H_C // 2, B),
        in_specs=[spec(C_CQ), spec(C_CK), spec(C_CV), cspec, cspec,
                  pl.BlockSpec((2, 3, nq, nk), lambda p, b: (p, 0, 0, 0))],
        out_specs=pl.BlockSpec((n, LANES), lambda p, b: (b, p)),
        out_shape=jax.ShapeDtypeStruct((B * n, W_C), MXU),
        compiler_params=_cparams(("arbitrary", "arbitrary")),
        name="attn_lat",
    )(P, P, P, k_ctx, v_ctx, bias)


def _merge_kernel(x_ref, mod_ref, g_ref, oa_ref, ob_ref, oc_ref, wm_ref, wa_ref, wb_ref, wc_ref, m_ref):
    D = x_ref.shape[1]
    h = _norm_mod(x_ref[...], g_ref[...], mod_ref[0, 0:1, :], mod_ref[0, 1:2, :]).astype(MXU)
    for s, w in _col_chunks(D, 512):
        acc = None
        for i, (o_ref, wbr_ref) in enumerate(((oa_ref, wa_ref), (ob_ref, wb_ref), (oc_ref, wc_ref))):
            gate = _sigmoid(jnp.dot(h, wm_ref[:, i * D + s:i * D + s + w], preferred_element_type=F32))
            term = gate * jnp.dot(o_ref[...], wbr_ref[:, s:s + w], preferred_element_type=F32)
            acc = term if acc is None else acc + term
        m_ref[:, s:s + w] = acc.astype(m_ref.dtype)


def _merge(x, mod, g, oa, ob, oc, wm, wa, wb, wc, n_per_batch):
    T, D = x.shape
    tm = min(256, T)
    bmap = _tile_batch_map(tm, n_per_batch, mod.shape[0])
    tile = lambda w: pl.BlockSpec((tm, w), lambda i: (i, 0))
    return pl.pallas_call(
        _merge_kernel,
        grid=(T // tm,),
        in_specs=[tile(D), pl.BlockSpec((1, 6, D), bmap), _resident((1, D)), tile(V_A), tile(V_B), tile(W_C),
                  _resident(wm.shape), _resident(wa.shape), _resident(wb.shape), _resident(wc.shape)],
        out_specs=tile(D),
        out_shape=jax.ShapeDtypeStruct((T, D), MXU),
        compiler_params=_cparams(("arbitrary",)),
        name="merge",
    )(x, mod, g.reshape(1, D), oa, ob, oc, wm, wa, wb, wc)


def _post_kernel(x_ref, m_ref, mod_ref, g_ref, wo_ref, wr_ref, x1_ref, h2_ref, aff_ref):
    mod = mod_ref[0]
    x1 = x_ref[...] + mod[2:3, :] * jnp.dot(m_ref[...], wo_ref[...], preferred_element_type=F32)
    x1_ref[...] = x1
    h2 = _norm_mod(x1, g_ref[...], mod[3:4, :], mod[4:5, :])
    h2_ref[...] = h2
    logits = _mm_nt(wr_ref[...], h2)
    e = jnp.exp(logits - jnp.max(logits, axis=0, keepdims=True))
    aff_ref[...] = e / jnp.sum(e, axis=0, keepdims=True)


def _post(x, merged, mod, g, wo, wr_t, n_per_batch):
    T, D = x.shape
    tm = min(512, T)
    bmap = _tile_batch_map(tm, n_per_batch, mod.shape[0])
    tile = pl.BlockSpec((tm, D), lambda i: (i, 0))
    return pl.pallas_call(
        _post_kernel,
        grid=(T // tm,),
        in_specs=[tile, tile, pl.BlockSpec((1, 6, D), bmap), _resident((1, D)), _resident(wo.shape),
                  _resident(wr_t.shape)],
        out_specs=(tile, tile, pl.BlockSpec((N_EXPERTS, tm), lambda i: (0, i))),
        out_shape=(jax.ShapeDtypeStruct((T, D), F32), jax.ShapeDtypeStruct((T, D), F32),
                   jax.ShapeDtypeStruct((N_EXPERTS, T), F32)),
        compiler_params=_cparams(("arbitrary",)),
        name="post",
    )(x, merged, mod, g.reshape(1, D), wo, wr_t)


def _prefix_counts(mask, upper, lower_strict):
    within = _mm(mask, upper)
    row_tot = jnp.broadcast_to(within[:, LANES - 1:LANES], within.shape)
    row_start = _mm(lower_strict, row_tot)
    return row_start + within - mask, row_start, within


def _split_int(x):
    high = jnp.floor(x * (1.0 / 256.0))
    return high, x - high * 256.0


def _select_kernel(aff_ref, idx_ref, gate_ref, *, cap, slot_tile):
    R = aff_ref.shape[1]
    upper = (_iota((LANES, LANES), 0) <= _iota((LANES, LANES), 1)).astype(MXU)
    lower_strict = (_iota((R, R), 1) < _iota((R, R), 0)).astype(MXU)
    lane0 = (_iota((8, LANES), 1) == 0).astype(MXU)
    lane_id = _iota((slot_tile, LANES), 1)
    row_id = _iota((slot_tile, R), 1)

    aff = aff_ref[0]
    bits = pltpu.bitcast(aff, I32)

    def bit_step(j, thr):
        cand = thr | jnp.left_shift(jnp.int32(1), 30 - j)
        cnt = jnp.sum((bits >= cand).astype(I32), axis=(0, 1), keepdims=True)
        return jnp.where(cnt >= cap, cand, thr)

    thr = lax.fori_loop(0, 31, bit_step, jnp.zeros((1, 1), I32))
    gt = (bits > thr).astype(F32)
    eq = (bits == thr).astype(F32)
    need = float(cap) - jnp.sum(gt, axis=(0, 1), keepdims=True)
    eq_rank, _, _ = _prefix_counts(eq, upper, lower_strict)
    sel = gt + eq * (eq_rank < need).astype(F32)
    _, row_start, within = _prefix_counts(sel, upper, lower_strict)
    row_end = row_start + jnp.broadcast_to(within[:, LANES - 1:LANES], within.shape)
    end_hi, end_lo = _split_int(row_end)
    row_end_t = (_mm_nt(lane0, end_hi) * 256.0 + _mm_nt(lane0, end_lo))[0:1, :]
    start_hi, start_lo = _split_int(row_start)
    aff_parts = _split3(aff)

    def tile(t, carry):
        base = t * slot_tile
        slot = (base + _iota((slot_tile, 1), 0)).astype(F32)
        row = jnp.sum((row_end_t <= slot).astype(F32), axis=1, keepdims=True)
        onehot = (row_id.astype(F32) == row).astype(MXU)
        start = _mm(onehot, start_hi) * 256.0 + _mm(onehot, start_lo)
        rank = slot - start
        counts = _mm(onehot, within)
        col = jnp.sum((counts <= rank).astype(F32), axis=1, keepdims=True)
        vals = sum(_mm(onehot, p) for p in aff_parts)
        gate = jnp.sum(jnp.where(lane_id.astype(F32) == col, vals, 0.0), axis=1, keepdims=True)
        token = (row * float(LANES) + col).astype(I32)
        out_rows = pl.ds(pl.multiple_of(base, slot_tile), slot_tile)
        idx_ref[0, out_rows, :] = jnp.broadcast_to(token, (slot_tile, LANES))
        gate_ref[0, out_rows, :] = jnp.broadcast_to(gate, (slot_tile, LANES))
        return carry

    lax.fori_loop(0, cap // slot_tile, tile, 0)


def _select(aff_t, cap):
    E, n = aff_t.shape
    assert n % LANES == 0
    R = n // LANES
    slot_tile = min(512, cap)
    shape = (E, cap, LANES)
    return pl.pallas_call(
        functools.partial(_select_kernel, cap=cap, slot_tile=slot_tile),
        grid=(E,),
        in_specs=[pl.BlockSpec((1, R, LANES), lambda e: (e, 0, 0))],
        out_specs=(pl.BlockSpec((1, cap, LANES), lambda e: (e, 0, 0)),
                   pl.BlockSpec((1, cap, LANES), lambda e: (e, 0, 0))),
        out_shape=(jax.ShapeDtypeStruct(shape, I32), jax.ShapeDtypeStruct(shape, F32)),
        compiler_params=_cparams(("arbitrary",)),
        name="select",
    )(aff_t.reshape(E, R, LANES))


def _moe_kernel(idx_ref, h_hbm, gate_ref, wg_ref, wu_ref, wd_ref, acc_in, acc_hbm, xbuf, obuf, sem, *, ts):
    del acc_in

    def row_copies(i):
        tok = idx_ref[0, 0, i]
        dst = pl.ds(i, 1)
        return (pltpu.make_async_copy(h_hbm.at[pl.ds(tok, 1), :], xbuf.at[dst, :], sem.at[0]),
                pltpu.make_async_copy(acc_hbm.at[pl.ds(tok, 1), :], obuf.at[dst, :], sem.at[1]),
                pltpu.make_async_copy(obuf.at[dst, :], acc_hbm.at[pl.ds(tok, 1), :], sem.at[2]))

    def start_gather(i, c):
        gx, go, _ = row_copies(i)
        gx.start()
        go.start()
        return c

    def wait_gather(i, c):
        gx, go, _ = row_copies(i)
        gx.wait()
        go.wait()
        return c

    lax.fori_loop(0, ts, start_gather, 0)
    lax.fori_loop(0, ts, wait_gather, 0)
    x = xbuf[...].astype(MXU)
    hidden = _silu(jnp.dot(x, wg_ref[0], preferred_element_type=F32)) * jnp.dot(x, wu_ref[0],
                                                                                preferred_element_type=F32)
    y = jnp.dot(hidden.astype(MXU), wd_ref[0], preferred_element_type=F32)
    g = gate_ref[0]
    for s, w in _col_chunks(y.shape[1], LANES):
        obuf[:, s:s + w] = obuf[:, s:s + w] + y[:, s:s + w] * g

    def start_scatter(i, c):
        row_copies(i)[2].start()
        return c

    def wait_scatter(i, c):
        row_copies(i)[2].wait()
        return c

    lax.fori_loop(0, ts, start_scatter, 0)
    lax.fori_loop(0, ts, wait_scatter, 0)


def _moe(h2, idx, gate, wg, wu, wd):
    n, D = h2.shape
    E, cap, _ = gate.shape
    FF = wg.shape[2]
    ts = min(256, cap)
    per = cap // ts
    idx_blocks = idx.reshape(E * per, 1, ts)
    wspec = lambda shape: pl.BlockSpec((1,) + shape, lambda e, t: (e, 0, 0))
    return pl.pallas_call(
        functools.partial(_moe_kernel, ts=ts),
        grid=(E, per),
        in_specs=[pl.BlockSpec((1, 1, ts), lambda e, t: (e * per + t, 0, 0), memory_space=pltpu.SMEM),
                  pl.BlockSpec(memory_space=pl.ANY),
                  pl.BlockSpec((1, ts, LANES), lambda e, t: (e, t, 0)),
                  wspec((D, FF)), wspec((D, FF)), wspec((FF, D)),
                  pl.BlockSpec(memory_space=pl.ANY)],
        out_specs=pl.BlockSpec(memory_space=pl.ANY),
        out_shape=jax.ShapeDtypeStruct((n, D), F32),
        scratch_shapes=[pltpu.VMEM((ts, D), F32), pltpu.VMEM((ts, D), F32), pltpu.SemaphoreType.DMA((3,))],
        input_output_aliases={6: 0},
        compiler_params=_cparams(("arbitrary", "arbitrary")),
        name="moe",
    )(idx_blocks, h2, gate, wg, wu, wd, jnp.zeros((n, D), F32))


def _final_kernel(xa_ref, xb_ref, mod_ref, g_ref, o_ref):
    x = xa_ref[...] + mod_ref[0, 5:6, :] * xb_ref[...]
    o_ref[...] = x * lax.rsqrt(jnp.mean(x * x, axis=-1, keepdims=True) + EPS) * g_ref[...]


def _final(xa, xb, mod, g, n_per_batch):
    T, D = xa.shape
    tm = min(512, T)
    tile = pl.BlockSpec((tm, D), lambda i: (i, 0))
    return pl.pallas_call(
        _final_kernel,
        grid=(T // tm,),
        in_specs=[tile, tile, pl.BlockSpec((1, 6, D), _tile_batch_map(tm, n_per_batch, mod.shape[0])),
                  _resident((1, D))],
        out_specs=tile,
        out_shape=jax.ShapeDtypeStruct((T, D), F32),
        compiler_params=_cparams(("arbitrary",)),
        name="final_norm",
    )(xa, xb, mod, g.reshape(1, D))


def _rope_tables(n):
    quarter = DK_B // 4
    t = jnp.arange(n)
    row = (t // GRID_W).astype(F32)
    col = (t % GRID_W).astype(F32)
    inv = ROPE_BASE ** (-jnp.arange(quarter, dtype=F32) / quarter)
    ang = jnp.concatenate([row[:, None] * inv, col[:, None] * inv], axis=-1)
    cos, sin = jnp.cos(ang), jnp.sin(ang)
    cos_t = jnp.concatenate([cos, cos] * (LANES // DK_B), axis=-1)
    sin_t = jnp.concatenate([-sin, sin] * (LANES // DK_B), axis=-1)
    return cos_t, sin_t


def _prep_layer(l, w_in, gla_w_gf, gla_b_gf, gla_w_gb, gla_b_gb, ret_ld_f, ret_ld_b):
    D = w_in.shape[1]
    w = w_in[l]
    s = np.cumsum([0, QK_A, QK_A, V_A, V_A, GATE_RANK, GATE_RANK, QK_B, QK_B, V_B, V_B, W_C, W_C, W_C])
    a_q, a_k, a_v, a_r, a_zf, a_zb, b_q, b_k, b_v, b_g, c_q, c_k, c_v = [w[:, s[i]:s[i + 1]] for i in range(13)]
    zpad = jnp.zeros((D, LANES - 2 * GATE_RANK), w.dtype)
    w1 = jnp.concatenate([a_q, a_k, a_v, a_r, b_q, b_k, b_v, b_g, c_q, c_k, c_v, a_zf, a_zb, zpad],
                         axis=1).astype(MXU)
    wm = w[:, MIX_W:].astype(MXU)
    wf = jnp.zeros((LANES, QK_A), F32).at[:GATE_RANK].set(gla_w_gf[l])
    wb = jnp.zeros((LANES, QK_A), F32).at[GATE_RANK:2 * GATE_RANK].set(gla_w_gb[l])
    ld = lambda v: jnp.broadcast_to(v[l].reshape(H_B // 2, 2, 1), (H_B // 2, 2, 2 * LANES))
    return dict(w1=w1, wm=wm, wf=wf, bf=gla_b_gf[l].reshape(1, QK_A), wb=wb, bb=gla_b_gb[l].reshape(1, QK_A),
                ldf=ld(ret_ld_f), ldb=ld(ret_ld_b))


def kernel(x_prompt, x_sample, cache_nat_k, cache_nat_v, state_gla_fwd, state_gla_bwd, state_ret_fwd,
           state_ret_bwd, c, c_ctx, norm1, norm2, w_ada, b_ada, w_in, gla_w_gf, gla_b_gf, gla_w_gb, gla_b_gb,
           gla_gn, ret_log_decay_f, ret_log_decay_b, ret_gn, nat_rpb, w_br_a, w_br_b, w_br_c, w_out, w_router,
           w_gate, w_up, w_down, final_norm):
    B, n_ctx, D = x_prompt.shape
    Bd, n_lat, _ = x_sample.shape
    L = w_in.shape[0]
    past = cache_nat_k.shape[2]

    cond = jnp.concatenate([c_ctx[None, :], c], axis=0)
    n_cond = cond.shape[0]
    cond = jnp.pad(cond, ((0, -n_cond % 8), (0, 0)))
    mod_all = _ada(cond, w_ada, b_ada).reshape(L, cond.shape[0], 6, D)
    rope_tabs = _rope_tables(n_lat)
    zero_a = jnp.zeros((B, H_A, DK_A, DV_A), F32)
    zero_b = jnp.zeros((B, H_B, DK_B, DV_B), F32)

    paths = {
        "ctx": dict(x=x_prompt.reshape(B * n_ctx, D), moe=None, B=B, n=n_ctx),
        "lat": dict(x=x_sample.reshape(Bd * n_lat, D), moe=None, B=Bd, n=n_lat),
    }
    outs = dict(nk=[], nv=[], gf=[], gb=[], rf=[], rb=[])
    mod_prev = {}
    for l in range(L):
        w = _prep_layer(l, w_in, gla_w_gf, gla_b_gf, gla_w_gb, gla_b_gb, ret_log_decay_f, ret_log_decay_b)
        wa, wb_, wc = w_br_a[l].astype(MXU), w_br_b[l].astype(MXU), w_br_c[l].astype(MXU)
        wo = w_out[l].astype(MXU)
        wr_t = w_router[l].T.astype(MXU)
        wg, wu, wd = w_gate[l].astype(MXU), w_up[l].astype(MXU), w_down[l].astype(MXU)
        gn_a, gn_b = gla_gn[l].reshape(1, DV_A), ret_gn[l].reshape(1, DV_B)
        bias = _window_bias(nat_rpb[l].astype(F32), n_lat // GRID_W)
        mods = {"ctx": mod_all[l, 0:1], "lat": mod_all[l, 1:n_cond]}
        for name, st in paths.items():
            Bp, n = st["B"], st["n"]
            latent = name == "lat"
            mod = mods[name]
            P, x = _in_proj(st["x"], st["moe"], mod_prev.get(name), mod, norm1[l], w["w1"], n)
            if latent:
                sa_f0, sa_b0 = state_gla_fwd[:, l].astype(F32), state_gla_bwd[:, l].astype(F32)
                sb_f0, sb_b0 = state_ret_fwd[:, l].astype(F32), state_ret_bwd[:, l].astype(F32)
            else:
                sa_f0 = sa_b0 = zero_a
                sb_f0 = sb_b0 = zero_b
            o_a, sa_f, sa_b = _gla(P, Bp, n, w["wf"], w["bf"], w["wb"], w["bb"], gn_a, sa_f0, sa_b0)
            o_b, sb_f, sb_b = _ret(P, Bp, n, w["ldf"], w["ldb"], gn_b, sb_f0, sb_b0,
                                   rope_tabs if latent else None)
            if latent:
                k_ctx = cache_nat_k[:, l].astype(F32).reshape(Bp, past, W_C)
                v_ctx = cache_nat_v[:, l].astype(F32).reshape(Bp, past, W_C)
                o_c = _attn_lat(P, Bp, n, k_ctx, v_ctx, bias)
            else:
                o_c = _attn_ctx(P, Bp, n)
                outs["nk"].append(P[:, C_CK:C_CK + W_C].reshape(Bp, n, H_C, D_C))
                outs["nv"].append(P[:, C_CV:C_CV + W_C].reshape(Bp, n, H_C, D_C))
                outs["gf"].append(sa_f)
                outs["gb"].append(sa_b)
                outs["rf"].append(sb_f)
                outs["rb"].append(sb_b)
            merged = _merge(x, mod, norm1[l], o_a, o_b, o_c, w["wm"], wa, wb_, wc, n)
            x1, h2, aff_t = _post(x, merged, mod, norm2[l], wo, wr_t, n)
            cap = EC_FACTOR * (Bp * n) // N_EXPERTS
            idx, gate = _select(aff_t, cap)
            st["moe"] = _moe(h2, idx[:, :, 0], gate, wg, wu, wd)
            st["x"] = x1
            mod_prev[name] = mod
    y = {name: _final(st["x"], st["moe"], mod_prev[name], final_norm, st["n"]) for name, st in paths.items()}
    stack = lambda xs: jnp.stack(xs, axis=1)
    return (y["ctx"].reshape(B, n_ctx, D), y["lat"].reshape(Bd, n_lat, D), stack(outs["nk"]), stack(outs["nv"]),
            stack(outs["gf"]), stack(outs["gb"]), stack(outs["rf"]), stack(outs["rb"]))
```

```python
import functools

import numpy as np
import jax
import jax.numpy as jnp
from jax import lax
from jax.experimental import pallas as pl
from jax.experimental.pallas import tpu as pltpu

F32 = jnp.float32
I32 = jnp.int32
MXU = jnp.bfloat16

GRID_W = 64
H_A, DK_A, DV_A = 4, 64, 128
GATE_RANK, GATE_TEMP = 16, 16.0
H_B, DK_B, DV_B = 4, 64, 128
H_C, D_C = 8, 64
WIN_H, WIN_W = 8, 16
CHUNK = 64
N_EXPERTS, EC_FACTOR = 16, 2
ROPE_BASE = 10000.0
EPS = 1e-6
NEG_INF = -1e30

QK_A, V_A = H_A * DK_A, H_A * DV_A
QK_B, V_B = H_B * DK_B, H_B * DV_B
W_C = H_C * D_C
C_AQ, C_AK, C_AV, C_AR = 0, 256, 512, 1024
C_BQ, C_BK, C_BV, C_BG = 1536, 1792, 2048, 2560
C_CQ, C_CK, C_CV, C_Z = 3072, 3584, 4096, 4608
C1 = 4736
MIX_W = 4640

LANES = 128
SUPER = 256
NCH = SUPER // CHUNK
Q_ROWS = 4
K_ROWS = Q_ROWS + WIN_H
VMEM_LIMIT = 56 * 1024 * 1024


def _cparams(sem):
    return pltpu.CompilerParams(dimension_semantics=sem, vmem_limit_bytes=VMEM_LIMIT)


def _mm(a, b):
    return jnp.dot(a.astype(MXU), b.astype(MXU), preferred_element_type=F32)


def _mm_nt(a, b):
    return lax.dot_general(a.astype(MXU), b.astype(MXU), (((1,), (1,)), ((), ())), preferred_element_type=F32)


def _iota(shape, dim):
    return lax.broadcasted_iota(I32, shape, dim)


def _sigmoid(x):
    return 1.0 / (1.0 + jnp.exp(-x))


def _silu(x):
    return x * _sigmoid(x)


def _resident(shape):
    nd = len(shape)
    return pl.BlockSpec(shape, lambda *_: (0,) * nd, pipeline_mode=pl.Buffered(1))


def _ada_kernel(cond_ref, w_ref, b_ref, o_ref):
    o_ref[0] = _mm(_silu(cond_ref[...]), w_ref[0]) + b_ref[0]


def _ada(cond, w_ada, b_ada):
    L, D, D6 = w_ada.shape
    R = cond.shape[0]
    tn = 1024 if D6 % 1024 == 0 else D
    assert D6 % tn == 0
    return pl.pallas_call(
        _ada_kernel,
        grid=(L, D6 // tn),
        in_specs=[pl.BlockSpec((R, D), lambda l, j: (0, 0)),
                  pl.BlockSpec((1, D, tn), lambda l, j: (l, 0, j)),
                  pl.BlockSpec((1, 1, tn), lambda l, j: (l, 0, j))],
        out_specs=pl.BlockSpec((1, R, tn), lambda l, j: (l, 0, j)),
        out_shape=jax.ShapeDtypeStruct((L, R, D6), F32),
        compiler_params=_cparams(("arbitrary", "arbitrary")),
        name="ada",
    )(cond, w_ada, b_ada.reshape(L, 1, D6))


def _norm_mod(x, g, shift, scale):
    y = x * lax.rsqrt(jnp.mean(x * x, axis=-1, keepdims=True) + EPS) * g
    return y * (1.0 + scale) + shift


def _tile_batch_map(tm, n_per_batch, nb):
    if nb == 1:
        return lambda i: (0, 0, 0)
    assert n_per_batch % tm == 0
    per = n_per_batch // tm
    return lambda i: (i // per, 0, 0)


def _col_chunks(width, step):
    return [(s, min(step, width - s)) for s in range(0, width, step)]


def _in_kernel(*refs, has_res):
    if has_res:
        xa_ref, xb_ref, modp_ref, mod_ref, g_ref, w_ref, p_ref, x_ref = refs
        x = xa_ref[...] + modp_ref[0, 5:6, :] * xb_ref[...]
        x_ref[...] = x
    else:
        xa_ref, mod_ref, g_ref, w_ref, p_ref = refs
        x = xa_ref[...]
    h = _norm_mod(x, g_ref[...], mod_ref[0, 0:1, :], mod_ref[0, 1:2, :]).astype(MXU)
    for s, w in _col_chunks(w_ref.shape[1], 512):
        p_ref[:, s:s + w] = jnp.dot(h, w_ref[:, s:s + w], preferred_element_type=F32)


def _in_proj(xa, xb, mod_prev, mod, g, w1, n_per_batch):
    T, D = xa.shape
    nb = mod.shape[0]
    tm = min(256, T)
    bmap = _tile_batch_map(tm, n_per_batch, nb)
    tile = pl.BlockSpec((tm, D), lambda i: (i, 0))
    mspec = pl.BlockSpec((1, 6, D), bmap)
    has_res = xb is not None
    ins = [xa] + ([xb, mod_prev] if has_res else []) + [mod, g.reshape(1, D), w1]
    in_specs = [tile] + ([tile, mspec] if has_res else []) + [mspec, _resident((1, D)), _resident(w1.shape)]
    p_shape = jax.ShapeDtypeStruct((T, w1.shape[1]), F32)
    p_spec = pl.BlockSpec((tm, w1.shape[1]), lambda i: (i, 0))
    out = pl.pallas_call(
        functools.partial(_in_kernel, has_res=has_res),
        grid=(T // tm,),
        in_specs=in_specs,
        out_specs=(p_spec, tile) if has_res else p_spec,
        out_shape=(p_shape, jax.ShapeDtypeStruct((T, D), F32)) if has_res else p_shape,
        compiler_params=_cparams(("arbitrary",)),
        name="in_proj",
    )(*ins)
    return out if has_res else (out, xa)


def _split3(x):
    p1 = x.astype(MXU)
    r1 = x - p1.astype(F32)
    p2 = r1.astype(MXU)
    p3 = (r1 - p2.astype(F32)).astype(MXU)
    return p1, p2, p3


def _chunk_masks():
    row, col = _iota((SUPER, SUPER), 0), _iota((SUPER, SUPER), 1)
    same = (row // CHUNK) == (col // CHUNK)
    return same, same & (col <= row), same & (col >= row)


def _head_select(h):
    sel = (_iota((DK_A, LANES), 1) == _iota((DK_A, LANES), 0) + DK_A * h).astype(MXU)
    rep = (_iota((LANES, SUPER), 0) == (_iota((LANES, SUPER), 1) % CHUNK) + DK_A * h).astype(MXU)
    return sel, rep


def _recur_step(q_in, k_dec, att, v, state, dec_fn, same, fwd, sel, rep):
    o_intra = _mm(att, v)
    kdec_t = _mm_nt(sel, k_dec)
    kblk = jnp.where(same, jnp.concatenate([kdec_t] * NCH, axis=0), 0.0)
    upd = _mm(kblk, v)
    order = range(NCH) if fwd else range(NCH - 1, -1, -1)
    prev = [None] * NCH
    for c in order:
        prev[c] = state
        state = dec_fn(c) * state + upd[c * CHUNK:(c + 1) * CHUNK]
    q_blk = jnp.where(same, _mm(q_in, rep), 0.0)
    o_inter = _mm(q_blk, jnp.concatenate(prev, axis=0))
    return o_intra + o_inter, state


def _head_norm_gate(o, g, r):
    return o * lax.rsqrt(jnp.mean(o * o, axis=-1, keepdims=True) + EPS) * g * _silu(r)


def _gla_kernel(q_ref, k_ref, v_ref, r_ref, z_ref, wf_ref, bf_ref, wb_ref, bb_ref, gn_ref, s0f_ref, s0b_ref,
                o_ref, sf_ref, sb_ref, acc_ref, *, n_sc):
    same, tri_f, tri_b = _chunk_masks()
    sels = [_head_select(h) for h in range(2)]
    lane_head = _iota((1, LANES), 1) // DK_A

    def run(sc, states, fwd):
        off = pl.multiple_of(sc * SUPER, SUPER)
        rows = pl.ds(off, SUPER)
        q = q_ref[rows, :] * (DK_A ** -0.5)
        k = k_ref[rows, :]
        x = _mm(z_ref[rows, :], (wf_ref if fwd else wb_ref)[...]) + (bf_ref if fwd else bb_ref)[...]
        la = (jnp.minimum(x, 0.0) - jnp.log1p(jnp.exp(-jnp.abs(x)))) * (1.0 / GATE_TEMP)
        tri = tri_f if fwd else tri_b
        hi = la.astype(MXU)
        lo = la - hi.astype(F32)
        trim = tri.astype(MXU)
        b = _mm(trim, hi) + _mm(trim, lo)
        last = CHUNK - 1 if fwd else 0
        tot_rows = [b[c * CHUNK + last:c * CHUNK + last + 1, :] for c in range(NCH)]
        b_last = jnp.concatenate([jnp.broadcast_to(t, (CHUNK, LANES)) for t in tot_rows], axis=0)
        tot8 = jnp.concatenate(tot_rows + [jnp.zeros((8 - NCH, LANES), F32)], axis=0)
        parts = _split3(tot8)
        q_in = q * jnp.exp(b)
        k_out = k * jnp.exp(-b)
        k_dec = k * jnp.exp(b_last - b)
        outs, new_states = [], []
        for h in range(2):
            sel, rep = sels[h]
            tot_t = sum(_mm_nt(sel, p) for p in parts)
            dec_fn = lambda c, tot_t=tot_t: jnp.exp(jnp.broadcast_to(tot_t[:, c:c + 1], (DK_A, DV_A)))
            qh = jnp.where(lane_head == h, q_in, 0.0)
            att = jnp.where(tri, _mm_nt(qh, k_out), 0.0)
            v = v_ref[rows, h * DV_A:(h + 1) * DV_A]
            o, s = _recur_step(qh, k_dec, att, v, states[h], dec_fn, same, fwd, sel, rep)
            outs.append(o)
            new_states.append(s)
        return rows, outs, tuple(new_states)

    def fwd_body(i, states):
        rows, outs, states = run(i, states, True)
        for h in range(2):
            acc_ref[rows, h * DV_A:(h + 1) * DV_A] = outs[h]
        return states

    def bwd_body(i, states):
        rows, outs, states = run(n_sc - 1 - i, states, False)
        for h in range(2):
            cols = slice(h * DV_A, (h + 1) * DV_A)
            o = _head_norm_gate(acc_ref[rows, cols] + outs[h], gn_ref[...], r_ref[rows, cols])
            o_ref[rows, cols] = o.astype(o_ref.dtype)
        return states

    sf = lax.fori_loop(0, n_sc, fwd_body, (s0f_ref[0, 0], s0f_ref[0, 1]))
    sb = lax.fori_loop(0, n_sc, bwd_body, (s0b_ref[0, 0], s0b_ref[0, 1]))
    for h in range(2):
        sf_ref[0, h] = sf[h]
        sb_ref[0, h] = sb[h]


def _pair_specs(n, col_q, col_k, col_v, col_r):
    return [pl.BlockSpec((n, LANES), lambda b, p: (b, col_q // LANES + p)),
            pl.BlockSpec((n, LANES), lambda b, p: (b, col_k // LANES + p)),
            pl.BlockSpec((n, 2 * DV_A), lambda b, p: (b, col_v // (2 * DV_A) + p)),
            pl.BlockSpec((n, 2 * DV_A), lambda b, p: (b, col_r // (2 * DV_A) + p))]


def _state_spec():
    return pl.BlockSpec((1, 2, DK_A, DV_A), lambda b, p: (b, p, 0, 0))


def _gla(P, B, n, wf, bf, wb, bb, gn, s0f, s0b):
    T = B * n
    assert n % SUPER == 0
    pair_w = pl.BlockSpec((LANES, LANES), lambda b, p: (0, p))
    pair_b = pl.BlockSpec((1, LANES), lambda b, p: (0, p))
    st_shape = jax.ShapeDtypeStruct((B, H_A, DK_A, DV_A), F32)
    return pl.pallas_call(
        functools.partial(_gla_kernel, n_sc=n // SUPER),
        grid=(B, H_A // 2),
        in_specs=_pair_specs(n, C_AQ, C_AK, C_AV, C_AR)
        + [pl.BlockSpec((n, LANES), lambda b, p: (b, C_Z // LANES)),
           pair_w, pair_b, pair_w, pair_b, _resident((1, DV_A)), _state_spec(), _state_spec()],
        out_specs=(pl.BlockSpec((n, 2 * DV_A), lambda b, p: (b, p)), _state_spec(), _state_spec()),
        out_shape=(jax.ShapeDtypeStruct((T, V_A), MXU), st_shape, st_shape),
        scratch_shapes=[pltpu.VMEM((n, 2 * DV_A), F32)],
        compiler_params=_cparams(("arbitrary", "arbitrary")),
        name="gla",
    )(P, P, P, P, P, wf, bf, wb, bb, gn, s0f, s0b)


def _ret_kernel(*refs, n_sc, rope):
    if rope:
        (q_ref, k_ref, v_ref, g_ref, ldf_ref, ldb_ref, gn_ref, s0f_ref, s0b_ref, cos_ref, sin_ref,
         o_ref, sf_ref, sb_ref, acc_ref) = refs
    else:
        (q_ref, k_ref, v_ref, g_ref, ldf_ref, ldb_ref, gn_ref, s0f_ref, s0b_ref,
         o_ref, sf_ref, sb_ref, acc_ref) = refs
    same, tri_f, tri_b = _chunk_masks()
    sels = [_head_select(h) for h in range(2)]
    lane_head = _iota((1, LANES), 1) // DK_B
    half = DK_B // 2
    first_half = (_iota((1, LANES), 1) % DK_B) < half
    diff = (_iota((SUPER, SUPER), 0) - _iota((SUPER, SUPER), 1)).astype(F32)
    pos = (_iota((SUPER, LANES), 0) % CHUNK).astype(F32)

    def rotate(x, rows):
        swapped = jnp.where(first_half, pltpu.roll(x, LANES - half, 1), pltpu.roll(x, half, 1))
        return x * cos_ref[rows, :] + swapped * sin_ref[rows, :]

    def run(sc, states, fwd):
        off = pl.multiple_of(sc * SUPER, SUPER)
        rows = pl.ds(off, SUPER)
        q = q_ref[rows, :]
        k = k_ref[rows, :] * (DK_B ** -0.5)
        if rope:
            q, k = rotate(q, rows), rotate(k, rows)
        ld = (ldf_ref if fwd else ldb_ref)[0]
        lg_lane = jnp.where(lane_head == 0, ld[0:1, :LANES], ld[1:2, :LANES])
        if fwd:
            q_in = q * jnp.exp(lg_lane * (pos + 1.0))
            k_dec = k * jnp.exp(lg_lane * (CHUNK - 1.0 - pos))
        else:
            q_in = q * jnp.exp(lg_lane * (CHUNK - pos))
            k_dec = k * jnp.exp(lg_lane * pos)
        tri = tri_f if fwd else tri_b
        outs, new_states = [], []
        for h in range(2):
            sel, rep = sels[h]
            lg = ld[h:h + 1, :]
            decay = jnp.where(tri, jnp.exp(lg * jnp.abs(diff)), 0.0)
            qh = jnp.where(lane_head == h, q, 0.0)
            att = _mm_nt(qh, k) * decay
            chunk_dec = jnp.exp(lg[:, :DV_B] * float(CHUNK))
            v = v_ref[rows, h * DV_B:(h + 1) * DV_B]
            o, s = _recur_step(q_in, k_dec, att, v, states[h], lambda c: chunk_dec, same, fwd, sel, rep)
            outs.append(o)
            new_states.append(s)
        return rows, outs, tuple(new_states)

    def fwd_body(i, states):
        rows, outs, states = run(i, states, True)
        for h in range(2):
            acc_ref[rows, h * DV_B:(h + 1) * DV_B] = outs[h]
        return states

    def bwd_body(i, states):
        rows, outs, states = run(n_sc - 1 - i, states, False)
        for h in range(2):
            cols = slice(h * DV_B, (h + 1) * DV_B)
            o = _head_norm_gate(acc_ref[rows, cols] + outs[h], gn_ref[...], g_ref[rows, cols])
            o_ref[rows, cols] = o.astype(o_ref.dtype)
        return states

    sf = lax.fori_loop(0, n_sc, fwd_body, (s0f_ref[0, 0], s0f_ref[0, 1]))
    sb = lax.fori_loop(0, n_sc, bwd_body, (s0b_ref[0, 0], s0b_ref[0, 1]))
    for h in range(2):
        sf_ref[0, h] = sf[h]
        sb_ref[0, h] = sb[h]


def _ret(P, B, n, ldf, ldb, gn, s0f, s0b, rope_tabs):
    T = B * n
    rope = rope_tabs is not None
    ld_spec = pl.BlockSpec((1, 2, 2 * LANES), lambda b, p: (p, 0, 0))
    st_shape = jax.ShapeDtypeStruct((B, H_B, DK_B, DV_B), F32)
    ins = [P, P, P, P, ldf, ldb, gn, s0f, s0b] + (list(rope_tabs) if rope else [])
    return pl.pallas_call(
        functools.partial(_ret_kernel, n_sc=n // SUPER, rope=rope),
        grid=(B, H_B // 2),
        in_specs=_pair_specs(n, C_BQ, C_BK, C_BV, C_BG)
        + [ld_spec, ld_spec, _resident((1, DV_B)), _state_spec(), _state_spec()]
        + ([_resident((n, LANES))] * 2 if rope else []),
        out_specs=(pl.BlockSpec((n, 2 * DV_B), lambda b, p: (b, p)), _state_spec(), _state_spec()),
        out_shape=(jax.ShapeDtypeStruct((T, V_B), MXU), st_shape, st_shape),
        scratch_shapes=[pltpu.VMEM((n, 2 * DV_B), F32)],
        compiler_params=_cparams(("arbitrary", "arbitrary")),
        name="retention",
    )(*ins)


def _attn_ctx_kernel(q_ref, k_ref, v_ref, o_ref):
    lane_head = _iota((1, LANES), 1) // D_C
    q = q_ref[...] * (D_C ** -0.5)
    k = k_ref[...]
    v = v_ref[...]
    out = jnp.zeros(q.shape, F32)
    for h in range(2):
        s = _mm_nt(jnp.where(lane_head == h, q, 0.0), k)
        e = jnp.exp(s - jnp.max(s, axis=-1, keepdims=True))
        p = e / jnp.sum(e, axis=-1, keepdims=True)
        out = jnp.where(lane_head == h, _mm(p, v), out)
    o_ref[...] = out.astype(o_ref.dtype)


def _attn_ctx(P, B, n):
    spec = lambda col: pl.BlockSpec((n, LANES), lambda b, p: (b, col // LANES + p))
    return pl.pallas_call(
        _attn_ctx_kernel,
        grid=(B, H_C // 2),
        in_specs=[spec(C_CQ), spec(C_CK), spec(C_CV)],
        out_specs=pl.BlockSpec((n, LANES), lambda b, p: (b, p)),
        out_shape=jax.ShapeDtypeStruct((B * n, W_C), MXU),
        compiler_params=_cparams(("arbitrary", "arbitrary")),
        name="attn_ctx",
    )(P, P, P)


def _attn_lat_kernel(q_ref, k_ref, v_ref, kc_ref, vc_ref, bias_ref, o_ref, *, n_blk, key_start_max):
    lane_head = _iota((1, LANES), 1) // D_C
    nq, nk = Q_ROWS * GRID_W, K_ROWS * GRID_W
    kc = kc_ref[0]
    vc = vc_ref[0]

    def body(i, carry):
        qrows = pl.ds(pl.multiple_of(i * nq, nq), nq)
        kstart = jnp.clip(i * Q_ROWS - WIN_H // 2, 0, key_start_max)
        krows = pl.ds(pl.multiple_of(kstart * GRID_W, GRID_W), nk)
        variant = jnp.where(i == 0, 0, jnp.where(i == n_blk - 1, 2, 1))
        q = q_ref[qrows, :] * (D_C ** -0.5)
        k = k_ref[krows, :]
        v = v_ref[krows, :]
        out = jnp.zeros((nq, LANES), F32)
        for h in range(2):
            qh = jnp.where(lane_head == h, q, 0.0)
            s_win = _mm_nt(qh, k) + bias_ref[h, variant]
            s_ctx = _mm_nt(qh, kc)
            m = jnp.maximum(jnp.max(s_win, axis=-1, keepdims=True), jnp.max(s_ctx, axis=-1, keepdims=True))
            e_win = jnp.exp(s_win - m)
            e_ctx = jnp.exp(s_ctx - m)
            inv = 1.0 / (jnp.sum(e_win, axis=-1, keepdims=True) + jnp.sum(e_ctx, axis=-1, keepdims=True))
            o = _mm(e_win * inv, v) + _mm(e_ctx * inv, vc)
            out = jnp.where(lane_head == h, o, out)
        o_ref[qrows, :] = out.astype(o_ref.dtype)
        return carry

    lax.fori_loop(0, n_blk, body, 0)


def _window_bias(rpb, rows):
    n_blk = rows // Q_ROWS
    kh = min(WIN_H, rows)
    qc = np.arange(GRID_W)
    win_c = np.clip(qc - WIN_W // 2, 0, GRID_W - WIN_W)
    kc = np.arange(GRID_W)
    col_ok = (kc[None, :] >= win_c[:, None]) & (kc[None, :] < win_c[:, None] + WIN_W)
    coff = np.clip(kc[None, :] - qc[:, None], -(WIN_W - 1), WIN_W - 1) + WIN_W - 1
    col_pick = (coff[:, :, None] == np.arange(2 * WIN_W - 1)).astype(np.float32)
    exact = lax.Precision.HIGHEST
    tabs = []
    for blk in (0, min(1, n_blk - 1), n_blk - 1):
        r = blk * Q_ROWS + np.arange(Q_ROWS)
        kstart = int(np.clip(blk * Q_ROWS - WIN_H // 2, 0, rows - K_ROWS))
        kr = kstart + np.arange(K_ROWS)
        r0 = np.clip(r - WIN_H // 2, 0, rows - kh)
        row_ok = (kr[None, :] >= r0[:, None]) & (kr[None, :] < r0[:, None] + kh)
        roff = np.clip(kr[None, :] - r[:, None] + WIN_H - 1, 0, 2 * WIN_H - 2)
        ok = row_ok[:, None, :, None] & col_ok[None, :, None, :]
        row_pick = (roff[:, :, None] == np.arange(2 * WIN_H - 1)).astype(np.float32)
        by_row = jnp.einsum('qka,hab->hqkb', row_pick, rpb, precision=exact)
        bias = jnp.einsum('hqkb,cdb->hqckd', by_row, col_pick, precision=exact)
        tab = jnp.where(ok[None], bias, NEG_INF)
        tabs.append(tab.reshape(rpb.shape[0], Q_ROWS * GRID_W, K_ROWS * GRID_W))
    return jnp.stack(tabs, axis=1)


def _attn_lat(P, B, n, k_ctx, v_ctx, bias):
    rows = n // GRID_W
    assert rows % Q_ROWS == 0 and rows >= K_ROWS
    L_ctx = k_ctx.shape[1]
    spec = lambda col: pl.BlockSpec((n, LANES), lambda p, b: (b, col // LANES + p))
    cspec = pl.BlockSpec((1, L_ctx, LANES), lambda p, b: (b, 0, p))
    nq, nk = Q_ROWS * GRID_W, K_ROWS * GRID_W
    return pl.pallas_call(
        functools.partial(_attn_lat_kernel, n_blk=rows // Q_ROWS, key_start_max=rows - K_ROWS),
        grid=(H_C // 2, B),
        in_specs=[spec(C_CQ), spec(C_CK), spec(C_CV), cspec, cspec,
                  pl.BlockSpec((2, 3, nq, nk), lambda p, b: (p, 0, 0, 0))],
        out_specs=pl.BlockSpec((n, LANES), lambda p, b: (b, p)),
        out_shape=jax.ShapeDtypeStruct((B * n, W_C), MXU),
        compiler_params=_cparams(("arbitrary", "arbitrary")),
        name="attn_lat",
    )(P, P, P, k_ctx, v_ctx, bias)


def _merge_kernel(x_ref, mod_ref, g_ref, oa_ref, ob_ref, oc_ref, wm_ref, wa_ref, wb_ref, wc_ref, m_ref):
    D = x_ref.shape[1]
    h = _norm_mod(x_ref[...], g_ref[...], mod_ref[0, 0:1, :], mod_ref[0, 1:2, :]).astype(MXU)
    for s, w in _col_chunks(D, 512):
        acc = None
        for i, (o_ref, wbr_ref) in enumerate(((oa_ref, wa_ref), (ob_ref, wb_ref), (oc_ref, wc_ref))):
            gate = _sigmoid(jnp.dot(h, wm_ref[:, i * D + s:i * D + s + w], preferred_element_type=F32))
            term = gate * jnp.dot(o_ref[...], wbr_ref[:, s:s + w], preferred_element_type=F32)
            acc = term if acc is None else acc + term
        m_ref[:, s:s + w] = acc.astype(m_ref.dtype)


def _merge(x, mod, g, oa, ob, oc, wm, wa, wb, wc, n_per_batch):
    T, D = x.shape
    tm = min(256, T)
    bmap = _tile_batch_map(tm, n_per_batch, mod.shape[0])
    tile = lambda w: pl.BlockSpec((tm, w), lambda i: (i, 0))
    return pl.pallas_call(
        _merge_kernel,
        grid=(T // tm,),
        in_specs=[tile(D), pl.BlockSpec((1, 6, D), bmap), _resident((1, D)), tile(V_A), tile(V_B), tile(W_C),
                  _resident(wm.shape), _resident(wa.shape), _resident(wb.shape), _resident(wc.shape)],
        out_specs=tile(D),
        out_shape=jax.ShapeDtypeStruct((T, D), MXU),
        compiler_params=_cparams(("arbitrary",)),
        name="merge",
    )(x, mod, g.reshape(1, D), oa, ob, oc, wm, wa, wb, wc)


def _post_kernel(x_ref, m_ref, mod_ref, g_ref, wo_ref, wr_ref, x1_ref, h2_ref, aff_ref):
    mod = mod_ref[0]
    x1 = x_ref[...] + mod[2:3, :] * jnp.dot(m_ref[...], wo_ref[...], preferred_element_type=F32)
    x1_ref[...] = x1
    h2 = _norm_mod(x1, g_ref[...], mod[3:4, :], mod[4:5, :])
    h2_ref[...] = h2
    logits = _mm_nt(wr_ref[...], h2)
    e = jnp.exp(logits - jnp.max(logits, axis=0, keepdims=True))
    aff_ref[...] = e / jnp.sum(e, axis=0, keepdims=True)


def _post(x, merged, mod, g, wo, wr_t, n_per_batch):
    T, D = x.shape
    tm = min(512, T)
    bmap = _tile_batch_map(tm, n_per_batch, mod.shape[0])
    tile = pl.BlockSpec((tm, D), lambda i: (i, 0))
    return pl.pallas_call(
        _post_kernel,
        grid=(T // tm,),
        in_specs=[tile, tile, pl.BlockSpec((1, 6, D), bmap), _resident((1, D)), _resident(wo.shape),
                  _resident(wr_t.shape)],
        out_specs=(tile, tile, pl.BlockSpec((N_EXPERTS, tm), lambda i: (0, i))),
        out_shape=(jax.ShapeDtypeStruct((T, D), F32), jax.ShapeDtypeStruct((T, D), F32),
                   jax.ShapeDtypeStruct((N_EXPERTS, T), F32)),
        compiler_params=_cparams(("arbitrary",)),
        name="post",
    )(x, merged, mod, g.reshape(1, D), wo, wr_t)


def _prefix_counts(mask, upper, lower_strict):
    within = _mm(mask, upper)
    row_tot = jnp.broadcast_to(within[:, LANES - 1:LANES], within.shape)
    row_start = _mm(lower_strict, row_tot)
    return row_start + within - mask, row_start, within


def _split_int(x):
    high = jnp.floor(x * (1.0 / 256.0))
    return high, x - high * 256.0


def _select_kernel(aff_ref, idx_ref, gate_ref, *, cap, slot_tile):
    R = aff_ref.shape[1]
    upper = (_iota((LANES, LANES), 0) <= _iota((LANES, LANES), 1)).astype(MXU)
    lower_strict = (_iota((R, R), 1) < _iota((R, R), 0)).astype(MXU)
    lane0 = (_iota((8, LANES), 1) == 0).astype(MXU)
    lane_id = _iota((slot_tile, LANES), 1)
    row_id = _iota((slot_tile, R), 1)

    aff = aff_ref[0]
    bits = pltpu.bitcast(aff, I32)

    def bit_step(j, thr):
        cand = thr | jnp.left_shift(jnp.int32(1), 30 - j)
        cnt = jnp.sum((bits >= cand).astype(I32), axis=(0, 1), keepdims=True)
        return jnp.where(cnt >= cap, cand, thr)

    thr = lax.fori_loop(0, 31, bit_step, jnp.zeros((1, 1), I32))
    gt = (bits > thr).astype(F32)
    eq = (bits == thr).astype(F32)
    need = float(cap) - jnp.sum(gt, axis=(0, 1), keepdims=True)
    eq_rank, _, _ = _prefix_counts(eq, upper, lower_strict)
    sel = gt + eq * (eq_rank < need).astype(F32)
    _, row_start, within = _prefix_counts(sel, upper, lower_strict)
    row_end = row_start + jnp.broadcast_to(within[:, LANES - 1:LANES], within.shape)
    end_hi, end_lo = _split_int(row_end)
    row_end_t = (_mm_nt(lane0, end_hi) * 256.0 + _mm_nt(lane0, end_lo))[0:1, :]
    start_hi, start_lo = _split_int(row_start)
    aff_parts = _split3(aff)

    def tile(t, carry):
        base = t * slot_tile
        slot = (base + _iota((slot_tile, 1), 0)).astype(F32)
        row = jnp.sum((row_end_t <= slot).astype(F32), axis=1, keepdims=True)
        onehot = (row_id.astype(F32) == row).astype(MXU)
        start = _mm(onehot, start_hi) * 256.0 + _mm(onehot, start_lo)
        rank = slot - start
        counts = _mm(onehot, within)
        col = jnp.sum((counts <= rank).astype(F32), axis=1, keepdims=True)
        vals = sum(_mm(onehot, p) for p in aff_parts)
        gate = jnp.sum(jnp.where(lane_id.astype(F32) == col, vals, 0.0), axis=1, keepdims=True)
        token = (row * float(LANES) + col).astype(I32)
        out_rows = pl.ds(pl.multiple_of(base, slot_tile), slot_tile)
        idx_ref[0, out_rows, :] = jnp.broadcast_to(token, (slot_tile, LANES))
        gate_ref[0, out_rows, :] = jnp.broadcast_to(gate, (slot_tile, LANES))
        return carry

    lax.fori_loop(0, cap // slot_tile, tile, 0)


def _select(aff_t, cap):
    E, n = aff_t.shape
    assert n % LANES == 0
    R = n // LANES
    slot_tile = min(512, cap)
    shape = (E, cap, LANES)
    return pl.pallas_call(
        functools.partial(_select_kernel, cap=cap, slot_tile=slot_tile),
        grid=(E,),
        in_specs=[pl.BlockSpec((1, R, LANES), lambda e: (e, 0, 0))],
        out_specs=(pl.BlockSpec((1, cap, LANES), lambda e: (e, 0, 0)),
                   pl.BlockSpec((1, cap, LANES), lambda e: (e, 0, 0))),
        out_shape=(jax.ShapeDtypeStruct(shape, I32), jax.ShapeDtypeStruct(shape, F32)),
        compiler_params=_cparams(("arbitrary",)),
        name="select",
    )(aff_t.reshape(E, R, LANES))


MOE_TILE = 512
MOE_ISSUE_UNROLL = 8


def _moe_kernel(idx_ref, idx_next_ref, h_hbm, gate_ref, wg_ref, wu_ref, wd_ref, acc_in, acc_hbm, xbuf, obuf, sem,
                *, ts, per):
    del acc_in
    t = pl.program_id(1)
    step = pl.program_id(0) * per + t
    n_steps = pl.num_programs(0) * per
    cur = lax.rem(step, 2)
    nxt = 1 - cur
    first, last = step == 0, step == n_steps - 1
    expert_start, expert_end = t == 0, t == per - 1

    def issue(ids_ref, kind, buf):
        def body(i, c):
            tok = pl.ds(ids_ref[0, 0, i], 1)
            row = pl.ds(i, 1)
            if kind == 0:
                cp = pltpu.make_async_copy(h_hbm.at[tok, :], xbuf.at[buf, row, :], sem.at[0, buf])
            elif kind == 1:
                cp = pltpu.make_async_copy(acc_hbm.at[tok, :], obuf.at[buf, row, :], sem.at[1, buf])
            else:
                cp = pltpu.make_async_copy(obuf.at[buf, row, :], acc_hbm.at[tok, :], sem.at[2, buf])
            cp.start()
            return c
        lax.fori_loop(0, ts, body, 0, unroll=MOE_ISSUE_UNROLL)

    def wait_all(kind, buf):
        ref = xbuf if kind == 0 else obuf
        pltpu.make_async_copy(ref.at[buf], ref.at[buf], sem.at[kind, buf]).wait()

    @pl.when(first)
    def _():
        issue(idx_ref, 0, cur)

    @pl.when(expert_start)
    def _():
        @pl.when(jnp.logical_not(first))
        def _():
            wait_all(2, nxt)
        issue(idx_ref, 1, cur)

    wait_all(0, cur)
    wait_all(1, cur)

    @pl.when(jnp.logical_not(last))
    def _():
        issue(idx_next_ref, 0, nxt)

    @pl.when(jnp.logical_not(expert_start))
    def _():
        wait_all(2, nxt)

    @pl.when(jnp.logical_not(expert_end))
    def _():
        issue(idx_next_ref, 1, nxt)

    x = xbuf[cur].astype(MXU)
    hidden = _silu(jnp.dot(x, wg_ref[0], preferred_element_type=F32)) * jnp.dot(x, wu_ref[0],
                                                                                preferred_element_type=F32)
    y = jnp.dot(hidden.astype(MXU), wd_ref[0], preferred_element_type=F32)
    g = gate_ref[0]
    for s, w in _col_chunks(y.shape[1], LANES):
        obuf[cur, :, s:s + w] = obuf[cur, :, s:s + w] + y[:, s:s + w] * g
    issue(idx_ref, 2, cur)

    @pl.when(last)
    def _():
        wait_all(2, cur)


def _moe(h2, idx, gate, wg, wu, wd):
    n, D = h2.shape
    E, cap, _ = gate.shape
    FF = wg.shape[2]
    ts = min(MOE_TILE, cap)
    per = cap // ts
    n_steps = E * per
    idx_blocks = idx.reshape(n_steps, 1, ts)
    wspec = lambda shape: pl.BlockSpec((1,) + shape, lambda e, t: (e, 0, 0))
    ids = lambda shift: pl.BlockSpec((1, 1, ts), lambda e, t: (jnp.minimum(e * per + t + shift, n_steps - 1), 0, 0),
                                     memory_space=pltpu.SMEM)
    return pl.pallas_call(
        functools.partial(_moe_kernel, ts=ts, per=per),
        grid=(E, per),
        in_specs=[ids(0), ids(1),
                  pl.BlockSpec(memory_space=pl.ANY),
                  pl.BlockSpec((1, ts, LANES), lambda e, t: (e, t, 0)),
                  wspec((D, FF)), wspec((D, FF)), wspec((FF, D)),
                  pl.BlockSpec(memory_space=pl.ANY)],
        out_specs=pl.BlockSpec(memory_space=pl.ANY),
        out_shape=jax.ShapeDtypeStruct((n, D), F32),
        scratch_shapes=[pltpu.VMEM((2, ts, D), F32), pltpu.VMEM((2, ts, D), F32),
                        pltpu.SemaphoreType.DMA((3, 2))],
        input_output_aliases={7: 0},
        compiler_params=_cparams(("arbitrary", "arbitrary")),
        name="moe",
    )(idx_blocks, idx_blocks, h2, gate, wg, wu, wd, jnp.zeros((n, D), F32))


def _final_kernel(xa_ref, xb_ref, mod_ref, g_ref, o_ref):
    x = xa_ref[...] + mod_ref[0, 5:6, :] * xb_ref[...]
    o_ref[...] = x * lax.rsqrt(jnp.mean(x * x, axis=-1, keepdims=True) + EPS) * g_ref[...]


def _final(xa, xb, mod, g, n_per_batch):
    T, D = xa.shape
    tm = min(512, T)
    tile = pl.BlockSpec((tm, D), lambda i: (i, 0))
    return pl.pallas_call(
        _final_kernel,
        grid=(T // tm,),
        in_specs=[tile, tile, pl.BlockSpec((1, 6, D), _tile_batch_map(tm, n_per_batch, mod.shape[0])),
                  _resident((1, D))],
        out_specs=tile,
        out_shape=jax.ShapeDtypeStruct((T, D), F32),
        compiler_params=_cparams(("arbitrary",)),
        name="final_norm",
    )(xa, xb, mod, g.reshape(1, D))


def _rope_tables(n):
    quarter = DK_B // 4
    t = jnp.arange(n)
    row = (t // GRID_W).astype(F32)
    col = (t % GRID_W).astype(F32)
    inv = ROPE_BASE ** (-jnp.arange(quarter, dtype=F32) / quarter)
    ang = jnp.concatenate([row[:, None] * inv, col[:, None] * inv], axis=-1)
    cos, sin = jnp.cos(ang), jnp.sin(ang)
    cos_t = jnp.concatenate([cos, cos] * (LANES // DK_B), axis=-1)
    sin_t = jnp.concatenate([-sin, sin] * (LANES // DK_B), axis=-1)
    return cos_t, sin_t


def _prep_layer(l, w_in, gla_w_gf, gla_b_gf, gla_w_gb, gla_b_gb, ret_ld_f, ret_ld_b):
    D = w_in.shape[1]
    w = w_in[l]
    s = np.cumsum([0, QK_A, QK_A, V_A, V_A, GATE_RANK, GATE_RANK, QK_B, QK_B, V_B, V_B, W_C, W_C, W_C])
    a_q, a_k, a_v, a_r, a_zf, a_zb, b_q, b_k, b_v, b_g, c_q, c_k, c_v = [w[:, s[i]:s[i + 1]] for i in range(13)]
    zpad = jnp.zeros((D, LANES - 2 * GATE_RANK), w.dtype)
    w1 = jnp.concatenate([a_q, a_k, a_v, a_r, b_q, b_k, b_v, b_g, c_q, c_k, c_v, a_zf, a_zb, zpad],
                         axis=1).astype(MXU)
    wm = w[:, MIX_W:].astype(MXU)
    wf = jnp.zeros((LANES, QK_A), F32).at[:GATE_RANK].set(gla_w_gf[l])
    wb = jnp.zeros((LANES, QK_A), F32).at[GATE_RANK:2 * GATE_RANK].set(gla_w_gb[l])
    ld = lambda v: jnp.broadcast_to(v[l].reshape(H_B // 2, 2, 1), (H_B // 2, 2, 2 * LANES))
    return dict(w1=w1, wm=wm, wf=wf, bf=gla_b_gf[l].reshape(1, QK_A), wb=wb, bb=gla_b_gb[l].reshape(1, QK_A),
                ldf=ld(ret_ld_f), ldb=ld(ret_ld_b))


def kernel(x_prompt, x_sample, cache_nat_k, cache_nat_v, state_gla_fwd, state_gla_bwd, state_ret_fwd,
           state_ret_bwd, c, c_ctx, norm1, norm2, w_ada, b_ada, w_in, gla_w_gf, gla_b_gf, gla_w_gb, gla_b_gb,
           gla_gn, ret_log_decay_f, ret_log_decay_b, ret_gn, nat_rpb, w_br_a, w_br_b, w_br_c, w_out, w_router,
           w_gate, w_up, w_down, final_norm):
    B, n_ctx, D = x_prompt.shape
    Bd, n_lat, _ = x_sample.shape
    L = w_in.shape[0]
    past = cache_nat_k.shape[2]

    cond = jnp.concatenate([c_ctx[None, :], c], axis=0)
    n_cond = cond.shape[0]
    cond = jnp.pad(cond, ((0, -n_cond % 8), (0, 0)))
    mod_all = _ada(cond, w_ada, b_ada).reshape(L, cond.shape[0], 6, D)
    rope_tabs = _rope_tables(n_lat)
    zero_a = jnp.zeros((B, H_A, DK_A, DV_A), F32)
    zero_b = jnp.zeros((B, H_B, DK_B, DV_B), F32)

    paths = {
        "ctx": dict(x=x_prompt.reshape(B * n_ctx, D), moe=None, B=B, n=n_ctx),
        "lat": dict(x=x_sample.reshape(Bd * n_lat, D), moe=None, B=Bd, n=n_lat),
    }
    outs = dict(nk=[], nv=[], gf=[], gb=[], rf=[], rb=[])
    mod_prev = {}
    for l in range(L):
        w = _prep_layer(l, w_in, gla_w_gf, gla_b_gf, gla_w_gb, gla_b_gb, ret_log_decay_f, ret_log_decay_b)
        wa, wb_, wc = w_br_a[l].astype(MXU), w_br_b[l].astype(MXU), w_br_c[l].astype(MXU)
        wo = w_out[l].astype(MXU)
        wr_t = w_router[l].T.astype(MXU)
        wg, wu, wd = w_gate[l].astype(MXU), w_up[l].astype(MXU), w_down[l].astype(MXU)
        gn_a, gn_b = gla_gn[l].reshape(1, DV_A), ret_gn[l].reshape(1, DV_B)
        bias = _window_bias(nat_rpb[l].astype(F32), n_lat // GRID_W)
        mods = {"ctx": mod_all[l, 0:1], "lat": mod_all[l, 1:n_cond]}
        for name, st in paths.items():
            Bp, n = st["B"], st["n"]
            latent = name == "lat"
            mod = mods[name]
            P, x = _in_proj(st["x"], st["moe"], mod_prev.get(name), mod, norm1[l], w["w1"], n)
            if latent:
                sa_f0, sa_b0 = state_gla_fwd[:, l].astype(F32), state_gla_bwd[:, l].astype(F32)
                sb_f0, sb_b0 = state_ret_fwd[:, l].astype(F32), state_ret_bwd[:, l].astype(F32)
            else:
                sa_f0 = sa_b0 = zero_a
                sb_f0 = sb_b0 = zero_b
            o_a, sa_f, sa_b = _gla(P, Bp, n, w["wf"], w["bf"], w["wb"], w["bb"], gn_a, sa_f0, sa_b0)
            o_b, sb_f, sb_b = _ret(P, Bp, n, w["ldf"], w["ldb"], gn_b, sb_f0, sb_b0,
                                   rope_tabs if latent else None)
            if latent:
                k_ctx = cache_nat_k[:, l].astype(F32).reshape(Bp, past, W_C)
                v_ctx = cache_nat_v[:, l].astype(F32).reshape(Bp, past, W_C)
                o_c = _attn_lat(P, Bp, n, k_ctx, v_ctx, bias)
            else:
                o_c = _attn_ctx(P, Bp, n)
                outs["nk"].append(P[:, C_CK:C_CK + W_C].reshape(Bp, n, H_C, D_C))
                outs["nv"].append(P[:, C_CV:C_CV + W_C].reshape(Bp, n, H_C, D_C))
                outs["gf"].append(sa_f)
                outs["gb"].append(sa_b)
                outs["rf"].append(sb_f)
                outs["rb"].append(sb_b)
            merged = _merge(x, mod, norm1[l], o_a, o_b, o_c, w["wm"], wa, wb_, wc, n)
            x1, h2, aff_t = _post(x, merged, mod, norm2[l], wo, wr_t, n)
            cap = EC_FACTOR * (Bp * n) // N_EXPERTS
            idx, gate = _select(aff_t, cap)
            st["moe"] = _moe(h2, idx[:, :, 0], gate, wg, wu, wd)
            st["x"] = x1
            mod_prev[name] = mod
    y = {name: _final(st["x"], st["moe"], mod_prev[name], final_norm, st["n"]) for name, st in paths.items()}
    stack = lambda xs: jnp.stack(xs, axis=1)
    return (y["ctx"].reshape(B, n_ctx, D), y["lat"].reshape(Bd, n_lat, D), stack(outs["nk"]), stack(outs["nv"]),
            stack(outs["gf"]), stack(outs["gb"]), stack(outs["rf"]), stack(outs["rb"]))
```

```python
import functools

import numpy as np
import jax
import jax.numpy as jnp
from jax import lax
from jax.experimental import pallas as pl
from jax.experimental.pallas import tpu as pltpu

F32 = jnp.float32
I32 = jnp.int32
MXU = jnp.bfloat16

GRID_W = 64
H_A, DK_A, DV_A = 4, 64, 128
GATE_RANK, GATE_TEMP = 16, 16.0
H_B, DK_B, DV_B = 4, 64, 128
H_C, D_C = 8, 64
WIN_H, WIN_W = 8, 16
CHUNK = 64
N_EXPERTS, EC_FACTOR = 16, 2
ROPE_BASE = 10000.0
EPS = 1e-6
NEG_INF = -1e30

QK_A, V_A = H_A * DK_A, H_A * DV_A
QK_B, V_B = H_B * DK_B, H_B * DV_B
W_C = H_C * D_C
C_AQ, C_AK, C_AV, C_AR = 0, 256, 512, 1024
C_BQ, C_BK, C_BV, C_BG = 1536, 1792, 2048, 2560
C_CQ, C_CK, C_CV, C_Z = 3072, 3584, 4096, 4608
C1 = 4736
MIX_W = 4640

LANES = 128
SUPER = 256
NCH = SUPER // CHUNK
Q_ROWS = 4
K_ROWS = Q_ROWS + WIN_H
VMEM_LIMIT = 56 * 1024 * 1024


def _cparams(sem):
    return pltpu.CompilerParams(dimension_semantics=sem, vmem_limit_bytes=VMEM_LIMIT)


def _mm(a, b):
    return jnp.dot(a.astype(MXU), b.astype(MXU), preferred_element_type=F32)


def _mm_nt(a, b):
    return lax.dot_general(a.astype(MXU), b.astype(MXU), (((1,), (1,)), ((), ())), preferred_element_type=F32)


def _iota(shape, dim):
    return lax.broadcasted_iota(I32, shape, dim)


def _sigmoid(x):
    return 1.0 / (1.0 + jnp.exp(-x))


def _silu(x):
    return x * _sigmoid(x)


def _resident(shape):
    nd = len(shape)
    return pl.BlockSpec(shape, lambda *_: (0,) * nd, pipeline_mode=pl.Buffered(1))


def _ada_kernel(cond_ref, w_ref, b_ref, o_ref):
    o_ref[0] = _mm(_silu(cond_ref[...]), w_ref[0]) + b_ref[0]


def _ada(cond, w_ada, b_ada):
    L, D, D6 = w_ada.shape
    R = cond.shape[0]
    tn = 1024 if D6 % 1024 == 0 else D
    assert D6 % tn == 0
    return pl.pallas_call(
        _ada_kernel,
        grid=(L, D6 // tn),
        in_specs=[pl.BlockSpec((R, D), lambda l, j: (0, 0)),
                  pl.BlockSpec((1, D, tn), lambda l, j: (l, 0, j)),
                  pl.BlockSpec((1, 1, tn), lambda l, j: (l, 0, j))],
        out_specs=pl.BlockSpec((1, R, tn), lambda l, j: (l, 0, j)),
        out_shape=jax.ShapeDtypeStruct((L, R, D6), F32),
        compiler_params=_cparams(("arbitrary", "arbitrary")),
        name="ada",
    )(cond, w_ada, b_ada.reshape(L, 1, D6))


def _norm_mod(x, g, shift, scale):
    y = x * lax.rsqrt(jnp.mean(x * x, axis=-1, keepdims=True) + EPS) * g
    return y * (1.0 + scale) + shift


def _tile_batch_map(tm, n_per_batch, nb):
    if nb == 1:
        return lambda i: (0, 0, 0)
    assert n_per_batch % tm == 0
    per = n_per_batch // tm
    return lambda i: (i // per, 0, 0)


def _col_chunks(width, step):
    return [(s, min(step, width - s)) for s in range(0, width, step)]


def _tok_rows(D):
    assert D % LANES == 0
    return D // LANES


def _read_token_major(ref, lead, n_tok):
    r = ref.shape[-2] // n_tok
    parts = [ref[lead + (pl.ds(j, n_tok, stride=r), slice(None))] for j in range(r)]
    return jnp.concatenate(parts, axis=1)


def _write_token_major(ref, lead, value):
    n_tok = value.shape[0]
    r = ref.shape[-2] // n_tok
    for j in range(r):
        ref[lead + (pl.ds(j, n_tok, stride=r), slice(None))] = value[:, j * LANES:(j + 1) * LANES]


def _token_major_spec(tm, D):
    return pl.BlockSpec((tm * _tok_rows(D), LANES), lambda i: (i, 0))


def _in_kernel(*refs, has_res, emit_kv):
    refs = list(refs)
    xa_ref = refs.pop(0)
    x = xa_ref[...]
    if has_res:
        xb_ref, modp_ref = refs.pop(0), refs.pop(0)
        x = x + modp_ref[0, 5:6, :] * _read_token_major(xb_ref, (), xa_ref.shape[0])
    mod_ref, g_ref, w_ref, p_ref = refs[:4]
    outs = refs[4:]
    if has_res:
        outs.pop(0)[...] = x
    h = _norm_mod(x, g_ref[...], mod_ref[0, 0:1, :], mod_ref[0, 1:2, :]).astype(MXU)
    for s, w in _col_chunks(w_ref.shape[1], 512):
        p_ref[:, s:s + w] = jnp.dot(h, w_ref[:, s:s + w], preferred_element_type=F32)
    if emit_kv:
        k_ref, v_ref = outs
        k_ref[...] = p_ref[:, C_CK:C_CK + W_C]
        v_ref[...] = p_ref[:, C_CV:C_CV + W_C]


def _in_proj(xa, xb, mod_prev, mod, g, w1, n_per_batch, emit_kv):
    T, D = xa.shape
    nb = mod.shape[0]
    tm = min(256, T)
    bmap = _tile_batch_map(tm, n_per_batch, nb)
    tile = lambda w: pl.BlockSpec((tm, w), lambda i: (i, 0))
    shape = lambda w: jax.ShapeDtypeStruct((T, w), F32)
    mspec = pl.BlockSpec((1, 6, D), bmap)
    has_res = xb is not None
    ins = [xa] + ([xb, mod_prev] if has_res else []) + [mod, g.reshape(1, D), w1]
    in_specs = ([tile(D)] + ([_token_major_spec(tm, D), mspec] if has_res else [])
                + [mspec, _resident((1, D)), _resident(w1.shape)])
    widths = [w1.shape[1]] + ([D] if has_res else []) + ([W_C, W_C] if emit_kv else [])
    out = list(pl.pallas_call(
        functools.partial(_in_kernel, has_res=has_res, emit_kv=emit_kv),
        grid=(T // tm,),
        in_specs=in_specs,
        out_specs=tuple(tile(w) for w in widths),
        out_shape=tuple(shape(w) for w in widths),
        compiler_params=_cparams(("arbitrary",)),
        name="in_proj",
    )(*ins))
    P = out.pop(0)
    x = out.pop(0) if has_res else xa
    k, v = out if emit_kv else (None, None)
    return P, x, k, v


def _split3(x):
    p1 = x.astype(MXU)
    r1 = x - p1.astype(F32)
    p2 = r1.astype(MXU)
    p3 = (r1 - p2.astype(F32)).astype(MXU)
    return p1, p2, p3


def _chunk_masks():
    row, col = _iota((SUPER, SUPER), 0), _iota((SUPER, SUPER), 1)
    same = (row // CHUNK) == (col // CHUNK)
    return same, same & (col <= row), same & (col >= row)


def _head_select(h):
    sel = (_iota((DK_A, LANES), 1) == _iota((DK_A, LANES), 0) + DK_A * h).astype(MXU)
    rep = (_iota((LANES, SUPER), 0) == (_iota((LANES, SUPER), 1) % CHUNK) + DK_A * h).astype(MXU)
    return sel, rep


def _recur_step(q_in, k_dec, att, v, state, dec_fn, same, fwd, sel, rep):
    o_intra = _mm(att, v)
    kdec_t = _mm_nt(sel, k_dec)
    kblk = jnp.where(same, jnp.concatenate([kdec_t] * NCH, axis=0), 0.0)
    upd = _mm(kblk, v)
    order = range(NCH) if fwd else range(NCH - 1, -1, -1)
    prev = [None] * NCH
    for c in order:
        prev[c] = state
        state = dec_fn(c) * state + upd[c * CHUNK:(c + 1) * CHUNK]
    q_blk = jnp.where(same, _mm(q_in, rep), 0.0)
    o_inter = _mm(q_blk, jnp.concatenate(prev, axis=0))
    return o_intra + o_inter, state


def _head_norm_gate(o, g, r):
    return o * lax.rsqrt(jnp.mean(o * o, axis=-1, keepdims=True) + EPS) * g * _silu(r)


def _gla_kernel(q_ref, k_ref, v_ref, r_ref, z_ref, wf_ref, bf_ref, wb_ref, bb_ref, gn_ref, s0f_ref, s0b_ref,
                o_ref, sf_ref, sb_ref, acc_ref, *, n_sc):
    same, tri_f, tri_b = _chunk_masks()
    sels = [_head_select(h) for h in range(2)]
    lane_head = _iota((1, LANES), 1) // DK_A

    def run(sc, states, fwd):
        off = pl.multiple_of(sc * SUPER, SUPER)
        rows = pl.ds(off, SUPER)
        q = q_ref[rows, :] * (DK_A ** -0.5)
        k = k_ref[rows, :]
        x = _mm(z_ref[rows, :], (wf_ref if fwd else wb_ref)[...]) + (bf_ref if fwd else bb_ref)[...]
        la = (jnp.minimum(x, 0.0) - jnp.log1p(jnp.exp(-jnp.abs(x)))) * (1.0 / GATE_TEMP)
        tri = tri_f if fwd else tri_b
        hi = la.astype(MXU)
        lo = la - hi.astype(F32)
        trim = tri.astype(MXU)
        b = _mm(trim, hi) + _mm(trim, lo)
        last = CHUNK - 1 if fwd else 0
        tot_rows = [b[c * CHUNK + last:c * CHUNK + last + 1, :] for c in range(NCH)]
        b_last = jnp.concatenate([jnp.broadcast_to(t, (CHUNK, LANES)) for t in tot_rows], axis=0)
        tot8 = jnp.concatenate(tot_rows + [jnp.zeros((8 - NCH, LANES), F32)], axis=0)
        parts = _split3(tot8)
        q_in = q * jnp.exp(b)
        k_out = k * jnp.exp(-b)
        k_dec = k * jnp.exp(b_last - b)
        outs, new_states = [], []
        for h in range(2):
            sel, rep = sels[h]
            tot_t = sum(_mm_nt(sel, p) for p in parts)
            dec_fn = lambda c, tot_t=tot_t: jnp.exp(jnp.broadcast_to(tot_t[:, c:c + 1], (DK_A, DV_A)))
            qh = jnp.where(lane_head == h, q_in, 0.0)
            att = jnp.where(tri, _mm_nt(qh, k_out), 0.0)
            v = v_ref[rows, h * DV_A:(h + 1) * DV_A]
            o, s = _recur_step(qh, k_dec, att, v, states[h], dec_fn, same, fwd, sel, rep)
            outs.append(o)
            new_states.append(s)
        return rows, outs, tuple(new_states)

    sf, sb = _both_directions(run, n_sc, acc_ref, o_ref, gn_ref, r_ref,
                              (s0f_ref[0, 0], s0f_ref[0, 1]), (s0b_ref[0, 0], s0b_ref[0, 1]))
    for h in range(2):
        sf_ref[0, h] = sf[h]
        sb_ref[0, h] = sb[h]


def _both_directions(run, n_sc, acc_ref, o_ref, gn_ref, gate_ref, sf, sb):
    dv = acc_ref.shape[1] // 2

    def park(rows, outs):
        for h in range(2):
            acc_ref[rows, h * dv:(h + 1) * dv] = outs[h]

    def finish(rows, outs, other=None):
        for h in range(2):
            cols = slice(h * dv, (h + 1) * dv)
            total = outs[h] + (acc_ref[rows, cols] if other is None else other[h])
            o_ref[rows, cols] = _head_norm_gate(total, gn_ref[...], gate_ref[rows, cols]).astype(o_ref.dtype)

    if n_sc == 1:
        rows, outs_f, sf = run(0, sf, True)
        _, outs_b, sb = run(0, sb, False)
        finish(rows, outs_f, outs_b)
        return sf, sb
    assert n_sc % 2 == 0

    def body(second):
        def step(i, carry):
            sf, sb = carry
            rows_f, outs_f, sf = run(i, sf, True)
            rows_b, outs_b, sb = run(n_sc - 1 - i, sb, False)
            (finish if second else park)(rows_f, outs_f)
            (finish if second else park)(rows_b, outs_b)
            return sf, sb
        return step

    carry = lax.fori_loop(0, n_sc // 2, body(False), (sf, sb))
    return lax.fori_loop(n_sc // 2, n_sc, body(True), carry)


def _pair_specs(n, col_q, col_k, col_v, col_r):
    return [pl.BlockSpec((n, LANES), lambda b, p: (b, col_q // LANES + p)),
            pl.BlockSpec((n, LANES), lambda b, p: (b, col_k // LANES + p)),
            pl.BlockSpec((n, 2 * DV_A), lambda b, p: (b, col_v // (2 * DV_A) + p)),
            pl.BlockSpec((n, 2 * DV_A), lambda b, p: (b, col_r // (2 * DV_A) + p))]


def _state_spec():
    return pl.BlockSpec((1, 2, DK_A, DV_A), lambda b, p: (b, p, 0, 0))


def _gla(P, B, n, wf, bf, wb, bb, gn, s0f, s0b):
    T = B * n
    assert n % SUPER == 0
    pair_w = pl.BlockSpec((LANES, LANES), lambda b, p: (0, p))
    pair_b = pl.BlockSpec((1, LANES), lambda b, p: (0, p))
    st_shape = jax.ShapeDtypeStruct((B, H_A, DK_A, DV_A), F32)
    return pl.pallas_call(
        functools.partial(_gla_kernel, n_sc=n // SUPER),
        grid=(B, H_A // 2),
        in_specs=_pair_specs(n, C_AQ, C_AK, C_AV, C_AR)
        + [pl.BlockSpec((n, LANES), lambda b, p: (b, C_Z // LANES)),
           pair_w, pair_b, pair_w, pair_b, _resident((1, DV_A)), _state_spec(), _state_spec()],
        out_specs=(pl.BlockSpec((n, 2 * DV_A), lambda b, p: (b, p)), _state_spec(), _state_spec()),
        out_shape=(jax.ShapeDtypeStruct((T, V_A), MXU), st_shape, st_shape),
        scratch_shapes=[pltpu.VMEM((n, 2 * DV_A), F32)],
        compiler_params=_cparams(("arbitrary", "arbitrary")),
        name="gla",
    )(P, P, P, P, P, wf, bf, wb, bb, gn, s0f, s0b)


def _ret_kernel(*refs, n_sc, rope):
    if rope:
        (q_ref, k_ref, v_ref, g_ref, ldf_ref, ldb_ref, gn_ref, s0f_ref, s0b_ref, cos_ref, sin_ref,
         o_ref, sf_ref, sb_ref, acc_ref) = refs
    else:
        (q_ref, k_ref, v_ref, g_ref, ldf_ref, ldb_ref, gn_ref, s0f_ref, s0b_ref,
         o_ref, sf_ref, sb_ref, acc_ref) = refs
    same, tri_f, tri_b = _chunk_masks()
    sels = [_head_select(h) for h in range(2)]
    lane_head = _iota((1, LANES), 1) // DK_B
    half = DK_B // 2
    first_half = (_iota((1, LANES), 1) % DK_B) < half
    diff = (_iota((SUPER, SUPER), 0) - _iota((SUPER, SUPER), 1)).astype(F32)
    pos = (_iota((SUPER, LANES), 0) % CHUNK).astype(F32)

    def rotate(x, rows):
        swapped = jnp.where(first_half, pltpu.roll(x, LANES - half, 1), pltpu.roll(x, half, 1))
        return x * cos_ref[rows, :] + swapped * sin_ref[rows, :]

    def run(sc, states, fwd):
        off = pl.multiple_of(sc * SUPER, SUPER)
        rows = pl.ds(off, SUPER)
        q = q_ref[rows, :]
        k = k_ref[rows, :] * (DK_B ** -0.5)
        if rope:
            q, k = rotate(q, rows), rotate(k, rows)
        ld = (ldf_ref if fwd else ldb_ref)[0]
        lg_lane = jnp.where(lane_head == 0, ld[0:1, :LANES], ld[1:2, :LANES])
        if fwd:
            q_in = q * jnp.exp(lg_lane * (pos + 1.0))
            k_dec = k * jnp.exp(lg_lane * (CHUNK - 1.0 - pos))
        else:
            q_in = q * jnp.exp(lg_lane * (CHUNK - pos))
            k_dec = k * jnp.exp(lg_lane * pos)
        tri = tri_f if fwd else tri_b
        outs, new_states = [], []
        for h in range(2):
            sel, rep = sels[h]
            lg = ld[h:h + 1, :]
            decay = jnp.where(tri, jnp.exp(lg * jnp.abs(diff)), 0.0)
            qh = jnp.where(lane_head == h, q, 0.0)
            att = _mm_nt(qh, k) * decay
            chunk_dec = jnp.exp(lg[:, :DV_B] * float(CHUNK))
            v = v_ref[rows, h * DV_B:(h + 1) * DV_B]
            o, s = _recur_step(q_in, k_dec, att, v, states[h], lambda c: chunk_dec, same, fwd, sel, rep)
            outs.append(o)
            new_states.append(s)
        return rows, outs, tuple(new_states)

    sf, sb = _both_directions(run, n_sc, acc_ref, o_ref, gn_ref, g_ref,
                              (s0f_ref[0, 0], s0f_ref[0, 1]), (s0b_ref[0, 0], s0b_ref[0, 1]))
    for h in range(2):
        sf_ref[0, h] = sf[h]
        sb_ref[0, h] = sb[h]


def _ret(P, B, n, ldf, ldb, gn, s0f, s0b, rope_tabs):
    T = B * n
    rope = rope_tabs is not None
    ld_spec = pl.BlockSpec((1, 2, 2 * LANES), lambda b, p: (p, 0, 0))
    st_shape = jax.ShapeDtypeStruct((B, H_B, DK_B, DV_B), F32)
    ins = [P, P, P, P, ldf, ldb, gn, s0f, s0b] + (list(rope_tabs) if rope else [])
    return pl.pallas_call(
        functools.partial(_ret_kernel, n_sc=n // SUPER, rope=rope),
        grid=(B, H_B // 2),
        in_specs=_pair_specs(n, C_BQ, C_BK, C_BV, C_BG)
        + [ld_spec, ld_spec, _resident((1, DV_B)), _state_spec(), _state_spec()]
        + ([_resident((n, LANES))] * 2 if rope else []),
        out_specs=(pl.BlockSpec((n, 2 * DV_B), lambda b, p: (b, p)), _state_spec(), _state_spec()),
        out_shape=(jax.ShapeDtypeStruct((T, V_B), MXU), st_shape, st_shape),
        scratch_shapes=[pltpu.VMEM((n, 2 * DV_B), F32)],
        compiler_params=_cparams(("arbitrary", "arbitrary")),
        name="retention",
    )(*ins)


def _attn_ctx_kernel(q_ref, k_ref, v_ref, o_ref):
    lane_head = _iota((1, LANES), 1) // D_C
    q = q_ref[...] * (D_C ** -0.5)
    k = k_ref[...]
    v = v_ref[...]
    out = jnp.zeros(q.shape, F32)
    for h in range(2):
        s = _mm_nt(jnp.where(lane_head == h, q, 0.0), k)
        e = jnp.exp(s - jnp.max(s, axis=-1, keepdims=True))
        p = e / jnp.sum(e, axis=-1, keepdims=True)
        out = jnp.where(lane_head == h, _mm(p, v), out)
    o_ref[...] = out.astype(o_ref.dtype)


def _attn_ctx(P, B, n):
    spec = lambda col: pl.BlockSpec((n, LANES), lambda b, p: (b, col // LANES + p))
    return pl.pallas_call(
        _attn_ctx_kernel,
        grid=(B, H_C // 2),
        in_specs=[spec(C_CQ), spec(C_CK), spec(C_CV)],
        out_specs=pl.BlockSpec((n, LANES), lambda b, p: (b, p)),
        out_shape=jax.ShapeDtypeStruct((B * n, W_C), MXU),
        compiler_params=_cparams(("arbitrary", "arbitrary")),
        name="attn_ctx",
    )(P, P, P)


def _attn_lat_kernel(q_ref, k_ref, v_ref, kc_ref, vc_ref, bias_ref, o_ref, *, n_blk, key_start_max):
    lane_head = _iota((1, LANES), 1) // D_C
    nq, nk = Q_ROWS * GRID_W, K_ROWS * GRID_W
    kc = kc_ref[0]
    vc = vc_ref[0]

    def body(i, carry):
        qrows = pl.ds(pl.multiple_of(i * nq, nq), nq)
        kstart = jnp.clip(i * Q_ROWS - WIN_H // 2, 0, key_start_max)
        krows = pl.ds(pl.multiple_of(kstart * GRID_W, GRID_W), nk)
        variant = jnp.where(i == 0, 0, jnp.where(i == n_blk - 1, 2, 1))
        q = q_ref[qrows, :] * (D_C ** -0.5)
        k = k_ref[krows, :]
        v = v_ref[krows, :]
        out = jnp.zeros((nq, LANES), F32)
        for h in range(2):
            qh = jnp.where(lane_head == h, q, 0.0)
            s_win = _mm_nt(qh, k) + bias_ref[h, variant]
            s_ctx = _mm_nt(qh, kc)
            m = jnp.maximum(jnp.max(s_win, axis=-1, keepdims=True), jnp.max(s_ctx, axis=-1, keepdims=True))
            e_win = jnp.exp(s_win - m)
            e_ctx = jnp.exp(s_ctx - m)
            inv = 1.0 / (jnp.sum(e_win, axis=-1, keepdims=True) + jnp.sum(e_ctx, axis=-1, keepdims=True))
            o = _mm(e_win * inv, v) + _mm(e_ctx * inv, vc)
            out = jnp.where(lane_head == h, o, out)
        o_ref[qrows, :] = out.astype(o_ref.dtype)
        return carry

    lax.fori_loop(0, n_blk, body, 0, unroll=2)


def _window_bias(rpb, rows):
    n_blk = rows // Q_ROWS
    kh = min(WIN_H, rows)
    qc = np.arange(GRID_W)
    win_c = np.clip(qc - WIN_W // 2, 0, GRID_W - WIN_W)
    kc = np.arange(GRID_W)
    col_ok = (kc[None, :] >= win_c[:, None]) & (kc[None, :] < win_c[:, None] + WIN_W)
    coff = np.clip(kc[None, :] - qc[:, None], -(WIN_W - 1), WIN_W - 1) + WIN_W - 1
    col_pick = (coff[:, :, None] == np.arange(2 * WIN_W - 1)).astype(np.float32)
    exact = lax.Precision.HIGHEST
    tabs = []
    for blk in (0, min(1, n_blk - 1), n_blk - 1):
        r = blk * Q_ROWS + np.arange(Q_ROWS)
        kstart = int(np.clip(blk * Q_ROWS - WIN_H // 2, 0, rows - K_ROWS))
        kr = kstart + np.arange(K_ROWS)
        r0 = np.clip(r - WIN_H // 2, 0, rows - kh)
        row_ok = (kr[None, :] >= r0[:, None]) & (kr[None, :] < r0[:, None] + kh)
        roff = np.clip(kr[None, :] - r[:, None] + WIN_H - 1, 0, 2 * WIN_H - 2)
        ok = row_ok[:, None, :, None] & col_ok[None, :, None, :]
        row_pick = (roff[:, :, None] == np.arange(2 * WIN_H - 1)).astype(np.float32)
        by_row = jnp.einsum('qka,hab->hqkb', row_pick, rpb, precision=exact)
        bias = jnp.einsum('hqkb,cdb->hqckd', by_row, col_pick, precision=exact)
        tab = jnp.where(ok[None], bias, NEG_INF)
        tabs.append(tab.reshape(rpb.shape[0], Q_ROWS * GRID_W, K_ROWS * GRID_W))
    return jnp.stack(tabs, axis=1)


def _attn_lat(P, B, n, k_ctx, v_ctx, bias):
    rows = n // GRID_W
    assert rows % Q_ROWS == 0 and rows >= K_ROWS
    L_ctx = k_ctx.shape[1]
    spec = lambda col: pl.BlockSpec((n, LANES), lambda p, b: (b, col // LANES + p))
    cspec = pl.BlockSpec((1, L_ctx, LANES), lambda p, b: (b, 0, p))
    nq, nk = Q_ROWS * GRID_W, K_ROWS * GRID_W
    return pl.pallas_call(
        functools.partial(_attn_lat_kernel, n_blk=rows // Q_ROWS, key_start_max=rows - K_ROWS),
        grid=(H_C // 2, B),
        in_specs=[spec(C_CQ), spec(C_CK), spec(C_CV), cspec, cspec,
                  pl.BlockSpec((2, 3, nq, nk), lambda p, b: (p, 0, 0, 0))],
        out_specs=pl.BlockSpec((n, LANES), lambda p, b: (b, p)),
        out_shape=jax.ShapeDtypeStruct((B * n, W_C), MXU),
        compiler_params=_cparams(("arbitrary", "arbitrary")),
        name="attn_lat",
    )(P, P, P, k_ctx, v_ctx, bias)


def _merge_kernel(x_ref, mod_ref, g_ref, oa_ref, ob_ref, oc_ref, wm_ref, wa_ref, wb_ref, wc_ref, m_ref):
    D = x_ref.shape[1]
    h = _norm_mod(x_ref[...], g_ref[...], mod_ref[0, 0:1, :], mod_ref[0, 1:2, :]).astype(MXU)
    for s, w in _col_chunks(D, 512):
        acc = None
        for i, (o_ref, wbr_ref) in enumerate(((oa_ref, wa_ref), (ob_ref, wb_ref), (oc_ref, wc_ref))):
            gate = _sigmoid(jnp.dot(h, wm_ref[:, i * D + s:i * D + s + w], preferred_element_type=F32))
            term = gate * jnp.dot(o_ref[...], wbr_ref[:, s:s + w], preferred_element_type=F32)
            acc = term if acc is None else acc + term
        m_ref[:, s:s + w] = acc.astype(m_ref.dtype)


def _merge(x, mod, g, oa, ob, oc, wm, wa, wb, wc, n_per_batch):
    T, D = x.shape
    tm = min(256, T)
    bmap = _tile_batch_map(tm, n_per_batch, mod.shape[0])
    tile = lambda w: pl.BlockSpec((tm, w), lambda i: (i, 0))
    return pl.pallas_call(
        _merge_kernel,
        grid=(T // tm,),
        in_specs=[tile(D), pl.BlockSpec((1, 6, D), bmap), _resident((1, D)), tile(V_A), tile(V_B), tile(W_C),
                  _resident(wm.shape), _resident(wa.shape), _resident(wb.shape), _resident(wc.shape)],
        out_specs=tile(D),
        out_shape=jax.ShapeDtypeStruct((T, D), MXU),
        compiler_params=_cparams(("arbitrary",)),
        name="merge",
    )(x, mod, g.reshape(1, D), oa, ob, oc, wm, wa, wb, wc)


def _post_kernel(x_ref, m_ref, mod_ref, g_ref, wo_ref, wr_ref, x1_ref, h2_ref, aff_ref):
    mod = mod_ref[0]
    x1 = x_ref[...] + mod[2:3, :] * jnp.dot(m_ref[...], wo_ref[...], preferred_element_type=F32)
    x1_ref[...] = x1
    h2 = _norm_mod(x1, g_ref[...], mod[3:4, :], mod[4:5, :])
    _write_token_major(h2_ref, (), h2)
    logits = _mm_nt(wr_ref[...], h2)
    e = jnp.exp(logits - jnp.max(logits, axis=0, keepdims=True))
    aff_ref[...] = e / jnp.sum(e, axis=0, keepdims=True)


def _post(x, merged, mod, g, wo, wr_t, n_per_batch):
    T, D = x.shape
    tm = min(512, T)
    bmap = _tile_batch_map(tm, n_per_batch, mod.shape[0])
    tile = pl.BlockSpec((tm, D), lambda i: (i, 0))
    return pl.pallas_call(
        _post_kernel,
        grid=(T // tm,),
        in_specs=[tile, tile, pl.BlockSpec((1, 6, D), bmap), _resident((1, D)), _resident(wo.shape),
                  _resident(wr_t.shape)],
        out_specs=(tile, _token_major_spec(tm, D), pl.BlockSpec((N_EXPERTS, tm), lambda i: (0, i))),
        out_shape=(jax.ShapeDtypeStruct((T, D), F32), jax.ShapeDtypeStruct((T * _tok_rows(D), LANES), F32),
                   jax.ShapeDtypeStruct((N_EXPERTS, T), F32)),
        compiler_params=_cparams(("arbitrary",)),
        name="post",
    )(x, merged, mod, g.reshape(1, D), wo, wr_t)


def _prefix_counts(mask, upper, lower_strict):
    within = _mm(mask, upper)
    row_tot = jnp.broadcast_to(within[:, LANES - 1:LANES], within.shape)
    row_start = _mm(lower_strict, row_tot)
    return row_start + within - mask, row_start, within


def _split_int(x):
    high = jnp.floor(x * (1.0 / 256.0))
    return high, x - high * 256.0


def _select_kernel(aff_ref, idx_ref, gate_ref, *, cap, slot_tile):
    R = aff_ref.shape[1]
    upper = (_iota((LANES, LANES), 0) <= _iota((LANES, LANES), 1)).astype(MXU)
    lower_strict = (_iota((R, R), 1) < _iota((R, R), 0)).astype(MXU)
    lane0 = (_iota((8, LANES), 1) == 0).astype(MXU)
    lane_id = _iota((slot_tile, LANES), 1)
    row_id = _iota((slot_tile, R), 1)

    aff = aff_ref[0]
    bits = pltpu.bitcast(aff, I32)

    def bit_step(j, thr):
        cand = thr | jnp.left_shift(jnp.int32(1), 30 - j)
        cnt = jnp.sum((bits >= cand).astype(I32), axis=(0, 1), keepdims=True)
        return jnp.where(cnt >= cap, cand, thr)

    thr = lax.fori_loop(0, 31, bit_step, jnp.zeros((1, 1), I32))
    gt = (bits > thr).astype(F32)
    eq = (bits == thr).astype(F32)
    need = float(cap) - jnp.sum(gt, axis=(0, 1), keepdims=True)
    eq_rank, _, _ = _prefix_counts(eq, upper, lower_strict)
    sel = gt + eq * (eq_rank < need).astype(F32)
    _, row_start, within = _prefix_counts(sel, upper, lower_strict)
    row_end = row_start + jnp.broadcast_to(within[:, LANES - 1:LANES], within.shape)
    end_hi, end_lo = _split_int(row_end)
    row_end_t = (_mm_nt(lane0, end_hi) * 256.0 + _mm_nt(lane0, end_lo))[0:1, :]
    start_hi, start_lo = _split_int(row_start)
    aff_parts = _split3(aff)

    def tile(t, carry):
        base = t * slot_tile
        slot = (base + _iota((slot_tile, 1), 0)).astype(F32)
        row = jnp.sum((row_end_t <= slot).astype(F32), axis=1, keepdims=True)
        onehot = (row_id.astype(F32) == row).astype(MXU)
        start = _mm(onehot, start_hi) * 256.0 + _mm(onehot, start_lo)
        rank = slot - start
        counts = _mm(onehot, within)
        col = jnp.sum((counts <= rank).astype(F32), axis=1, keepdims=True)
        vals = sum(_mm(onehot, p) for p in aff_parts)
        gate = jnp.sum(jnp.where(lane_id.astype(F32) == col, vals, 0.0), axis=1, keepdims=True)
        token = (row * float(LANES) + col).astype(I32)
        out_rows = pl.ds(pl.multiple_of(base, slot_tile), slot_tile)
        idx_ref[0, out_rows, :] = jnp.broadcast_to(token, (slot_tile, LANES))
        gate_ref[0, out_rows, :] = jnp.broadcast_to(gate, (slot_tile, LANES))
        return carry

    lax.fori_loop(0, cap // slot_tile, tile, 0)


def _select(aff_t, cap):
    E, n = aff_t.shape
    assert n % LANES == 0
    R = n // LANES
    slot_tile = min(512, cap)
    shape = (E, cap, LANES)
    return pl.pallas_call(
        functools.partial(_select_kernel, cap=cap, slot_tile=slot_tile),
        grid=(E,),
        in_specs=[pl.BlockSpec((1, R, LANES), lambda e: (e, 0, 0))],
        out_specs=(pl.BlockSpec((1, cap, LANES), lambda e: (e, 0, 0)),
                   pl.BlockSpec((1, cap, LANES), lambda e: (e, 0, 0))),
        out_shape=(jax.ShapeDtypeStruct(shape, I32), jax.ShapeDtypeStruct(shape, F32)),
        compiler_params=_cparams(("arbitrary",)),
        name="select",
    )(aff_t.reshape(E, R, LANES))


MOE_TILE = 512
MOE_ISSUE_UNROLL = 8


def _moe_kernel(idx_ref, idx_next_ref, h_hbm, gate_ref, wg_ref, wu_ref, wd_ref, acc_in, acc_hbm, xbuf, obuf, sem,
                *, ts, per):
    del acc_in
    tok_rows = xbuf.shape[1] // ts
    t = pl.program_id(1)
    step = pl.program_id(0) * per + t
    n_steps = pl.num_programs(0) * per
    cur = lax.rem(step, 2)
    nxt = 1 - cur
    first, last = step == 0, step == n_steps - 1
    expert_start, expert_end = t == 0, t == per - 1

    def issue(ids_ref, kind, buf):
        def body(i, c):
            tok = pl.ds(pl.multiple_of(ids_ref[0, 0, i] * tok_rows, tok_rows), tok_rows)
            row = pl.ds(pl.multiple_of(i * tok_rows, tok_rows), tok_rows)
            if kind == 0:
                cp = pltpu.make_async_copy(h_hbm.at[tok, :], xbuf.at[buf, row, :], sem.at[0, buf])
            elif kind == 1:
                cp = pltpu.make_async_copy(acc_hbm.at[tok, :], obuf.at[buf, row, :], sem.at[1, buf])
            else:
                cp = pltpu.make_async_copy(obuf.at[buf, row, :], acc_hbm.at[tok, :], sem.at[2, buf])
            cp.start()
            return c
        lax.fori_loop(0, ts, body, 0, unroll=MOE_ISSUE_UNROLL)

    def wait_all(kind, buf):
        ref = xbuf if kind == 0 else obuf
        pltpu.make_async_copy(ref.at[buf], ref.at[buf], sem.at[kind, buf]).wait()

    @pl.when(first)
    def _():
        issue(idx_ref, 0, cur)

    @pl.when(expert_start)
    def _():
        @pl.when(jnp.logical_not(first))
        def _():
            wait_all(2, nxt)
        issue(idx_ref, 1, cur)

    wait_all(0, cur)
    wait_all(1, cur)

    @pl.when(jnp.logical_not(last))
    def _():
        issue(idx_next_ref, 0, nxt)

    @pl.when(jnp.logical_not(expert_start))
    def _():
        wait_all(2, nxt)

    @pl.when(jnp.logical_not(expert_end))
    def _():
        issue(idx_next_ref, 1, nxt)

    x = _read_token_major(xbuf, (cur,), ts).astype(MXU)
    hidden = _silu(jnp.dot(x, wg_ref[0], preferred_element_type=F32)) * jnp.dot(x, wu_ref[0],
                                                                                preferred_element_type=F32)
    y = jnp.dot(hidden.astype(MXU), wd_ref[0], preferred_element_type=F32)
    g = gate_ref[0][:, :1]
    _write_token_major(obuf, (cur,), _read_token_major(obuf, (cur,), ts) + y * g)
    issue(idx_ref, 2, cur)

    @pl.when(last)
    def _():
        wait_all(2, cur)


def _moe(h2, idx, gate, wg, wu, wd):
    D = wg.shape[1]
    tok_rows = _tok_rows(D)
    n = h2.shape[0] // tok_rows
    E, cap, _ = gate.shape
    FF = wg.shape[2]
    ts = min(MOE_TILE, cap)
    per = cap // ts
    n_steps = E * per
    idx_blocks = idx.reshape(n_steps, 1, ts)
    wspec = lambda shape: pl.BlockSpec((1,) + shape, lambda e, t: (e, 0, 0))
    ids = lambda shift: pl.BlockSpec((1, 1, ts), lambda e, t: (jnp.minimum(e * per + t + shift, n_steps - 1), 0, 0),
                                     memory_space=pltpu.SMEM)
    return pl.pallas_call(
        functools.partial(_moe_kernel, ts=ts, per=per),
        grid=(E, per),
        in_specs=[ids(0), ids(1),
                  pl.BlockSpec(memory_space=pl.ANY),
                  pl.BlockSpec((1, ts, LANES), lambda e, t: (e, t, 0)),
                  wspec((D, FF)), wspec((D, FF)), wspec((FF, D)),
                  pl.BlockSpec(memory_space=pl.ANY)],
        out_specs=pl.BlockSpec(memory_space=pl.ANY),
        out_shape=jax.ShapeDtypeStruct((n * tok_rows, LANES), F32),
        scratch_shapes=[pltpu.VMEM((2, ts * tok_rows, LANES), F32), pltpu.VMEM((2, ts * tok_rows, LANES), F32),
                        pltpu.SemaphoreType.DMA((3, 2))],
        input_output_aliases={7: 0},
        compiler_params=_cparams(("arbitrary", "arbitrary")),
        name="moe",
    )(idx_blocks, idx_blocks, h2, gate, wg, wu, wd, jnp.zeros((n * tok_rows, LANES), F32))


def _final_kernel(xa_ref, xb_ref, mod_ref, g_ref, o_ref):
    x = xa_ref[...] + mod_ref[0, 5:6, :] * _read_token_major(xb_ref, (), xa_ref.shape[0])
    o_ref[...] = x * lax.rsqrt(jnp.mean(x * x, axis=-1, keepdims=True) + EPS) * g_ref[...]


def _final(xa, xb, mod, g, n_per_batch):
    T, D = xa.shape
    tm = min(512, T)
    tile = pl.BlockSpec((tm, D), lambda i: (i, 0))
    return pl.pallas_call(
        _final_kernel,
        grid=(T // tm,),
        in_specs=[tile, _token_major_spec(tm, D),
                  pl.BlockSpec((1, 6, D), _tile_batch_map(tm, n_per_batch, mod.shape[0])), _resident((1, D))],
        out_specs=tile,
        out_shape=jax.ShapeDtypeStruct((T, D), F32),
        compiler_params=_cparams(("arbitrary",)),
        name="final_norm",
    )(xa, xb, mod, g.reshape(1, D))


def _rope_tables(n):
    quarter = DK_B // 4
    t = jnp.arange(n)
    row = (t // GRID_W).astype(F32)
    col = (t % GRID_W).astype(F32)
    inv = ROPE_BASE ** (-jnp.arange(quarter, dtype=F32) / quarter)
    ang = jnp.concatenate([row[:, None] * inv, col[:, None] * inv], axis=-1)
    cos, sin = jnp.cos(ang), jnp.sin(ang)
    cos_t = jnp.concatenate([cos, cos] * (LANES // DK_B), axis=-1)
    sin_t = jnp.concatenate([-sin, sin] * (LANES // DK_B), axis=-1)
    return cos_t, sin_t


def _prep_layer(l, w_in, gla_w_gf, gla_b_gf, gla_w_gb, gla_b_gb, ret_ld_f, ret_ld_b):
    D = w_in.shape[1]
    w = w_in[l]
    s = np.cumsum([0, QK_A, QK_A, V_A, V_A, GATE_RANK, GATE_RANK, QK_B, QK_B, V_B, V_B, W_C, W_C, W_C])
    a_q, a_k, a_v, a_r, a_zf, a_zb, b_q, b_k, b_v, b_g, c_q, c_k, c_v = [w[:, s[i]:s[i + 1]] for i in range(13)]
    zpad = jnp.zeros((D, LANES - 2 * GATE_RANK), w.dtype)
    w1 = jnp.concatenate([a_q, a_k, a_v, a_r, b_q, b_k, b_v, b_g, c_q, c_k, c_v, a_zf, a_zb, zpad],
                         axis=1).astype(MXU)
    wm = w[:, MIX_W:].astype(MXU)
    wf = jnp.zeros((LANES, QK_A), F32).at[:GATE_RANK].set(gla_w_gf[l])
    wb = jnp.zeros((LANES, QK_A), F32).at[GATE_RANK:2 * GATE_RANK].set(gla_w_gb[l])
    ld = lambda v: jnp.broadcast_to(v[l].reshape(H_B // 2, 2, 1), (H_B // 2, 2, 2 * LANES))
    return dict(w1=w1, wm=wm, wf=wf, bf=gla_b_gf[l].reshape(1, QK_A), wb=wb, bb=gla_b_gb[l].reshape(1, QK_A),
                ldf=ld(ret_ld_f), ldb=ld(ret_ld_b))


def kernel(x_prompt, x_sample, cache_nat_k, cache_nat_v, state_gla_fwd, state_gla_bwd, state_ret_fwd,
           state_ret_bwd, c, c_ctx, norm1, norm2, w_ada, b_ada, w_in, gla_w_gf, gla_b_gf, gla_w_gb, gla_b_gb,
           gla_gn, ret_log_decay_f, ret_log_decay_b, ret_gn, nat_rpb, w_br_a, w_br_b, w_br_c, w_out, w_router,
           w_gate, w_up, w_down, final_norm):
    B, n_ctx, D = x_prompt.shape
    Bd, n_lat, _ = x_sample.shape
    L = w_in.shape[0]
    past = cache_nat_k.shape[2]

    cond = jnp.concatenate([c_ctx[None, :], c], axis=0)
    n_cond = cond.shape[0]
    cond = jnp.pad(cond, ((0, -n_cond % 8), (0, 0)))
    mod_all = _ada(cond, w_ada, b_ada).reshape(L, cond.shape[0], 6, D)
    rope_tabs = _rope_tables(n_lat)
    zero_a = jnp.zeros((B, H_A, DK_A, DV_A), F32)
    zero_b = jnp.zeros((B, H_B, DK_B, DV_B), F32)

    paths = {
        "ctx": dict(x=x_prompt.reshape(B * n_ctx, D), moe=None, B=B, n=n_ctx),
        "lat": dict(x=x_sample.reshape(Bd * n_lat, D), moe=None, B=Bd, n=n_lat),
    }
    outs = dict(nk=[], nv=[], gf=[], gb=[], rf=[], rb=[])
    mod_prev = {}
    for l in range(L):
        w = _prep_layer(l, w_in, gla_w_gf, gla_b_gf, gla_w_gb, gla_b_gb, ret_log_decay_f, ret_log_decay_b)
        wa, wb_, wc = w_br_a[l].astype(MXU), w_br_b[l].astype(MXU), w_br_c[l].astype(MXU)
        wo = w_out[l].astype(MXU)
        wr_t = w_router[l].T.astype(MXU)
        wg, wu, wd = w_gate[l].astype(MXU), w_up[l].astype(MXU), w_down[l].astype(MXU)
        gn_a, gn_b = gla_gn[l].reshape(1, DV_A), ret_gn[l].reshape(1, DV_B)
        bias = _window_bias(nat_rpb[l].astype(F32), n_lat // GRID_W)
        mods = {"ctx": mod_all[l, 0:1], "lat": mod_all[l, 1:n_cond]}
        for name, st in paths.items():
            Bp, n = st["B"], st["n"]
            latent = name == "lat"
            mod = mods[name]
            P, x, nat_k, nat_v = _in_proj(st["x"], st["moe"], mod_prev.get(name), mod, norm1[l], w["w1"], n,
                                          emit_kv=not latent)
            if latent:
                sa_f0, sa_b0 = state_gla_fwd[:, l].astype(F32), state_gla_bwd[:, l].astype(F32)
                sb_f0, sb_b0 = state_ret_fwd[:, l].astype(F32), state_ret_bwd[:, l].astype(F32)
            else:
                sa_f0 = sa_b0 = zero_a
                sb_f0 = sb_b0 = zero_b
            o_a, sa_f, sa_b = _gla(P, Bp, n, w["wf"], w["bf"], w["wb"], w["bb"], gn_a, sa_f0, sa_b0)
            o_b, sb_f, sb_b = _ret(P, Bp, n, w["ldf"], w["ldb"], gn_b, sb_f0, sb_b0,
                                   rope_tabs if latent else None)
            if latent:
                k_ctx = cache_nat_k[:, l].astype(F32).reshape(Bp, past, W_C)
                v_ctx = cache_nat_v[:, l].astype(F32).reshape(Bp, past, W_C)
                o_c = _attn_lat(P, Bp, n, k_ctx, v_ctx, bias)
            else:
                o_c = _attn_ctx(P, Bp, n)
                outs["nk"].append(nat_k.reshape(Bp, n, H_C, D_C))
                outs["nv"].append(nat_v.reshape(Bp, n, H_C, D_C))
                outs["gf"].append(sa_f)
                outs["gb"].append(sa_b)
                outs["rf"].append(sb_f)
                outs["rb"].append(sb_b)
            merged = _merge(x, mod, norm1[l], o_a, o_b, o_c, w["wm"], wa, wb_, wc, n)
            x1, h2, aff_t = _post(x, merged, mod, norm2[l], wo, wr_t, n)
            cap = EC_FACTOR * (Bp * n) // N_EXPERTS
            idx, gate = _select(aff_t, cap)
            st["moe"] = _moe(h2, idx[:, :, 0], gate, wg, wu, wd)
            st["x"] = x1
            mod_prev[name] = mod
    y = {name: _final(st["x"], st["moe"], mod_prev[name], final_norm, st["n"]) for name, st in paths.items()}
    stack = lambda xs: jnp.stack(xs, axis=1)
    return (y["ctx"].reshape(B, n_ctx, D), y["lat"].reshape(Bd, n_lat, D), stack(outs["nk"]), stack(outs["nv"]),
            stack(outs["gf"]), stack(outs["gb"]), stack(outs["rf"]), stack(outs["rb"]))
```

```python
import functools

import numpy as np
import jax
import jax.numpy as jnp
from jax import lax
from jax.experimental import pallas as pl
from jax.experimental.pallas import tpu as pltpu

F32 = jnp.float32
I32 = jnp.int32
MXU = jnp.bfloat16

GRID_W = 64
H_A, DK_A, DV_A = 4, 64, 128
GATE_RANK, GATE_TEMP = 16, 16.0
H_B, DK_B, DV_B = 4, 64, 128
H_C, D_C = 8, 64
WIN_H, WIN_W = 8, 16
CHUNK = 64
N_EXPERTS, EC_FACTOR = 16, 2
ROPE_BASE = 10000.0
EPS = 1e-6
NEG_INF = -1e30

QK_A, V_A = H_A * DK_A, H_A * DV_A
QK_B, V_B = H_B * DK_B, H_B * DV_B
W_C = H_C * D_C
C_AQ, C_AK, C_AV, C_AR = 0, 256, 512, 1024
C_BQ, C_BK, C_BV, C_BG = 1536, 1792, 2048, 2560
C_CQ, C_CK, C_CV, C_Z = 3072, 3584, 4096, 4608
C1 = 4736
MIX_W = 4640

LANES = 128
SUPER = 256
NCH = SUPER // CHUNK
Q_ROWS = 4
K_ROWS = Q_ROWS + WIN_H
VMEM_LIMIT = 56 * 1024 * 1024


def _cparams(sem):
    return pltpu.CompilerParams(dimension_semantics=sem, vmem_limit_bytes=VMEM_LIMIT)


def _mm(a, b):
    return jnp.dot(a.astype(MXU), b.astype(MXU), preferred_element_type=F32)


def _mm_nt(a, b):
    return lax.dot_general(a.astype(MXU), b.astype(MXU), (((1,), (1,)), ((), ())), preferred_element_type=F32)


def _iota(shape, dim):
    return lax.broadcasted_iota(I32, shape, dim)


def _sigmoid(x):
    return 1.0 / (1.0 + jnp.exp(-x))


def _silu(x):
    return x * _sigmoid(x)


def _resident(shape):
    nd = len(shape)
    return pl.BlockSpec(shape, lambda *_: (0,) * nd, pipeline_mode=pl.Buffered(1))


def _ada_kernel(cond_ref, w_ref, b_ref, o_ref):
    o_ref[0] = _mm(_silu(cond_ref[...]), w_ref[0]) + b_ref[0]


def _ada(cond, w_ada, b_ada):
    L, D, D6 = w_ada.shape
    R = cond.shape[0]
    tn = 1024 if D6 % 1024 == 0 else D
    assert D6 % tn == 0
    return pl.pallas_call(
        _ada_kernel,
        grid=(L, D6 // tn),
        in_specs=[pl.BlockSpec((R, D), lambda l, j: (0, 0)),
                  pl.BlockSpec((1, D, tn), lambda l, j: (l, 0, j)),
                  pl.BlockSpec((1, 1, tn), lambda l, j: (l, 0, j))],
        out_specs=pl.BlockSpec((1, R, tn), lambda l, j: (l, 0, j)),
        out_shape=jax.ShapeDtypeStruct((L, R, D6), F32),
        compiler_params=_cparams(("arbitrary", "arbitrary")),
        name="ada",
    )(cond, w_ada, b_ada.reshape(L, 1, D6))


def _norm_mod(x, g, shift, scale):
    y = x * lax.rsqrt(jnp.mean(x * x, axis=-1, keepdims=True) + EPS) * g
    return y * (1.0 + scale) + shift


def _tile_batch_map(tm, n_per_batch, nb):
    if nb == 1:
        return lambda i: (0, 0, 0)
    assert n_per_batch % tm == 0
    per = n_per_batch // tm
    return lambda i: (i // per, 0, 0)


def _col_chunks(width, step):
    return [(s, min(step, width - s)) for s in range(0, width, step)]


def _tok_rows(D):
    assert D % LANES == 0
    return D // LANES


def _read_token_major(ref, lead, n_tok):
    r = ref.shape[-2] // n_tok
    parts = [ref[lead + (pl.ds(j, n_tok, stride=r), slice(None))] for j in range(r)]
    return jnp.concatenate(parts, axis=1)


def _write_token_major(ref, lead, value):
    n_tok = value.shape[0]
    r = ref.shape[-2] // n_tok
    for j in range(r):
        ref[lead + (pl.ds(j, n_tok, stride=r), slice(None))] = value[:, j * LANES:(j + 1) * LANES]


def _token_major_spec(tm, D):
    return pl.BlockSpec((tm * _tok_rows(D), LANES), lambda i: (i, 0))


def _in_kernel(*refs, has_res, emit_kv):
    refs = list(refs)
    xa_ref = refs.pop(0)
    x = xa_ref[...]
    if has_res:
        xb_ref, modp_ref = refs.pop(0), refs.pop(0)
        x = x + modp_ref[0, 5:6, :] * _read_token_major(xb_ref, (), xa_ref.shape[0])
    mod_ref, g_ref, w_ref, p_ref = refs[:4]
    outs = refs[4:]
    if has_res:
        outs.pop(0)[...] = x
    h = _norm_mod(x, g_ref[...], mod_ref[0, 0:1, :], mod_ref[0, 1:2, :]).astype(MXU)
    for s, w in _col_chunks(w_ref.shape[1], 512):
        p_ref[:, s:s + w] = jnp.dot(h, w_ref[:, s:s + w], preferred_element_type=F32)
    if emit_kv:
        k_ref, v_ref = outs
        k_ref[...] = p_ref[:, C_CK:C_CK + W_C]
        v_ref[...] = p_ref[:, C_CV:C_CV + W_C]


def _in_proj(xa, xb, mod_prev, mod, g, w1, n_per_batch, emit_kv):
    T, D = xa.shape
    nb = mod.shape[0]
    tm = min(256, T)
    bmap = _tile_batch_map(tm, n_per_batch, nb)
    tile = lambda w: pl.BlockSpec((tm, w), lambda i: (i, 0))
    shape = lambda w: jax.ShapeDtypeStruct((T, w), F32)
    mspec = pl.BlockSpec((1, 6, D), bmap)
    has_res = xb is not None
    ins = [xa] + ([xb, mod_prev] if has_res else []) + [mod, g.reshape(1, D), w1]
    in_specs = ([tile(D)] + ([_token_major_spec(tm, D), mspec] if has_res else [])
                + [mspec, _resident((1, D)), _resident(w1.shape)])
    widths = [w1.shape[1]] + ([D] if has_res else []) + ([W_C, W_C] if emit_kv else [])
    out = list(pl.pallas_call(
        functools.partial(_in_kernel, has_res=has_res, emit_kv=emit_kv),
        grid=(T // tm,),
        in_specs=in_specs,
        out_specs=tuple(tile(w) for w in widths),
        out_shape=tuple(shape(w) for w in widths),
        compiler_params=_cparams(("arbitrary",)),
        name="in_proj",
    )(*ins))
    P = out.pop(0)
    x = out.pop(0) if has_res else xa
    k, v = out if emit_kv else (None, None)
    return P, x, k, v


def _split3(x):
    p1 = x.astype(MXU)
    r1 = x - p1.astype(F32)
    p2 = r1.astype(MXU)
    p3 = (r1 - p2.astype(F32)).astype(MXU)
    return p1, p2, p3


def _chunk_masks():
    row, col = _iota((SUPER, SUPER), 0), _iota((SUPER, SUPER), 1)
    same = (row // CHUNK) == (col // CHUNK)
    return same, same & (col <= row), same & (col >= row)


def _head_select(h):
    sel = (_iota((DK_A, LANES), 1) == _iota((DK_A, LANES), 0) + DK_A * h).astype(MXU)
    rep = (_iota((LANES, SUPER), 0) == (_iota((LANES, SUPER), 1) % CHUNK) + DK_A * h).astype(MXU)
    return sel, rep


def _recur_step(q_in, k_dec, att, v, state, dec_fn, same, fwd, sel, rep):
    o_intra = _mm(att, v)
    kdec_t = _mm_nt(sel, k_dec)
    kblk = jnp.where(same, jnp.concatenate([kdec_t] * NCH, axis=0), 0.0)
    upd = _mm(kblk, v)
    order = range(NCH) if fwd else range(NCH - 1, -1, -1)
    prev = [None] * NCH
    for c in order:
        prev[c] = state
        state = dec_fn(c) * state + upd[c * CHUNK:(c + 1) * CHUNK]
    q_blk = jnp.where(same, _mm(q_in, rep), 0.0)
    o_inter = _mm(q_blk, jnp.concatenate(prev, axis=0))
    return o_intra + o_inter, state


def _head_norm_gate(o, g, r):
    return o * lax.rsqrt(jnp.mean(o * o, axis=-1, keepdims=True) + EPS) * g * _silu(r)


def _gla_kernel(q_ref, k_ref, v_ref, r_ref, z_ref, wf_ref, bf_ref, wb_ref, bb_ref, gn_ref, s0f_ref, s0b_ref,
                o_ref, sf_ref, sb_ref, acc_ref, *, n_sc):
    same, tri_f, tri_b = _chunk_masks()
    sels = [_head_select(h) for h in range(2)]
    lane_head = _iota((1, LANES), 1) // DK_A

    def run(sc, states, fwd):
        off = pl.multiple_of(sc * SUPER, SUPER)
        rows = pl.ds(off, SUPER)
        q = q_ref[rows, :] * (DK_A ** -0.5)
        k = k_ref[rows, :]
        x = _mm(z_ref[rows, :], (wf_ref if fwd else wb_ref)[...]) + (bf_ref if fwd else bb_ref)[...]
        la = (jnp.minimum(x, 0.0) - jnp.log1p(jnp.exp(-jnp.abs(x)))) * (1.0 / GATE_TEMP)
        tri = tri_f if fwd else tri_b
        hi = la.astype(MXU)
        lo = la - hi.astype(F32)
        trim = tri.astype(MXU)
        b = _mm(trim, hi) + _mm(trim, lo)
        last = CHUNK - 1 if fwd else 0
        tot_rows = [b[c * CHUNK + last:c * CHUNK + last + 1, :] for c in range(NCH)]
        b_last = jnp.concatenate([jnp.broadcast_to(t, (CHUNK, LANES)) for t in tot_rows], axis=0)
        tot8 = jnp.concatenate(tot_rows + [jnp.zeros((8 - NCH, LANES), F32)], axis=0)
        parts = _split3(tot8)
        q_in = q * jnp.exp(b)
        k_out = k * jnp.exp(-b)
        k_dec = k * jnp.exp(b_last - b)
        outs, new_states = [], []
        for h in range(2):
            sel, rep = sels[h]
            tot_t = sum(_mm_nt(sel, p) for p in parts)
            dec_fn = lambda c, tot_t=tot_t: jnp.exp(jnp.broadcast_to(tot_t[:, c:c + 1], (DK_A, DV_A)))
            qh = jnp.where(lane_head == h, q_in, 0.0)
            att = jnp.where(tri, _mm_nt(qh, k_out), 0.0)
            v = v_ref[rows, h * DV_A:(h + 1) * DV_A]
            o, s = _recur_step(qh, k_dec, att, v, states[h], dec_fn, same, fwd, sel, rep)
            outs.append(o)
            new_states.append(s)
        return rows, outs, tuple(new_states)

    sf, sb = _both_directions(run, n_sc, acc_ref, o_ref, gn_ref, r_ref,
                              (s0f_ref[0, 0], s0f_ref[0, 1]), (s0b_ref[0, 0], s0b_ref[0, 1]))
    for h in range(2):
        sf_ref[0, h] = sf[h]
        sb_ref[0, h] = sb[h]


def _both_directions(run, n_sc, acc_ref, o_ref, gn_ref, gate_ref, sf, sb, unroll=1):
    dv = acc_ref.shape[1] // 2

    def park(rows, outs):
        for h in range(2):
            acc_ref[rows, h * dv:(h + 1) * dv] = outs[h]

    def finish(rows, outs, other=None):
        for h in range(2):
            cols = slice(h * dv, (h + 1) * dv)
            total = outs[h] + (acc_ref[rows, cols] if other is None else other[h])
            o_ref[rows, cols] = _head_norm_gate(total, gn_ref[...], gate_ref[rows, cols]).astype(o_ref.dtype)

    if n_sc == 1:
        rows, outs_f, sf = run(0, sf, True)
        _, outs_b, sb = run(0, sb, False)
        finish(rows, outs_f, outs_b)
        return sf, sb
    assert n_sc % 2 == 0

    def body(second):
        def step(i, carry):
            sf, sb = carry
            rows_f, outs_f, sf = run(i, sf, True)
            rows_b, outs_b, sb = run(n_sc - 1 - i, sb, False)
            (finish if second else park)(rows_f, outs_f)
            (finish if second else park)(rows_b, outs_b)
            return sf, sb
        return step

    carry = lax.fori_loop(0, n_sc // 2, body(False), (sf, sb), unroll=unroll)
    return lax.fori_loop(n_sc // 2, n_sc, body(True), carry, unroll=unroll)


def _pair_specs(n, col_q, col_k, col_v, col_r):
    return [pl.BlockSpec((n, LANES), lambda b, p: (b, col_q // LANES + p)),
            pl.BlockSpec((n, LANES), lambda b, p: (b, col_k // LANES + p)),
            pl.BlockSpec((n, 2 * DV_A), lambda b, p: (b, col_v // (2 * DV_A) + p)),
            pl.BlockSpec((n, 2 * DV_A), lambda b, p: (b, col_r // (2 * DV_A) + p))]


def _state_spec():
    return pl.BlockSpec((1, 2, DK_A, DV_A), lambda b, p: (b, p, 0, 0))


def _gla(P, B, n, wf, bf, wb, bb, gn, s0f, s0b):
    T = B * n
    assert n % SUPER == 0
    pair_w = pl.BlockSpec((LANES, LANES), lambda b, p: (0, p))
    pair_b = pl.BlockSpec((1, LANES), lambda b, p: (0, p))
    st_shape = jax.ShapeDtypeStruct((B, H_A, DK_A, DV_A), F32)
    return pl.pallas_call(
        functools.partial(_gla_kernel, n_sc=n // SUPER),
        grid=(B, H_A // 2),
        in_specs=_pair_specs(n, C_AQ, C_AK, C_AV, C_AR)
        + [pl.BlockSpec((n, LANES), lambda b, p: (b, C_Z // LANES)),
           pair_w, pair_b, pair_w, pair_b, _resident((1, DV_A)), _state_spec(), _state_spec()],
        out_specs=(pl.BlockSpec((n, 2 * DV_A), lambda b, p: (b, p)), _state_spec(), _state_spec()),
        out_shape=(jax.ShapeDtypeStruct((T, V_A), MXU), st_shape, st_shape),
        scratch_shapes=[pltpu.VMEM((n, 2 * DV_A), F32)],
        compiler_params=_cparams(("arbitrary", "arbitrary")),
        name="gla",
    )(P, P, P, P, P, wf, bf, wb, bb, gn, s0f, s0b)


def _ret_kernel(*refs, n_sc, rope):
    if rope:
        (q_ref, k_ref, v_ref, g_ref, ldf_ref, ldb_ref, gn_ref, s0f_ref, s0b_ref, cos_ref, sin_ref,
         o_ref, sf_ref, sb_ref, acc_ref) = refs
    else:
        (q_ref, k_ref, v_ref, g_ref, ldf_ref, ldb_ref, gn_ref, s0f_ref, s0b_ref,
         o_ref, sf_ref, sb_ref, acc_ref) = refs
    same, tri_f, tri_b = _chunk_masks()
    sels = [_head_select(h) for h in range(2)]
    lane_head = _iota((1, LANES), 1) // DK_B
    half = DK_B // 2
    first_half = (_iota((1, LANES), 1) % DK_B) < half
    diff = (_iota((SUPER, SUPER), 0) - _iota((SUPER, SUPER), 1)).astype(F32)
    pos = (_iota((SUPER, LANES), 0) % CHUNK).astype(F32)

    def rotate(x, rows):
        swapped = jnp.where(first_half, pltpu.roll(x, LANES - half, 1), pltpu.roll(x, half, 1))
        return x * cos_ref[rows, :] + swapped * sin_ref[rows, :]

    def run(sc, states, fwd):
        off = pl.multiple_of(sc * SUPER, SUPER)
        rows = pl.ds(off, SUPER)
        q = q_ref[rows, :]
        k = k_ref[rows, :] * (DK_B ** -0.5)
        if rope:
            q, k = rotate(q, rows), rotate(k, rows)
        ld = (ldf_ref if fwd else ldb_ref)[0]
        lg_lane = jnp.where(lane_head == 0, ld[0:1, :LANES], ld[1:2, :LANES])
        if fwd:
            q_in = q * jnp.exp(lg_lane * (pos + 1.0))
            k_dec = k * jnp.exp(lg_lane * (CHUNK - 1.0 - pos))
        else:
            q_in = q * jnp.exp(lg_lane * (CHUNK - pos))
            k_dec = k * jnp.exp(lg_lane * pos)
        tri = tri_f if fwd else tri_b
        outs, new_states = [], []
        for h in range(2):
            sel, rep = sels[h]
            lg = ld[h:h + 1, :]
            decay = jnp.where(tri, jnp.exp(lg * jnp.abs(diff)), 0.0)
            qh = jnp.where(lane_head == h, q, 0.0)
            att = _mm_nt(qh, k) * decay
            chunk_dec = jnp.exp(lg[:, :DV_B] * float(CHUNK))
            v = v_ref[rows, h * DV_B:(h + 1) * DV_B]
            o, s = _recur_step(q_in, k_dec, att, v, states[h], lambda c: chunk_dec, same, fwd, sel, rep)
            outs.append(o)
            new_states.append(s)
        return rows, outs, tuple(new_states)

    sf, sb = _both_directions(run, n_sc, acc_ref, o_ref, gn_ref, g_ref,
                              (s0f_ref[0, 0], s0f_ref[0, 1]), (s0b_ref[0, 0], s0b_ref[0, 1]), unroll=2)
    for h in range(2):
        sf_ref[0, h] = sf[h]
        sb_ref[0, h] = sb[h]


def _ret(P, B, n, ldf, ldb, gn, s0f, s0b, rope_tabs):
    T = B * n
    rope = rope_tabs is not None
    ld_spec = pl.BlockSpec((1, 2, 2 * LANES), lambda b, p: (p, 0, 0))
    st_shape = jax.ShapeDtypeStruct((B, H_B, DK_B, DV_B), F32)
    ins = [P, P, P, P, ldf, ldb, gn, s0f, s0b] + (list(rope_tabs) if rope else [])
    return pl.pallas_call(
        functools.partial(_ret_kernel, n_sc=n // SUPER, rope=rope),
        grid=(B, H_B // 2),
        in_specs=_pair_specs(n, C_BQ, C_BK, C_BV, C_BG)
        + [ld_spec, ld_spec, _resident((1, DV_B)), _state_spec(), _state_spec()]
        + ([_resident((n, LANES))] * 2 if rope else []),
        out_specs=(pl.BlockSpec((n, 2 * DV_B), lambda b, p: (b, p)), _state_spec(), _state_spec()),
        out_shape=(jax.ShapeDtypeStruct((T, V_B), MXU), st_shape, st_shape),
        scratch_shapes=[pltpu.VMEM((n, 2 * DV_B), F32)],
        compiler_params=_cparams(("arbitrary", "arbitrary")),
        name="retention",
    )(*ins)


def _attn_ctx_kernel(q_ref, k_ref, v_ref, o_ref):
    lane_head = _iota((1, LANES), 1) // D_C
    q = q_ref[...] * (D_C ** -0.5)
    k = k_ref[...]
    v = v_ref[...]
    out = jnp.zeros(q.shape, F32)
    for h in range(2):
        s = _mm_nt(jnp.where(lane_head == h, q, 0.0), k)
        e = jnp.exp(s - jnp.max(s, axis=-1, keepdims=True))
        p = e / jnp.sum(e, axis=-1, keepdims=True)
        out = jnp.where(lane_head == h, _mm(p, v), out)
    o_ref[...] = out.astype(o_ref.dtype)


def _attn_ctx(P, B, n):
    spec = lambda col: pl.BlockSpec((n, LANES), lambda b, p: (b, col // LANES + p))
    return pl.pallas_call(
        _attn_ctx_kernel,
        grid=(B, H_C // 2),
        in_specs=[spec(C_CQ), spec(C_CK), spec(C_CV)],
        out_specs=pl.BlockSpec((n, LANES), lambda b, p: (b, p)),
        out_shape=jax.ShapeDtypeStruct((B * n, W_C), MXU),
        compiler_params=_cparams(("arbitrary", "arbitrary")),
        name="attn_ctx",
    )(P, P, P)


def _attn_lat_kernel(q_ref, k_ref, v_ref, kc_ref, vc_ref, bias_ref, o_ref, *, n_blk, key_start_max):
    lane_head = _iota((1, LANES), 1) // D_C
    nq, nk = Q_ROWS * GRID_W, K_ROWS * GRID_W
    kc = kc_ref[0]
    vc = vc_ref[0]

    def body(i, carry):
        qrows = pl.ds(pl.multiple_of(i * nq, nq), nq)
        kstart = jnp.clip(i * Q_ROWS - WIN_H // 2, 0, key_start_max)
        krows = pl.ds(pl.multiple_of(kstart * GRID_W, GRID_W), nk)
        variant = jnp.where(i == 0, 0, jnp.where(i == n_blk - 1, 2, 1))
        q = q_ref[qrows, :] * (D_C ** -0.5)
        k = k_ref[krows, :]
        v = v_ref[krows, :]
        out = jnp.zeros((nq, LANES), F32)
        for h in range(2):
            qh = jnp.where(lane_head == h, q, 0.0)
            s_win = _mm_nt(qh, k) + bias_ref[h, variant]
            s_ctx = _mm_nt(qh, kc)
            m = jnp.maximum(jnp.max(s_win, axis=-1, keepdims=True), jnp.max(s_ctx, axis=-1, keepdims=True))
            e_win = jnp.exp(s_win - m)
            e_ctx = jnp.exp(s_ctx - m)
            inv = 1.0 / (jnp.sum(e_win, axis=-1, keepdims=True) + jnp.sum(e_ctx, axis=-1, keepdims=True))
            o = (_mm(e_win, v) + _mm(e_ctx, vc)) * inv
            out = jnp.where(lane_head == h, o, out)
        o_ref[qrows, :] = out.astype(o_ref.dtype)
        return carry

    lax.fori_loop(0, n_blk, body, 0, unroll=2)


def _window_bias(rpb, rows):
    n_blk = rows // Q_ROWS
    kh = min(WIN_H, rows)
    qc = np.arange(GRID_W)
    win_c = np.clip(qc - WIN_W // 2, 0, GRID_W - WIN_W)
    kc = np.arange(GRID_W)
    col_ok = (kc[None, :] >= win_c[:, None]) & (kc[None, :] < win_c[:, None] + WIN_W)
    coff = np.clip(kc[None, :] - qc[:, None], -(WIN_W - 1), WIN_W - 1) + WIN_W - 1
    col_pick = (coff[:, :, None] == np.arange(2 * WIN_W - 1)).astype(np.float32)
    exact = lax.Precision.HIGHEST
    tabs = []
    for blk in (0, min(1, n_blk - 1), n_blk - 1):
        r = blk * Q_ROWS + np.arange(Q_ROWS)
        kstart = int(np.clip(blk * Q_ROWS - WIN_H // 2, 0, rows - K_ROWS))
        kr = kstart + np.arange(K_ROWS)
        r0 = np.clip(r - WIN_H // 2, 0, rows - kh)
        row_ok = (kr[None, :] >= r0[:, None]) & (kr[None, :] < r0[:, None] + kh)
        roff = np.clip(kr[None, :] - r[:, None] + WIN_H - 1, 0, 2 * WIN_H - 2)
        ok = row_ok[:, None, :, None] & col_ok[None, :, None, :]
        row_pick = (roff[:, :, None] == np.arange(2 * WIN_H - 1)).astype(np.float32)
        by_row = jnp.einsum('qka,hab->hqkb', row_pick, rpb, precision=exact)
        bias = jnp.einsum('hqkb,cdb->hqckd', by_row, col_pick, precision=exact)
        tab = jnp.where(ok[None], bias, NEG_INF)
        tabs.append(tab.reshape(rpb.shape[0], Q_ROWS * GRID_W, K_ROWS * GRID_W))
    return jnp.stack(tabs, axis=1)


def _attn_lat(P, B, n, k_ctx, v_ctx, bias):
    rows = n // GRID_W
    assert rows % Q_ROWS == 0 and rows >= K_ROWS
    L_ctx = k_ctx.shape[1]
    spec = lambda col: pl.BlockSpec((n, LANES), lambda p, b: (b, col // LANES + p))
    cspec = pl.BlockSpec((1, L_ctx, LANES), lambda p, b: (b, 0, p))
    nq, nk = Q_ROWS * GRID_W, K_ROWS * GRID_W
    return pl.pallas_call(
        functools.partial(_attn_lat_kernel, n_blk=rows // Q_ROWS, key_start_max=rows - K_ROWS),
        grid=(H_C // 2, B),
        in_specs=[spec(C_CQ), spec(C_CK), spec(C_CV), cspec, cspec,
                  pl.BlockSpec((2, 3, nq, nk), lambda p, b: (p, 0, 0, 0))],
        out_specs=pl.BlockSpec((n, LANES), lambda p, b: (b, p)),
        out_shape=jax.ShapeDtypeStruct((B * n, W_C), MXU),
        compiler_params=_cparams(("arbitrary", "arbitrary")),
        name="attn_lat",
    )(P, P, P, k_ctx, v_ctx, bias)


def _merge_kernel(x_ref, mod_ref, g_ref, oa_ref, ob_ref, oc_ref, wm_ref, wa_ref, wb_ref, wc_ref, m_ref):
    D = x_ref.shape[1]
    h = _norm_mod(x_ref[...], g_ref[...], mod_ref[0, 0:1, :], mod_ref[0, 1:2, :]).astype(MXU)
    for s, w in _col_chunks(D, 512):
        acc = None
        for i, (o_ref, wbr_ref) in enumerate(((oa_ref, wa_ref), (ob_ref, wb_ref), (oc_ref, wc_ref))):
            gate = _sigmoid(jnp.dot(h, wm_ref[:, i * D + s:i * D + s + w], preferred_element_type=F32))
            term = gate * jnp.dot(o_ref[...], wbr_ref[:, s:s + w], preferred_element_type=F32)
            acc = term if acc is None else acc + term
        m_ref[:, s:s + w] = acc.astype(m_ref.dtype)


def _merge(x, mod, g, oa, ob, oc, wm, wa, wb, wc, n_per_batch):
    T, D = x.shape
    tm = min(256, T)
    bmap = _tile_batch_map(tm, n_per_batch, mod.shape[0])
    tile = lambda w: pl.BlockSpec((tm, w), lambda i: (i, 0))
    return pl.pallas_call(
        _merge_kernel,
        grid=(T // tm,),
        in_specs=[tile(D), pl.BlockSpec((1, 6, D), bmap), _resident((1, D)), tile(V_A), tile(V_B), tile(W_C),
                  _resident(wm.shape), _resident(wa.shape), _resident(wb.shape), _resident(wc.shape)],
        out_specs=tile(D),
        out_shape=jax.ShapeDtypeStruct((T, D), MXU),
        compiler_params=_cparams(("arbitrary",)),
        name="merge",
    )(x, mod, g.reshape(1, D), oa, ob, oc, wm, wa, wb, wc)


def _post_kernel(x_ref, m_ref, mod_ref, g_ref, wo_ref, wr_ref, x1_ref, h2_ref, aff_ref, acc0_ref):
    acc0_ref[...] = jnp.zeros(acc0_ref.shape, acc0_ref.dtype)
    mod = mod_ref[0]
    x1 = x_ref[...] + mod[2:3, :] * jnp.dot(m_ref[...], wo_ref[...], preferred_element_type=F32)
    x1_ref[...] = x1
    h2 = _norm_mod(x1, g_ref[...], mod[3:4, :], mod[4:5, :])
    _write_token_major(h2_ref, (), h2)
    logits = _mm_nt(wr_ref[...], h2)
    e = jnp.exp(logits - jnp.max(logits, axis=0, keepdims=True))
    aff_ref[...] = e / jnp.sum(e, axis=0, keepdims=True)


def _post(x, merged, mod, g, wo, wr_t, n_per_batch):
    T, D = x.shape
    tm = min(512, T)
    bmap = _tile_batch_map(tm, n_per_batch, mod.shape[0])
    tile = pl.BlockSpec((tm, D), lambda i: (i, 0))
    return pl.pallas_call(
        _post_kernel,
        grid=(T // tm,),
        in_specs=[tile, tile, pl.BlockSpec((1, 6, D), bmap), _resident((1, D)), _resident(wo.shape),
                  _resident(wr_t.shape)],
        out_specs=(tile, _token_major_spec(tm, D), pl.BlockSpec((N_EXPERTS, tm), lambda i: (0, i)),
                   _token_major_spec(tm, D)),
        out_shape=(jax.ShapeDtypeStruct((T, D), F32), jax.ShapeDtypeStruct((T * _tok_rows(D), LANES), F32),
                   jax.ShapeDtypeStruct((N_EXPERTS, T), F32),
                   jax.ShapeDtypeStruct((T * _tok_rows(D), LANES), F32)),
        compiler_params=_cparams(("arbitrary",)),
        name="post",
    )(x, merged, mod, g.reshape(1, D), wo, wr_t)


def _prefix_counts(mask, upper, lower_strict):
    within = _mm(mask, upper)
    row_tot = jnp.broadcast_to(within[:, LANES - 1:LANES], within.shape)
    row_start = _mm(lower_strict, row_tot)
    return row_start + within - mask, row_start, within


def _split_int(x):
    high = jnp.floor(x * (1.0 / 256.0))
    return high, x - high * 256.0


def _select_kernel(aff_ref, idx_ref, gate_ref, *, cap, slot_tile):
    R = aff_ref.shape[1]
    upper = (_iota((LANES, LANES), 0) <= _iota((LANES, LANES), 1)).astype(MXU)
    lower_strict = (_iota((R, R), 1) < _iota((R, R), 0)).astype(MXU)
    lane0 = (_iota((8, LANES), 1) == 0).astype(MXU)
    lane_id = _iota((slot_tile, LANES), 1)
    row_id = _iota((slot_tile, R), 1)

    aff = aff_ref[0]
    bits = pltpu.bitcast(aff, I32)

    def bit_step(j, thr):
        cand = thr | jnp.left_shift(jnp.int32(1), 30 - j)
        cnt = jnp.sum((bits >= cand).astype(I32), axis=(0, 1), keepdims=True)
        return jnp.where(cnt >= cap, cand, thr)

    thr = lax.fori_loop(0, 31, bit_step, jnp.zeros((1, 1), I32))
    gt = (bits > thr).astype(F32)
    eq = (bits == thr).astype(F32)
    need = float(cap) - jnp.sum(gt, axis=(0, 1), keepdims=True)
    eq_rank, _, _ = _prefix_counts(eq, upper, lower_strict)
    sel = gt + eq * (eq_rank < need).astype(F32)
    _, row_start, within = _prefix_counts(sel, upper, lower_strict)
    row_end = row_start + jnp.broadcast_to(within[:, LANES - 1:LANES], within.shape)
    end_hi, end_lo = _split_int(row_end)
    row_end_t = (_mm_nt(lane0, end_hi) * 256.0 + _mm_nt(lane0, end_lo))[0:1, :]
    start_hi, start_lo = _split_int(row_start)
    aff_parts = _split3(aff)

    def tile(t, carry):
        base = t * slot_tile
        slot = (base + _iota((slot_tile, 1), 0)).astype(F32)
        row = jnp.sum((row_end_t <= slot).astype(F32), axis=1, keepdims=True)
        onehot = (row_id.astype(F32) == row).astype(MXU)
        start = _mm(onehot, start_hi) * 256.0 + _mm(onehot, start_lo)
        rank = slot - start
        counts = _mm(onehot, within)
        col = jnp.sum((counts <= rank).astype(F32), axis=1, keepdims=True)
        vals = sum(_mm(onehot, p) for p in aff_parts)
        gate = jnp.sum(jnp.where(lane_id.astype(F32) == col, vals, 0.0), axis=1, keepdims=True)
        token = (row * float(LANES) + col).astype(I32)
        out_rows = pl.ds(pl.multiple_of(base, slot_tile), slot_tile)
        idx_ref[0, out_rows, :] = jnp.broadcast_to(token, (slot_tile, LANES))
        gate_ref[0, out_rows, :] = jnp.broadcast_to(gate, (slot_tile, LANES))
        return carry

    lax.fori_loop(0, cap // slot_tile, tile, 0)


def _select(aff_t, cap):
    E, n = aff_t.shape
    assert n % LANES == 0
    R = n // LANES
    slot_tile = min(512, cap)
    shape = (E, cap, LANES)
    return pl.pallas_call(
        functools.partial(_select_kernel, cap=cap, slot_tile=slot_tile),
        grid=(E,),
        in_specs=[pl.BlockSpec((1, R, LANES), lambda e: (e, 0, 0))],
        out_specs=(pl.BlockSpec((1, cap, LANES), lambda e: (e, 0, 0)),
                   pl.BlockSpec((1, cap, LANES), lambda e: (e, 0, 0))),
        out_shape=(jax.ShapeDtypeStruct(shape, I32), jax.ShapeDtypeStruct(shape, F32)),
        compiler_params=_cparams(("arbitrary",)),
        name="select",
    )(aff_t.reshape(E, R, LANES))


MOE_TILE = 512
MOE_ISSUE_UNROLL = 8


def _moe_kernel(idx_ref, idx_next_ref, h_hbm, gate_ref, wg_ref, wu_ref, wd_ref, acc_in, acc_hbm, xbuf, obuf, sem,
                *, ts, per):
    del acc_in
    tok_rows = xbuf.shape[1] // ts
    t = pl.program_id(1)
    step = pl.program_id(0) * per + t
    n_steps = pl.num_programs(0) * per
    cur = lax.rem(step, 2)
    nxt = 1 - cur
    first, last = step == 0, step == n_steps - 1
    expert_start, expert_end = t == 0, t == per - 1

    def issue_row(ids_ref, kind, buf, i):
        tok = pl.ds(pl.multiple_of(ids_ref[0, 0, i] * tok_rows, tok_rows), tok_rows)
        row = pl.ds(pl.multiple_of(i * tok_rows, tok_rows), tok_rows)
        if kind == 0:
            cp = pltpu.make_async_copy(h_hbm.at[tok, :], xbuf.at[buf, row, :], sem.at[0, buf])
        elif kind == 1:
            cp = pltpu.make_async_copy(acc_hbm.at[tok, :], obuf.at[buf, row, :], sem.at[1, buf])
        else:
            cp = pltpu.make_async_copy(obuf.at[buf, row, :], acc_hbm.at[tok, :], sem.at[2, buf])
        cp.start()

    def issue(ids_ref, kind, buf):
        def body(i, c):
            issue_row(ids_ref, kind, buf, i)
            return c
        lax.fori_loop(0, ts, body, 0, unroll=MOE_ISSUE_UNROLL)

    def wait_all(kind, buf):
        ref = xbuf if kind == 0 else obuf
        pltpu.make_async_copy(ref.at[buf], ref.at[buf], sem.at[kind, buf]).wait()

    @pl.when(first)
    def _():
        issue(idx_ref, 0, cur)

    @pl.when(expert_start)
    def _():
        @pl.when(jnp.logical_not(first))
        def _():
            wait_all(2, nxt)
        issue(idx_ref, 1, cur)

    wait_all(0, cur)
    wait_all(1, cur)

    @pl.when(jnp.logical_not(expert_start))
    def _():
        wait_all(2, nxt)

    def tile(prefetch_acc):
        for i in range(ts):
            issue_row(idx_next_ref, 0, nxt, i)
            if prefetch_acc:
                issue_row(idx_next_ref, 1, nxt, i)
        x = _read_token_major(xbuf, (cur,), ts).astype(MXU)
        hidden = (_silu(jnp.dot(x, wg_ref[0], preferred_element_type=F32))
                  * jnp.dot(x, wu_ref[0], preferred_element_type=F32))
        y = jnp.dot(hidden.astype(MXU), wd_ref[0], preferred_element_type=F32)
        g = gate_ref[0][:, :1]
        _write_token_major(obuf, (cur,), _read_token_major(obuf, (cur,), ts) + y * g)

    pl.when(jnp.logical_not(expert_end))(lambda: tile(True))
    pl.when(expert_end)(lambda: tile(False))
    issue(idx_ref, 2, cur)

    @pl.when(last)
    def _():
        wait_all(2, cur)
        wait_all(0, nxt)


def _moe(h2, idx, gate, wg, wu, wd, acc0):
    D = wg.shape[1]
    tok_rows = _tok_rows(D)
    n = h2.shape[0] // tok_rows
    E, cap, _ = gate.shape
    FF = wg.shape[2]
    ts = min(MOE_TILE, cap)
    per = cap // ts
    n_steps = E * per
    idx_blocks = idx.reshape(n_steps, 1, ts)
    wspec = lambda shape: pl.BlockSpec((1,) + shape, lambda e, t: (e, 0, 0))
    ids = lambda shift: pl.BlockSpec((1, 1, ts), lambda e, t: (jnp.minimum(e * per + t + shift, n_steps - 1), 0, 0),
                                     memory_space=pltpu.SMEM)
    return pl.pallas_call(
        functools.partial(_moe_kernel, ts=ts, per=per),
        grid=(E, per),
        in_specs=[ids(0), ids(1),
                  pl.BlockSpec(memory_space=pl.ANY),
                  pl.BlockSpec((1, ts, LANES), lambda e, t: (e, t, 0)),
                  wspec((D, FF)), wspec((D, FF)), wspec((FF, D)),
                  pl.BlockSpec(memory_space=pl.ANY)],
        out_specs=pl.BlockSpec(memory_space=pl.ANY),
        out_shape=jax.ShapeDtypeStruct((n * tok_rows, LANES), F32),
        scratch_shapes=[pltpu.VMEM((2, ts * tok_rows, LANES), F32), pltpu.VMEM((2, ts * tok_rows, LANES), F32),
                        pltpu.SemaphoreType.DMA((3, 2))],
        input_output_aliases={7: 0},
        compiler_params=_cparams(("arbitrary", "arbitrary")),
        name="moe",
    )(idx_blocks, idx_blocks, h2, gate, wg, wu, wd, acc0)


def _final_kernel(xa_ref, xb_ref, mod_ref, g_ref, o_ref):
    x = xa_ref[...] + mod_ref[0, 5:6, :] * _read_token_major(xb_ref, (), xa_ref.shape[0])
    o_ref[...] = x * lax.rsqrt(jnp.mean(x * x, axis=-1, keepdims=True) + EPS) * g_ref[...]


def _final(xa, xb, mod, g, n_per_batch):
    T, D = xa.shape
    tm = min(512, T)
    tile = pl.BlockSpec((tm, D), lambda i: (i, 0))
    return pl.pallas_call(
        _final_kernel,
        grid=(T // tm,),
        in_specs=[tile, _token_major_spec(tm, D),
                  pl.BlockSpec((1, 6, D), _tile_batch_map(tm, n_per_batch, mod.shape[0])), _resident((1, D))],
        out_specs=tile,
        out_shape=jax.ShapeDtypeStruct((T, D), F32),
        compiler_params=_cparams(("arbitrary",)),
        name="final_norm",
    )(xa, xb, mod, g.reshape(1, D))


def _rope_tables(n):
    quarter = DK_B // 4
    t = jnp.arange(n)
    row = (t // GRID_W).astype(F32)
    col = (t % GRID_W).astype(F32)
    inv = ROPE_BASE ** (-jnp.arange(quarter, dtype=F32) / quarter)
    ang = jnp.concatenate([row[:, None] * inv, col[:, None] * inv], axis=-1)
    cos, sin = jnp.cos(ang), jnp.sin(ang)
    cos_t = jnp.concatenate([cos, cos] * (LANES // DK_B), axis=-1)
    sin_t = jnp.concatenate([-sin, sin] * (LANES // DK_B), axis=-1)
    return cos_t, sin_t


def _prep_layer(l, w_in, gla_w_gf, gla_b_gf, gla_w_gb, gla_b_gb, ret_ld_f, ret_ld_b):
    D = w_in.shape[1]
    w = w_in[l]
    s = np.cumsum([0, QK_A, QK_A, V_A, V_A, GATE_RANK, GATE_RANK, QK_B, QK_B, V_B, V_B, W_C, W_C, W_C])
    a_q, a_k, a_v, a_r, a_zf, a_zb, b_q, b_k, b_v, b_g, c_q, c_k, c_v = [w[:, s[i]:s[i + 1]] for i in range(13)]
    zpad = jnp.zeros((D, LANES - 2 * GATE_RANK), w.dtype)
    w1 = jnp.concatenate([a_q, a_k, a_v, a_r, b_q, b_k, b_v, b_g, c_q, c_k, c_v, a_zf, a_zb, zpad],
                         axis=1).astype(MXU)
    wm = w[:, MIX_W:].astype(MXU)
    wf = jnp.zeros((LANES, QK_A), F32).at[:GATE_RANK].set(gla_w_gf[l])
    wb = jnp.zeros((LANES, QK_A), F32).at[GATE_RANK:2 * GATE_RANK].set(gla_w_gb[l])
    ld = lambda v: jnp.broadcast_to(v[l].reshape(H_B // 2, 2, 1), (H_B // 2, 2, 2 * LANES))
    return dict(w1=w1, wm=wm, wf=wf, bf=gla_b_gf[l].reshape(1, QK_A), wb=wb, bb=gla_b_gb[l].reshape(1, QK_A),
                ldf=ld(ret_ld_f), ldb=ld(ret_ld_b))


def kernel(x_prompt, x_sample, cache_nat_k, cache_nat_v, state_gla_fwd, state_gla_bwd, state_ret_fwd,
           state_ret_bwd, c, c_ctx, norm1, norm2, w_ada, b_ada, w_in, gla_w_gf, gla_b_gf, gla_w_gb, gla_b_gb,
           gla_gn, ret_log_decay_f, ret_log_decay_b, ret_gn, nat_rpb, w_br_a, w_br_b, w_br_c, w_out, w_router,
           w_gate, w_up, w_down, final_norm):
    B, n_ctx, D = x_prompt.shape
    Bd, n_lat, _ = x_sample.shape
    L = w_in.shape[0]
    past = cache_nat_k.shape[2]

    cond = jnp.concatenate([c_ctx[None, :], c], axis=0)
    n_cond = cond.shape[0]
    cond = jnp.pad(cond, ((0, -n_cond % 8), (0, 0)))
    mod_all = _ada(cond, w_ada, b_ada).reshape(L, cond.shape[0], 6, D)
    rope_tabs = _rope_tables(n_lat)
    zero_a = jnp.zeros((B, H_A, DK_A, DV_A), F32)
    zero_b = jnp.zeros((B, H_B, DK_B, DV_B), F32)

    paths = {
        "ctx": dict(x=x_prompt.reshape(B * n_ctx, D), moe=None, B=B, n=n_ctx),
        "lat": dict(x=x_sample.reshape(Bd * n_lat, D), moe=None, B=Bd, n=n_lat),
    }
    outs = dict(nk=[], nv=[], gf=[], gb=[], rf=[], rb=[])
    mod_prev = {}
    for l in range(L):
        w = _prep_layer(l, w_in, gla_w_gf, gla_b_gf, gla_w_gb, gla_b_gb, ret_log_decay_f, ret_log_decay_b)
        wa, wb_, wc = w_br_a[l].astype(MXU), w_br_b[l].astype(MXU), w_br_c[l].astype(MXU)
        wo = w_out[l].astype(MXU)
        wr_t = w_router[l].T.astype(MXU)
        wg, wu, wd = w_gate[l].astype(MXU), w_up[l].astype(MXU), w_down[l].astype(MXU)
        gn_a, gn_b = gla_gn[l].reshape(1, DV_A), ret_gn[l].reshape(1, DV_B)
        bias = _window_bias(nat_rpb[l].astype(F32), n_lat // GRID_W)
        mods = {"ctx": mod_all[l, 0:1], "lat": mod_all[l, 1:n_cond]}
        for name, st in paths.items():
            Bp, n = st["B"], st["n"]
            latent = name == "lat"
            mod = mods[name]
            P, x, nat_k, nat_v = _in_proj(st["x"], st["moe"], mod_prev.get(name), mod, norm1[l], w["w1"], n,
                                          emit_kv=not latent)
            if latent:
                sa_f0, sa_b0 = state_gla_fwd[:, l].astype(F32), state_gla_bwd[:, l].astype(F32)
                sb_f0, sb_b0 = state_ret_fwd[:, l].astype(F32), state_ret_bwd[:, l].astype(F32)
            else:
                sa_f0 = sa_b0 = zero_a
                sb_f0 = sb_b0 = zero_b
            o_a, sa_f, sa_b = _gla(P, Bp, n, w["wf"], w["bf"], w["wb"], w["bb"], gn_a, sa_f0, sa_b0)
            o_b, sb_f, sb_b = _ret(P, Bp, n, w["ldf"], w["ldb"], gn_b, sb_f0, sb_b0,
                                   rope_tabs if latent else None)
            if latent:
                k_ctx = cache_nat_k[:, l].astype(F32).reshape(Bp, past, W_C)
                v_ctx = cache_nat_v[:, l].astype(F32).reshape(Bp, past, W_C)
                o_c = _attn_lat(P, Bp, n, k_ctx, v_ctx, bias)
            else:
                o_c = _attn_ctx(P, Bp, n)
                outs["nk"].append(nat_k.reshape(Bp, n, H_C, D_C))
                outs["nv"].append(nat_v.reshape(Bp, n, H_C, D_C))
                outs["gf"].append(sa_f)
                outs["gb"].append(sa_b)
                outs["rf"].append(sb_f)
                outs["rb"].append(sb_b)
            merged = _merge(x, mod, norm1[l], o_a, o_b, o_c, w["wm"], wa, wb_, wc, n)
            x1, h2, aff_t, acc0 = _post(x, merged, mod, norm2[l], wo, wr_t, n)
            cap = EC_FACTOR * (Bp * n) // N_EXPERTS
            idx, gate = _select(aff_t, cap)
            st["moe"] = _moe(h2, idx[:, :, 0], gate, wg, wu, wd, acc0)
            st["x"] = x1
            mod_prev[name] = mod
    y = {name: _final(st["x"], st["moe"], mod_prev[name], final_norm, st["n"]) for name, st in paths.items()}
    stack = lambda xs: jnp.stack(xs, axis=1)
    return (y["ctx"].reshape(B, n_ctx, D), y["lat"].reshape(Bd, n_lat, D), stack(outs["nk"]), stack(outs["nv"]),
            stack(outs["gf"]), stack(outs["gb"]), stack(outs["rf"]), stack(outs["rb"]))
```

```python
import functools

import numpy as np
import jax
import jax.numpy as jnp
from jax import lax
from jax.experimental import pallas as pl
from jax.experimental.pallas import tpu as pltpu

F32 = jnp.float32
I32 = jnp.int32
MXU = jnp.bfloat16

GRID_W = 64
H_A, DK_A, DV_A = 4, 64, 128
GATE_RANK, GATE_TEMP = 16, 16.0
H_B, DK_B, DV_B = 4, 64, 128
H_C, D_C = 8, 64
WIN_H, WIN_W = 8, 16
CHUNK = 64
N_EXPERTS, EC_FACTOR = 16, 2
ROPE_BASE = 10000.0
EPS = 1e-6
NEG_INF = -1e30

QK_A, V_A = H_A * DK_A, H_A * DV_A
QK_B, V_B = H_B * DK_B, H_B * DV_B
W_C = H_C * D_C
C_AQ, C_AK, C_AV, C_AR = 0, 256, 512, 1024
C_BQ, C_BK, C_BV, C_BG = 1536, 1792, 2048, 2560
C_CQ, C_CK, C_CV, C_Z = 3072, 3584, 4096, 4608
C1 = 4736
MIX_W = 4640

LANES = 128
SUPER = 256
NCH = SUPER // CHUNK
Q_ROWS = 4
K_ROWS = Q_ROWS + WIN_H
VMEM_LIMIT = 56 * 1024 * 1024


def _cparams(sem):
    return pltpu.CompilerParams(dimension_semantics=sem, vmem_limit_bytes=VMEM_LIMIT)


def _mm(a, b):
    return jnp.dot(a.astype(MXU), b.astype(MXU), preferred_element_type=F32)


def _mm_nt(a, b):
    return lax.dot_general(a.astype(MXU), b.astype(MXU), (((1,), (1,)), ((), ())), preferred_element_type=F32)


def _iota(shape, dim):
    return lax.broadcasted_iota(I32, shape, dim)


def _sigmoid(x):
    return 1.0 / (1.0 + jnp.exp(-x))


def _silu(x):
    return x * _sigmoid(x)


def _resident(shape):
    nd = len(shape)
    return pl.BlockSpec(shape, lambda *_: (0,) * nd, pipeline_mode=pl.Buffered(1))


def _ada_kernel(cond_ref, w_ref, b_ref, o_ref):
    o_ref[0] = _mm(_silu(cond_ref[...]), w_ref[0]) + b_ref[0]


def _ada(cond, w_ada, b_ada):
    L, D, D6 = w_ada.shape
    R = cond.shape[0]
    tn = 1024 if D6 % 1024 == 0 else D
    assert D6 % tn == 0
    return pl.pallas_call(
        _ada_kernel,
        grid=(L, D6 // tn),
        in_specs=[pl.BlockSpec((R, D), lambda l, j: (0, 0)),
                  pl.BlockSpec((1, D, tn), lambda l, j: (l, 0, j)),
                  pl.BlockSpec((1, 1, tn), lambda l, j: (l, 0, j))],
        out_specs=pl.BlockSpec((1, R, tn), lambda l, j: (l, 0, j)),
        out_shape=jax.ShapeDtypeStruct((L, R, D6), F32),
        compiler_params=_cparams(("arbitrary", "arbitrary")),
        name="ada",
    )(cond, w_ada, b_ada.reshape(L, 1, D6))


def _norm_mod(x, g, shift, scale):
    y = x * lax.rsqrt(jnp.mean(x * x, axis=-1, keepdims=True) + EPS) * g
    return y * (1.0 + scale) + shift


def _tile_batch_map(tm, n_per_batch, nb):
    if nb == 1:
        return lambda i: (0, 0, 0)
    assert n_per_batch % tm == 0
    per = n_per_batch // tm
    return lambda i: (i // per, 0, 0)


def _col_chunks(width, step):
    return [(s, min(step, width - s)) for s in range(0, width, step)]


def _tok_rows(D):
    assert D % LANES == 0
    return D // LANES


def _read_token_major(ref, lead, n_tok):
    r = ref.shape[-2] // n_tok
    parts = [ref[lead + (pl.ds(j, n_tok, stride=r), slice(None))] for j in range(r)]
    return jnp.concatenate(parts, axis=1)


def _write_token_major(ref, lead, value):
    n_tok = value.shape[0]
    r = ref.shape[-2] // n_tok
    for j in range(r):
        ref[lead + (pl.ds(j, n_tok, stride=r), slice(None))] = value[:, j * LANES:(j + 1) * LANES]


def _token_major_spec(tm, D):
    return pl.BlockSpec((tm * _tok_rows(D), LANES), lambda i: (i, 0))


def _in_kernel(*refs, has_res, emit_kv):
    refs = list(refs)
    xa_ref = refs.pop(0)
    x = xa_ref[...]
    if has_res:
        xb_ref, modp_ref = refs.pop(0), refs.pop(0)
        x = x + modp_ref[0, 5:6, :] * _read_token_major(xb_ref, (), xa_ref.shape[0])
    mod_ref, g_ref, w_ref, p_ref = refs[:4]
    outs = refs[4:]
    if has_res:
        outs.pop(0)[...] = x
    h = _norm_mod(x, g_ref[...], mod_ref[0, 0:1, :], mod_ref[0, 1:2, :]).astype(MXU)
    for s, w in _col_chunks(w_ref.shape[1], 512):
        p_ref[:, s:s + w] = jnp.dot(h, w_ref[:, s:s + w], preferred_element_type=F32)
    if emit_kv:
        k_ref, v_ref = outs
        k_ref[...] = p_ref[:, C_CK:C_CK + W_C]
        v_ref[...] = p_ref[:, C_CV:C_CV + W_C]


def _in_proj(xa, xb, mod_prev, mod, g, w1, n_per_batch, emit_kv):
    T, D = xa.shape
    nb = mod.shape[0]
    tm = min(256, T)
    bmap = _tile_batch_map(tm, n_per_batch, nb)
    tile = lambda w: pl.BlockSpec((tm, w), lambda i: (i, 0))
    shape = lambda w: jax.ShapeDtypeStruct((T, w), F32)
    mspec = pl.BlockSpec((1, 6, D), bmap)
    has_res = xb is not None
    ins = [xa] + ([xb, mod_prev] if has_res else []) + [mod, g.reshape(1, D), w1]
    in_specs = ([tile(D)] + ([_token_major_spec(tm, D), mspec] if has_res else [])
                + [mspec, _resident((1, D)), _resident(w1.shape)])
    widths = [w1.shape[1]] + ([D] if has_res else []) + ([W_C, W_C] if emit_kv else [])
    out = list(pl.pallas_call(
        functools.partial(_in_kernel, has_res=has_res, emit_kv=emit_kv),
        grid=(T // tm,),
        in_specs=in_specs,
        out_specs=tuple(tile(w) for w in widths),
        out_shape=tuple(shape(w) for w in widths),
        compiler_params=_cparams(("arbitrary",)),
        name="in_proj",
    )(*ins))
    P = out.pop(0)
    x = out.pop(0) if has_res else xa
    k, v = out if emit_kv else (None, None)
    return P, x, k, v


def _split3(x):
    p1 = x.astype(MXU)
    r1 = x - p1.astype(F32)
    p2 = r1.astype(MXU)
    p3 = (r1 - p2.astype(F32)).astype(MXU)
    return p1, p2, p3


def _chunk_masks():
    row, col = _iota((SUPER, SUPER), 0), _iota((SUPER, SUPER), 1)
    same = (row // CHUNK) == (col // CHUNK)
    return same, same & (col <= row), same & (col >= row)


def _head_replicate(h):
    return (_iota((LANES, SUPER), 0) == (_iota((LANES, SUPER), 1) % CHUNK) + DK_A * h).astype(MXU)


def _recur_step(q_in, kdec_t, att, v, state, dec_fn, same, fwd, rep):
    o_intra = _mm(att, v)
    kblk = jnp.where(same, jnp.concatenate([kdec_t] * NCH, axis=0), 0.0)
    upd = _mm(kblk, v)
    order = range(NCH) if fwd else range(NCH - 1, -1, -1)
    prev = [None] * NCH
    for c in order:
        prev[c] = state
        state = dec_fn(c) * state + upd[c * CHUNK:(c + 1) * CHUNK]
    q_blk = jnp.where(same, _mm(q_in, rep), 0.0)
    o_inter = _mm(q_blk, jnp.concatenate(prev, axis=0))
    return o_intra + o_inter, state


def _head_norm_gate(o, g, r):
    return o * lax.rsqrt(jnp.mean(o * o, axis=-1, keepdims=True) + EPS) * g * _silu(r)


def _gla_kernel(q_ref, k_ref, v_ref, r_ref, z_ref, wf_ref, bf_ref, wb_ref, bb_ref, gn_ref, s0f_ref, s0b_ref,
                o_ref, sf_ref, sb_ref, acc_ref, *, n_sc):
    same, tri_f, tri_b = _chunk_masks()
    reps = [_head_replicate(h) for h in range(2)]
    lane_head = _iota((1, LANES), 1) // DK_A

    def run(sc, states, fwd):
        off = pl.multiple_of(sc * SUPER, SUPER)
        rows = pl.ds(off, SUPER)
        q = q_ref[rows, :] * (DK_A ** -0.5)
        k = k_ref[rows, :]
        x = _mm(z_ref[rows, :], (wf_ref if fwd else wb_ref)[...]) + (bf_ref if fwd else bb_ref)[...]
        la = (jnp.minimum(x, 0.0) - jnp.log1p(jnp.exp(-jnp.abs(x)))) * (1.0 / GATE_TEMP)
        tri = tri_f if fwd else tri_b
        hi = la.astype(MXU)
        lo = la - hi.astype(F32)
        trim = tri.astype(MXU)
        b = _mm(trim, hi) + _mm(trim, lo)
        last = CHUNK - 1 if fwd else 0
        tot_rows = [b[c * CHUNK + last:c * CHUNK + last + 1, :] for c in range(NCH)]
        b_last = jnp.concatenate([jnp.broadcast_to(t, (CHUNK, LANES)) for t in tot_rows], axis=0)
        tot_t = jnp.transpose(jnp.concatenate(tot_rows + [jnp.zeros((LANES - NCH, LANES), F32)], axis=0))
        q_in = q * jnp.exp(b)
        k_out = k * jnp.exp(-b)
        kdec_t = jnp.transpose(k * jnp.exp(b_last - b))
        outs, new_states = [], []
        for h in range(2):
            head_rows = slice(h * DK_A, (h + 1) * DK_A)
            dec_fn = lambda c, t=tot_t[head_rows]: jnp.exp(jnp.broadcast_to(t[:, c:c + 1], (DK_A, DV_A)))
            qh = jnp.where(lane_head == h, q_in, 0.0)
            att = jnp.where(tri, _mm_nt(qh, k_out), 0.0)
            v = v_ref[rows, h * DV_A:(h + 1) * DV_A]
            o, s = _recur_step(qh, kdec_t[head_rows], att, v, states[h], dec_fn, same, fwd, reps[h])
            outs.append(o)
            new_states.append(s)
        return rows, outs, tuple(new_states)

    sf, sb = _both_directions(run, n_sc, acc_ref, o_ref, gn_ref, r_ref,
                              (s0f_ref[0, 0], s0f_ref[0, 1]), (s0b_ref[0, 0], s0b_ref[0, 1]))
    for h in range(2):
        sf_ref[0, h] = sf[h]
        sb_ref[0, h] = sb[h]


def _both_directions(run, n_sc, acc_ref, o_ref, gn_ref, gate_ref, sf, sb, unroll=1):
    dv = acc_ref.shape[1] // 2

    def park(rows, outs):
        for h in range(2):
            acc_ref[rows, h * dv:(h + 1) * dv] = outs[h]

    def finish(rows, outs, other=None):
        for h in range(2):
            cols = slice(h * dv, (h + 1) * dv)
            total = outs[h] + (acc_ref[rows, cols] if other is None else other[h])
            o_ref[rows, cols] = _head_norm_gate(total, gn_ref[...], gate_ref[rows, cols]).astype(o_ref.dtype)

    if n_sc == 1:
        rows, outs_f, sf = run(0, sf, True)
        _, outs_b, sb = run(0, sb, False)
        finish(rows, outs_f, outs_b)
        return sf, sb
    assert n_sc % 2 == 0

    def body(second):
        def step(i, carry):
            sf, sb = carry
            rows_f, outs_f, sf = run(i, sf, True)
            rows_b, outs_b, sb = run(n_sc - 1 - i, sb, False)
            (finish if second else park)(rows_f, outs_f)
            (finish if second else park)(rows_b, outs_b)
            return sf, sb
        return step

    carry = lax.fori_loop(0, n_sc // 2, body(False), (sf, sb), unroll=unroll)
    return lax.fori_loop(n_sc // 2, n_sc, body(True), carry, unroll=unroll)


def _pair_specs(n, col_q, col_k, col_v, col_r):
    return [pl.BlockSpec((n, LANES), lambda b, p: (b, col_q // LANES + p)),
            pl.BlockSpec((n, LANES), lambda b, p: (b, col_k // LANES + p)),
            pl.BlockSpec((n, 2 * DV_A), lambda b, p: (b, col_v // (2 * DV_A) + p)),
            pl.BlockSpec((n, 2 * DV_A), lambda b, p: (b, col_r // (2 * DV_A) + p))]


def _state_spec():
    return pl.BlockSpec((1, 2, DK_A, DV_A), lambda b, p: (b, p, 0, 0))


def _gla(P, B, n, wf, bf, wb, bb, gn, s0f, s0b):
    T = B * n
    assert n % SUPER == 0
    pair_w = pl.BlockSpec((LANES, LANES), lambda b, p: (0, p))
    pair_b = pl.BlockSpec((1, LANES), lambda b, p: (0, p))
    st_shape = jax.ShapeDtypeStruct((B, H_A, DK_A, DV_A), F32)
    return pl.pallas_call(
        functools.partial(_gla_kernel, n_sc=n // SUPER),
        grid=(B, H_A // 2),
        in_specs=_pair_specs(n, C_AQ, C_AK, C_AV, C_AR)
        + [pl.BlockSpec((n, LANES), lambda b, p: (b, C_Z // LANES)),
           pair_w, pair_b, pair_w, pair_b, _resident((1, DV_A)), _state_spec(), _state_spec()],
        out_specs=(pl.BlockSpec((n, 2 * DV_A), lambda b, p: (b, p)), _state_spec(), _state_spec()),
        out_shape=(jax.ShapeDtypeStruct((T, V_A), MXU), st_shape, st_shape),
        scratch_shapes=[pltpu.VMEM((n, 2 * DV_A), F32)],
        compiler_params=_cparams(("arbitrary", "arbitrary")),
        name="gla",
    )(P, P, P, P, P, wf, bf, wb, bb, gn, s0f, s0b)


def _ret_kernel(*refs, n_sc, rope):
    if rope:
        (q_ref, k_ref, v_ref, g_ref, ldf_ref, ldb_ref, gn_ref, s0f_ref, s0b_ref, cos_ref, sin_ref,
         o_ref, sf_ref, sb_ref, acc_ref) = refs
    else:
        (q_ref, k_ref, v_ref, g_ref, ldf_ref, ldb_ref, gn_ref, s0f_ref, s0b_ref,
         o_ref, sf_ref, sb_ref, acc_ref) = refs
    same, tri_f, tri_b = _chunk_masks()
    reps = [_head_replicate(h) for h in range(2)]
    lane_head = _iota((1, LANES), 1) // DK_B
    half = DK_B // 2
    first_half = (_iota((1, LANES), 1) % DK_B) < half
    diff = jnp.abs(_iota((SUPER, SUPER), 0) - _iota((SUPER, SUPER), 1)).astype(F32)
    pos = (_iota((SUPER, LANES), 0) % CHUNK).astype(F32)
    decays = {fwd: [jnp.where(tri_f if fwd else tri_b, jnp.exp(ld_ref[0, h:h + 1, :] * diff), 0.0)
                    for h in range(2)]
              for fwd, ld_ref in ((True, ldf_ref), (False, ldb_ref))}

    def rotate(x, rows):
        swapped = jnp.where(first_half, pltpu.roll(x, LANES - half, 1), pltpu.roll(x, half, 1))
        return x * cos_ref[rows, :] + swapped * sin_ref[rows, :]

    def run(sc, states, fwd):
        off = pl.multiple_of(sc * SUPER, SUPER)
        rows = pl.ds(off, SUPER)
        q = q_ref[rows, :]
        k = k_ref[rows, :] * (DK_B ** -0.5)
        if rope:
            q, k = rotate(q, rows), rotate(k, rows)
        ld = (ldf_ref if fwd else ldb_ref)[0]
        lg_lane = jnp.where(lane_head == 0, ld[0:1, :LANES], ld[1:2, :LANES])
        if fwd:
            q_in = q * jnp.exp(lg_lane * (pos + 1.0))
            k_dec = k * jnp.exp(lg_lane * (CHUNK - 1.0 - pos))
        else:
            q_in = q * jnp.exp(lg_lane * (CHUNK - pos))
            k_dec = k * jnp.exp(lg_lane * pos)
        kdec_t = jnp.transpose(k_dec)
        outs, new_states = [], []
        for h in range(2):
            qh = jnp.where(lane_head == h, q, 0.0)
            att = _mm_nt(qh, k) * decays[fwd][h]
            chunk_dec = jnp.exp(ld[h:h + 1, :DV_B] * float(CHUNK))
            v = v_ref[rows, h * DV_B:(h + 1) * DV_B]
            o, s = _recur_step(q_in, kdec_t[h * DK_B:(h + 1) * DK_B], att, v, states[h], lambda c: chunk_dec,
                               same, fwd, reps[h])
            outs.append(o)
            new_states.append(s)
        return rows, outs, tuple(new_states)

    sf, sb = _both_directions(run, n_sc, acc_ref, o_ref, gn_ref, g_ref,
                              (s0f_ref[0, 0], s0f_ref[0, 1]), (s0b_ref[0, 0], s0b_ref[0, 1]), unroll=2)
    for h in range(2):
        sf_ref[0, h] = sf[h]
        sb_ref[0, h] = sb[h]


def _ret(P, B, n, ldf, ldb, gn, s0f, s0b, rope_tabs):
    T = B * n
    rope = rope_tabs is not None
    ld_spec = pl.BlockSpec((1, 2, 2 * LANES), lambda b, p: (p, 0, 0))
    st_shape = jax.ShapeDtypeStruct((B, H_B, DK_B, DV_B), F32)
    ins = [P, P, P, P, ldf, ldb, gn, s0f, s0b] + (list(rope_tabs) if rope else [])
    return pl.pallas_call(
        functools.partial(_ret_kernel, n_sc=n // SUPER, rope=rope),
        grid=(B, H_B // 2),
        in_specs=_pair_specs(n, C_BQ, C_BK, C_BV, C_BG)
        + [ld_spec, ld_spec, _resident((1, DV_B)), _state_spec(), _state_spec()]
        + ([_resident((n, LANES))] * 2 if rope else []),
        out_specs=(pl.BlockSpec((n, 2 * DV_B), lambda b, p: (b, p)), _state_spec(), _state_spec()),
        out_shape=(jax.ShapeDtypeStruct((T, V_B), MXU), st_shape, st_shape),
        scratch_shapes=[pltpu.VMEM((n, 2 * DV_B), F32)],
        compiler_params=_cparams(("arbitrary", "arbitrary")),
        name="retention",
    )(*ins)


def _attn_ctx_kernel(q_ref, k_ref, v_ref, o_ref):
    lane_head = _iota((1, LANES), 1) // D_C
    q = q_ref[...] * (D_C ** -0.5)
    k = k_ref[...]
    v = v_ref[...]
    out = jnp.zeros(q.shape, F32)
    for h in range(2):
        s = _mm_nt(jnp.where(lane_head == h, q, 0.0), k)
        e = jnp.exp(s - jnp.max(s, axis=-1, keepdims=True))
        p = e / jnp.sum(e, axis=-1, keepdims=True)
        out = jnp.where(lane_head == h, _mm(p, v), out)
    o_ref[...] = out.astype(o_ref.dtype)


def _attn_ctx(P, B, n):
    spec = lambda col: pl.BlockSpec((n, LANES), lambda b, p: (b, col // LANES + p))
    return pl.pallas_call(
        _attn_ctx_kernel,
        grid=(B, H_C // 2),
        in_specs=[spec(C_CQ), spec(C_CK), spec(C_CV)],
        out_specs=pl.BlockSpec((n, LANES), lambda b, p: (b, p)),
        out_shape=jax.ShapeDtypeStruct((B * n, W_C), MXU),
        compiler_params=_cparams(("arbitrary", "arbitrary")),
        name="attn_ctx",
    )(P, P, P)


def _attn_lat_kernel(q_ref, k_ref, v_ref, kc_ref, vc_ref, bias_ref, o_ref, *, n_blk, key_start_max):
    lane_head = _iota((1, LANES), 1) // D_C
    nq, nk = Q_ROWS * GRID_W, K_ROWS * GRID_W
    kc = kc_ref[0]
    vc = vc_ref[0]

    def body(i, carry):
        qrows = pl.ds(pl.multiple_of(i * nq, nq), nq)
        kstart = jnp.clip(i * Q_ROWS - WIN_H // 2, 0, key_start_max)
        krows = pl.ds(pl.multiple_of(kstart * GRID_W, GRID_W), nk)
        variant = jnp.where(i == 0, 0, jnp.where(i == n_blk - 1, 2, 1))
        q = q_ref[qrows, :] * (D_C ** -0.5)
        k = k_ref[krows, :]
        v = v_ref[krows, :]
        out = jnp.zeros((nq, LANES), F32)
        for h in range(2):
            qh = jnp.where(lane_head == h, q, 0.0)
            s_win = _mm_nt(qh, k) + bias_ref[h, variant]
            s_ctx = _mm_nt(qh, kc)
            m = jnp.maximum(jnp.max(s_win, axis=-1, keepdims=True), jnp.max(s_ctx, axis=-1, keepdims=True))
            e_win = jnp.exp(s_win - m)
            e_ctx = jnp.exp(s_ctx - m)
            inv = 1.0 / (jnp.sum(e_win, axis=-1, keepdims=True) + jnp.sum(e_ctx, axis=-1, keepdims=True))
            o = (_mm(e_win, v) + _mm(e_ctx, vc)) * inv
            out = jnp.where(lane_head == h, o, out)
        o_ref[qrows, :] = out.astype(o_ref.dtype)
        return carry

    lax.fori_loop(0, n_blk, body, 0, unroll=2)


def _window_bias(rpb, rows):
    n_blk = rows // Q_ROWS
    kh = min(WIN_H, rows)
    qc = np.arange(GRID_W)
    win_c = np.clip(qc - WIN_W // 2, 0, GRID_W - WIN_W)
    kc = np.arange(GRID_W)
    col_ok = (kc[None, :] >= win_c[:, None]) & (kc[None, :] < win_c[:, None] + WIN_W)
    coff = np.clip(kc[None, :] - qc[:, None], -(WIN_W - 1), WIN_W - 1) + WIN_W - 1
    col_pick = (coff[:, :, None] == np.arange(2 * WIN_W - 1)).astype(np.float32)
    exact = lax.Precision.HIGHEST
    tabs = []
    for blk in (0, min(1, n_blk - 1), n_blk - 1):
        r = blk * Q_ROWS + np.arange(Q_ROWS)
        kstart = int(np.clip(blk * Q_ROWS - WIN_H // 2, 0, rows - K_ROWS))
        kr = kstart + np.arange(K_ROWS)
        r0 = np.clip(r - WIN_H // 2, 0, rows - kh)
        row_ok = (kr[None, :] >= r0[:, None]) & (kr[None, :] < r0[:, None] + kh)
        roff = np.clip(kr[None, :] - r[:, None] + WIN_H - 1, 0, 2 * WIN_H - 2)
        ok = row_ok[:, None, :, None] & col_ok[None, :, None, :]
        row_pick = (roff[:, :, None] == np.arange(2 * WIN_H - 1)).astype(np.float32)
        by_row = jnp.einsum('qka,hab->hqkb', row_pick, rpb, precision=exact)
        bias = jnp.einsum('hqkb,cdb->hqckd', by_row, col_pick, precision=exact)
        tab = jnp.where(ok[None], bias, NEG_INF)
        tabs.append(tab.reshape(rpb.shape[0], Q_ROWS * GRID_W, K_ROWS * GRID_W))
    return jnp.stack(tabs, axis=1)


def _attn_lat(P, B, n, k_ctx, v_ctx, bias):
    rows = n // GRID_W
    assert rows % Q_ROWS == 0 and rows >= K_ROWS
    L_ctx = k_ctx.shape[1]
    spec = lambda col: pl.BlockSpec((n, LANES), lambda p, b: (b, col // LANES + p))
    cspec = pl.BlockSpec((1, L_ctx, LANES), lambda p, b: (b, 0, p))
    nq, nk = Q_ROWS * GRID_W, K_ROWS * GRID_W
    return pl.pallas_call(
        functools.partial(_attn_lat_kernel, n_blk=rows // Q_ROWS, key_start_max=rows - K_ROWS),
        grid=(H_C // 2, B),
        in_specs=[spec(C_CQ), spec(C_CK), spec(C_CV), cspec, cspec,
                  pl.BlockSpec((2, 3, nq, nk), lambda p, b: (p, 0, 0, 0))],
        out_specs=pl.BlockSpec((n, LANES), lambda p, b: (b, p)),
        out_shape=jax.ShapeDtypeStruct((B * n, W_C), MXU),
        compiler_params=_cparams(("arbitrary", "arbitrary")),
        name="attn_lat",
    )(P, P, P, k_ctx, v_ctx, bias)


def _merge_kernel(x_ref, mod_ref, g_ref, oa_ref, ob_ref, oc_ref, wm_ref, wa_ref, wb_ref, wc_ref, m_ref):
    D = x_ref.shape[1]
    h = _norm_mod(x_ref[...], g_ref[...], mod_ref[0, 0:1, :], mod_ref[0, 1:2, :]).astype(MXU)
    for s, w in _col_chunks(D, 512):
        acc = None
        for i, (o_ref, wbr_ref) in enumerate(((oa_ref, wa_ref), (ob_ref, wb_ref), (oc_ref, wc_ref))):
            gate = _sigmoid(jnp.dot(h, wm_ref[:, i * D + s:i * D + s + w], preferred_element_type=F32))
            term = gate * jnp.dot(o_ref[...], wbr_ref[:, s:s + w], preferred_element_type=F32)
            acc = term if acc is None else acc + term
        m_ref[:, s:s + w] = acc.astype(m_ref.dtype)


def _merge(x, mod, g, oa, ob, oc, wm, wa, wb, wc, n_per_batch):
    T, D = x.shape
    tm = min(256, T)
    bmap = _tile_batch_map(tm, n_per_batch, mod.shape[0])
    tile = lambda w: pl.BlockSpec((tm, w), lambda i: (i, 0))
    return pl.pallas_call(
        _merge_kernel,
        grid=(T // tm,),
        in_specs=[tile(D), pl.BlockSpec((1, 6, D), bmap), _resident((1, D)), tile(V_A), tile(V_B), tile(W_C),
                  _resident(wm.shape), _resident(wa.shape), _resident(wb.shape), _resident(wc.shape)],
        out_specs=tile(D),
        out_shape=jax.ShapeDtypeStruct((T, D), MXU),
        compiler_params=_cparams(("arbitrary",)),
        name="merge",
    )(x, mod, g.reshape(1, D), oa, ob, oc, wm, wa, wb, wc)


def _post_kernel(x_ref, m_ref, mod_ref, g_ref, wo_ref, wr_ref, x1_ref, h2_ref, aff_ref, acc0_ref):
    acc0_ref[...] = jnp.zeros(acc0_ref.shape, acc0_ref.dtype)
    mod = mod_ref[0]
    x1 = x_ref[...] + mod[2:3, :] * jnp.dot(m_ref[...], wo_ref[...], preferred_element_type=F32)
    x1_ref[...] = x1
    h2 = _norm_mod(x1, g_ref[...], mod[3:4, :], mod[4:5, :])
    _write_token_major(h2_ref, (), h2)
    logits = _mm_nt(wr_ref[...], h2)
    e = jnp.exp(logits - jnp.max(logits, axis=0, keepdims=True))
    aff_ref[...] = e / jnp.sum(e, axis=0, keepdims=True)


def _post(x, merged, mod, g, wo, wr_t, n_per_batch):
    T, D = x.shape
    tm = min(512, T)
    bmap = _tile_batch_map(tm, n_per_batch, mod.shape[0])
    tile = pl.BlockSpec((tm, D), lambda i: (i, 0))
    return pl.pallas_call(
        _post_kernel,
        grid=(T // tm,),
        in_specs=[tile, tile, pl.BlockSpec((1, 6, D), bmap), _resident((1, D)), _resident(wo.shape),
                  _resident(wr_t.shape)],
        out_specs=(tile, _token_major_spec(tm, D), pl.BlockSpec((N_EXPERTS, tm), lambda i: (0, i)),
                   _token_major_spec(tm, D)),
        out_shape=(jax.ShapeDtypeStruct((T, D), F32), jax.ShapeDtypeStruct((T * _tok_rows(D), LANES), F32),
                   jax.ShapeDtypeStruct((N_EXPERTS, T), F32),
                   jax.ShapeDtypeStruct((T * _tok_rows(D), LANES), F32)),
        compiler_params=_cparams(("arbitrary",)),
        name="post",
    )(x, merged, mod, g.reshape(1, D), wo, wr_t)


def _prefix_counts(mask, upper, lower_strict):
    within = _mm(mask, upper)
    row_tot = jnp.broadcast_to(within[:, LANES - 1:LANES], within.shape)
    row_start = _mm(lower_strict, row_tot)
    return row_start + within - mask, row_start, within


def _split_int(x):
    high = jnp.floor(x * (1.0 / 256.0))
    return high, x - high * 256.0


def _select_kernel(aff_ref, idx_ref, gate_ref, *, cap, slot_tile):
    R = aff_ref.shape[1]
    upper = (_iota((LANES, LANES), 0) <= _iota((LANES, LANES), 1)).astype(MXU)
    lower_strict = (_iota((R, R), 1) < _iota((R, R), 0)).astype(MXU)
    lane0 = (_iota((8, LANES), 1) == 0).astype(MXU)
    lane_id = _iota((slot_tile, LANES), 1)
    row_id = _iota((slot_tile, R), 1)

    aff = aff_ref[0]
    bits = pltpu.bitcast(aff, I32)

    def bit_step(j, thr):
        cand = thr | jnp.left_shift(jnp.int32(1), 30 - j)
        cnt = jnp.sum((bits >= cand).astype(I32), axis=(0, 1), keepdims=True)
        return jnp.where(cnt >= cap, cand, thr)

    thr = lax.fori_loop(0, 31, bit_step, jnp.zeros((1, 1), I32))
    gt = (bits > thr).astype(F32)
    eq = (bits == thr).astype(F32)
    need = float(cap) - jnp.sum(gt, axis=(0, 1), keepdims=True)
    eq_rank, _, _ = _prefix_counts(eq, upper, lower_strict)
    sel = gt + eq * (eq_rank < need).astype(F32)
    _, row_start, within = _prefix_counts(sel, upper, lower_strict)
    row_end = row_start + jnp.broadcast_to(within[:, LANES - 1:LANES], within.shape)
    end_hi, end_lo = _split_int(row_end)
    row_end_t = (_mm_nt(lane0, end_hi) * 256.0 + _mm_nt(lane0, end_lo))[0:1, :]
    start_hi, start_lo = _split_int(row_start)
    aff_parts = _split3(aff)

    def tile(t, carry):
        base = t * slot_tile
        slot = (base + _iota((slot_tile, 1), 0)).astype(F32)
        row = jnp.sum((row_end_t <= slot).astype(F32), axis=1, keepdims=True)
        onehot = (row_id.astype(F32) == row).astype(MXU)
        start = _mm(onehot, start_hi) * 256.0 + _mm(onehot, start_lo)
        rank = slot - start
        counts = _mm(onehot, within)
        col = jnp.sum((counts <= rank).astype(F32), axis=1, keepdims=True)
        vals = sum(_mm(onehot, p) for p in aff_parts)
        gate = jnp.sum(jnp.where(lane_id.astype(F32) == col, vals, 0.0), axis=1, keepdims=True)
        token = (row * float(LANES) + col).astype(I32)
        out_rows = pl.ds(pl.multiple_of(base, slot_tile), slot_tile)
        idx_ref[0, out_rows, :] = jnp.broadcast_to(token, (slot_tile, LANES))
        gate_ref[0, out_rows, :] = jnp.broadcast_to(gate, (slot_tile, LANES))
        return carry

    lax.fori_loop(0, cap // slot_tile, tile, 0)


def _select(aff_t, cap):
    E, n = aff_t.shape
    assert n % LANES == 0
    R = n // LANES
    slot_tile = min(512, cap)
    shape = (E, cap, LANES)
    return pl.pallas_call(
        functools.partial(_select_kernel, cap=cap, slot_tile=slot_tile),
        grid=(E,),
        in_specs=[pl.BlockSpec((1, R, LANES), lambda e: (e, 0, 0))],
        out_specs=(pl.BlockSpec((1, cap, LANES), lambda e: (e, 0, 0)),
                   pl.BlockSpec((1, cap, LANES), lambda e: (e, 0, 0))),
        out_shape=(jax.ShapeDtypeStruct(shape, I32), jax.ShapeDtypeStruct(shape, F32)),
        compiler_params=_cparams(("arbitrary",)),
        name="select",
    )(aff_t.reshape(E, R, LANES))


MOE_TILE = 512
MOE_ISSUE_UNROLL = 8


def _moe_kernel(idx_ref, idx_next_ref, h_hbm, gate_ref, wg_ref, wu_ref, wd_ref, acc_in, acc_hbm, xbuf, obuf, sem,
                *, ts, per):
    del acc_in
    tok_rows = xbuf.shape[1] // ts
    t = pl.program_id(1)
    step = pl.program_id(0) * per + t
    n_steps = pl.num_programs(0) * per
    cur, nxt = lax.rem(step, 2), lax.rem(step + 1, 2)
    ring, ring_next, ring_prev = lax.rem(step, 3), lax.rem(step + 1, 3), lax.rem(step + 2, 3)
    first, last = step == 0, step == n_steps - 1
    expert_start, expert_end = t == 0, t == per - 1
    prev_was_expert_start = (t == 1) if per > 1 else True

    def issue_row(ids_ref, kind, buf, i):
        tok = pl.ds(pl.multiple_of(ids_ref[0, 0, i] * tok_rows, tok_rows), tok_rows)
        row = pl.ds(pl.multiple_of(i * tok_rows, tok_rows), tok_rows)
        if kind == 0:
            cp = pltpu.make_async_copy(h_hbm.at[tok, :], xbuf.at[buf, row, :], sem.at[0, buf])
        elif kind == 1:
            cp = pltpu.make_async_copy(acc_hbm.at[tok, :], obuf.at[buf, row, :], sem.at[1, buf])
        else:
            cp = pltpu.make_async_copy(obuf.at[buf, row, :], acc_hbm.at[tok, :], sem.at[2, buf])
        cp.start()

    def issue(ids_ref, kind, buf):
        def body(i, c):
            issue_row(ids_ref, kind, buf, i)
            return c
        lax.fori_loop(0, ts, body, 0, unroll=MOE_ISSUE_UNROLL)

    def wait_all(kind, buf):
        ref = xbuf if kind == 0 else obuf
        pltpu.make_async_copy(ref.at[buf], ref.at[buf], sem.at[kind, buf]).wait()

    @pl.when(first)
    def _():
        issue(idx_ref, 0, cur)

    @pl.when(jnp.logical_and(step >= 2, jnp.logical_not(prev_was_expert_start)))
    def _():
        wait_all(2, ring_next)

    @pl.when(expert_start)
    def _():
        @pl.when(jnp.logical_not(first))
        def _():
            wait_all(2, ring_prev)
        issue(idx_ref, 1, ring)

    wait_all(0, cur)
    wait_all(1, ring)

    @pl.when(jnp.logical_not(last))
    def _():
        issue(idx_next_ref, 0, nxt)

    @pl.when(jnp.logical_not(expert_end))
    def _():
        issue(idx_next_ref, 1, ring_next)

    x = _read_token_major(xbuf, (cur,), ts).astype(MXU)
    hidden = (_silu(jnp.dot(x, wg_ref[0], preferred_element_type=F32))
              * jnp.dot(x, wu_ref[0], preferred_element_type=F32))
    y = jnp.dot(hidden.astype(MXU), wd_ref[0], preferred_element_type=F32)
    g = gate_ref[0][:, :1]
    _write_token_major(obuf, (ring,), _read_token_major(obuf, (ring,), ts) + y * g)
    issue(idx_ref, 2, ring)

    @pl.when(last)
    def _():
        @pl.when(jnp.logical_and(jnp.logical_not(first), jnp.logical_not(expert_start)))
        def _():
            wait_all(2, ring_prev)
        wait_all(2, ring)


def _moe(h2, idx, gate, wg, wu, wd, acc0):
    D = wg.shape[1]
    tok_rows = _tok_rows(D)
    n = h2.shape[0] // tok_rows
    E, cap, _ = gate.shape
    FF = wg.shape[2]
    ts = min(MOE_TILE, cap)
    per = cap // ts
    n_steps = E * per
    idx_blocks = idx.reshape(n_steps, 1, ts)
    wspec = lambda shape: pl.BlockSpec((1,) + shape, lambda e, t: (e, 0, 0))
    ids = lambda shift: pl.BlockSpec((1, 1, ts), lambda e, t: (jnp.minimum(e * per + t + shift, n_steps - 1), 0, 0),
                                     memory_space=pltpu.SMEM)
    return pl.pallas_call(
        functools.partial(_moe_kernel, ts=ts, per=per),
        grid=(E, per),
        in_specs=[ids(0), ids(1),
                  pl.BlockSpec(memory_space=pl.ANY),
                  pl.BlockSpec((1, ts, LANES), lambda e, t: (e, t, 0)),
                  wspec((D, FF)), wspec((D, FF)), wspec((FF, D)),
                  pl.BlockSpec(memory_space=pl.ANY)],
        out_specs=pl.BlockSpec(memory_space=pl.ANY),
        out_shape=jax.ShapeDtypeStruct((n * tok_rows, LANES), F32),
        scratch_shapes=[pltpu.VMEM((2, ts * tok_rows, LANES), F32), pltpu.VMEM((3, ts * tok_rows, LANES), F32),
                        pltpu.SemaphoreType.DMA((3, 3))],
        input_output_aliases={7: 0},
        compiler_params=_cparams(("arbitrary", "arbitrary")),
        name="moe",
    )(idx_blocks, idx_blocks, h2, gate, wg, wu, wd, acc0)


def _final_kernel(xa_ref, xb_ref, mod_ref, g_ref, o_ref):
    x = xa_ref[...] + mod_ref[0, 5:6, :] * _read_token_major(xb_ref, (), xa_ref.shape[0])
    o_ref[...] = x * lax.rsqrt(jnp.mean(x * x, axis=-1, keepdims=True) + EPS) * g_ref[...]


def _final(xa, xb, mod, g, n_per_batch):
    T, D = xa.shape
    tm = min(512, T)
    tile = pl.BlockSpec((tm, D), lambda i: (i, 0))
    return pl.pallas_call(
        _final_kernel,
        grid=(T // tm,),
        in_specs=[tile, _token_major_spec(tm, D),
                  pl.BlockSpec((1, 6, D), _tile_batch_map(tm, n_per_batch, mod.shape[0])), _resident((1, D))],
        out_specs=tile,
        out_shape=jax.ShapeDtypeStruct((T, D), F32),
        compiler_params=_cparams(("arbitrary",)),
        name="final_norm",
    )(xa, xb, mod, g.reshape(1, D))


def _rope_tables(n):
    quarter = DK_B // 4
    t = jnp.arange(n)
    row = (t // GRID_W).astype(F32)
    col = (t % GRID_W).astype(F32)
    inv = ROPE_BASE ** (-jnp.arange(quarter, dtype=F32) / quarter)
    ang = jnp.concatenate([row[:, None] * inv, col[:, None] * inv], axis=-1)
    cos, sin = jnp.cos(ang), jnp.sin(ang)
    cos_t = jnp.concatenate([cos, cos] * (LANES // DK_B), axis=-1)
    sin_t = jnp.concatenate([-sin, sin] * (LANES // DK_B), axis=-1)
    return cos_t, sin_t


def _prep_layer(l, w_in, gla_w_gf, gla_b_gf, gla_w_gb, gla_b_gb, ret_ld_f, ret_ld_b):
    D = w_in.shape[1]
    w = w_in[l]
    s = np.cumsum([0, QK_A, QK_A, V_A, V_A, GATE_RANK, GATE_RANK, QK_B, QK_B, V_B, V_B, W_C, W_C, W_C])
    a_q, a_k, a_v, a_r, a_zf, a_zb, b_q, b_k, b_v, b_g, c_q, c_k, c_v = [w[:, s[i]:s[i + 1]] for i in range(13)]
    zpad = jnp.zeros((D, LANES - 2 * GATE_RANK), w.dtype)
    w1 = jnp.concatenate([a_q, a_k, a_v, a_r, b_q, b_k, b_v, b_g, c_q, c_k, c_v, a_zf, a_zb, zpad],
                         axis=1).astype(MXU)
    wm = w[:, MIX_W:].astype(MXU)
    wf = jnp.zeros((LANES, QK_A), F32).at[:GATE_RANK].set(gla_w_gf[l])
    wb = jnp.zeros((LANES, QK_A), F32).at[GATE_RANK:2 * GATE_RANK].set(gla_w_gb[l])
    ld = lambda v: jnp.broadcast_to(v[l].reshape(H_B // 2, 2, 1), (H_B // 2, 2, 2 * LANES))
    return dict(w1=w1, wm=wm, wf=wf, bf=gla_b_gf[l].reshape(1, QK_A), wb=wb, bb=gla_b_gb[l].reshape(1, QK_A),
                ldf=ld(ret_ld_f), ldb=ld(ret_ld_b))


def kernel(x_prompt, x_sample, cache_nat_k, cache_nat_v, state_gla_fwd, state_gla_bwd, state_ret_fwd,
           state_ret_bwd, c, c_ctx, norm1, norm2, w_ada, b_ada, w_in, gla_w_gf, gla_b_gf, gla_w_gb, gla_b_gb,
           gla_gn, ret_log_decay_f, ret_log_decay_b, ret_gn, nat_rpb, w_br_a, w_br_b, w_br_c, w_out, w_router,
           w_gate, w_up, w_down, final_norm):
    B, n_ctx, D = x_prompt.shape
    Bd, n_lat, _ = x_sample.shape
    L = w_in.shape[0]
    past = cache_nat_k.shape[2]

    cond = jnp.concatenate([c_ctx[None, :], c], axis=0)
    n_cond = cond.shape[0]
    cond = jnp.pad(cond, ((0, -n_cond % 8), (0, 0)))
    mod_all = _ada(cond, w_ada, b_ada).reshape(L, cond.shape[0], 6, D)
    rope_tabs = _rope_tables(n_lat)
    zero_a = jnp.zeros((B, H_A, DK_A, DV_A), F32)
    zero_b = jnp.zeros((B, H_B, DK_B, DV_B), F32)

    paths = {
        "ctx": dict(x=x_prompt.reshape(B * n_ctx, D), moe=None, B=B, n=n_ctx),
        "lat": dict(x=x_sample.reshape(Bd * n_lat, D), moe=None, B=Bd, n=n_lat),
    }
    outs = dict(nk=[], nv=[], gf=[], gb=[], rf=[], rb=[])
    mod_prev = {}
    for l in range(L):
        w = _prep_layer(l, w_in, gla_w_gf, gla_b_gf, gla_w_gb, gla_b_gb, ret_log_decay_f, ret_log_decay_b)
        wa, wb_, wc = w_br_a[l].astype(MXU), w_br_b[l].astype(MXU), w_br_c[l].astype(MXU)
        wo = w_out[l].astype(MXU)
        wr_t = w_router[l].T.astype(MXU)
        wg, wu, wd = w_gate[l].astype(MXU), w_up[l].astype(MXU), w_down[l].astype(MXU)
        gn_a, gn_b = gla_gn[l].reshape(1, DV_A), ret_gn[l].reshape(1, DV_B)
        bias = _window_bias(nat_rpb[l].astype(F32), n_lat // GRID_W)
        mods = {"ctx": mod_all[l, 0:1], "lat": mod_all[l, 1:n_cond]}
        for name, st in paths.items():
            Bp, n = st["B"], st["n"]
            latent = name == "lat"
            mod = mods[name]
            P, x, nat_k, nat_v = _in_proj(st["x"], st["moe"], mod_prev.get(name), mod, norm1[l], w["w1"], n,
                                          emit_kv=not latent)
            if latent:
                sa_f0, sa_b0 = state_gla_fwd[:, l].astype(F32), state_gla_bwd[:, l].astype(F32)
                sb_f0, sb_b0 = state_ret_fwd[:, l].astype(F32), state_ret_bwd[:, l].astype(F32)
            else:
                sa_f0 = sa_b0 = zero_a
                sb_f0 = sb_b0 = zero_b
            o_a, sa_f, sa_b = _gla(P, Bp, n, w["wf"], w["bf"], w["wb"], w["bb"], gn_a, sa_f0, sa_b0)
            o_b, sb_f, sb_b = _ret(P, Bp, n, w["ldf"], w["ldb"], gn_b, sb_f0, sb_b0,
                                   rope_tabs if latent else None)
            if latent:
                k_ctx = cache_nat_k[:, l].astype(F32).reshape(Bp, past, W_C)
                v_ctx = cache_nat_v[:, l].astype(F32).reshape(Bp, past, W_C)
                o_c = _attn_lat(P, Bp, n, k_ctx, v_ctx, bias)
            else:
                o_c = _attn_ctx(P, Bp, n)
                outs["nk"].append(nat_k.reshape(Bp, n, H_C, D_C))
                outs["nv"].append(nat_v.reshape(Bp, n, H_C, D_C))
                outs["gf"].append(sa_f)
                outs["gb"].append(sa_b)
                outs["rf"].append(sb_f)
                outs["rb"].append(sb_b)
            merged = _merge(x, mod, norm1[l], o_a, o_b, o_c, w["wm"], wa, wb_, wc, n)
            x1, h2, aff_t, acc0 = _post(x, merged, mod, norm2[l], wo, wr_t, n)
            cap = EC_FACTOR * (Bp * n) // N_EXPERTS
            idx, gate = _select(aff_t, cap)
            st["moe"] = _moe(h2, idx[:, :, 0], gate, wg, wu, wd, acc0)
            st["x"] = x1
            mod_prev[name] = mod
    y = {name: _final(st["x"], st["moe"], mod_prev[name], final_norm, st["n"]) for name, st in paths.items()}
    stack = lambda xs: jnp.stack(xs, axis=1)
    return (y["ctx"].reshape(B, n_ctx, D), y["lat"].reshape(Bd, n_lat, D), stack(outs["nk"]), stack(outs["nv"]),
            stack(outs["gf"]), stack(outs["gb"]), stack(outs["rf"]), stack(outs["rb"]))
```

```python
import functools

import numpy as np
import jax
import jax.numpy as jnp
from jax import lax
from jax.experimental import pallas as pl
from jax.experimental.pallas import tpu as pltpu

F32 = jnp.float32
I32 = jnp.int32
MXU = jnp.bfloat16

GRID_W = 64
H_A, DK_A, DV_A = 4, 64, 128
GATE_RANK, GATE_TEMP = 16, 16.0
H_B, DK_B, DV_B = 4, 64, 128
H_C, D_C = 8, 64
WIN_H, WIN_W = 8, 16
CHUNK = 64
N_EXPERTS, EC_FACTOR = 16, 2
ROPE_BASE = 10000.0
EPS = 1e-6
NEG_INF = -1e30

QK_A, V_A = H_A * DK_A, H_A * DV_A
QK_B, V_B = H_B * DK_B, H_B * DV_B
W_C = H_C * D_C
C_AQ, C_AK, C_AR = 0, 256, 512
C_BQ, C_BK, C_BG, C_Z = 1024, 1280, 1536, 2048
C_AV, C_BV, C_CQ, C_CK, C_CV = 0, 512, 1024, 1536, 2048
MIX_W = 4640

LANES = 128
SUPER = 256
NCH = SUPER // CHUNK
Q_ROWS = 4
K_ROWS = Q_ROWS + WIN_H
VMEM_LIMIT = 56 * 1024 * 1024


def _cparams(sem):
    return pltpu.CompilerParams(dimension_semantics=sem, vmem_limit_bytes=VMEM_LIMIT)


def _mm(a, b):
    return jnp.dot(a.astype(MXU), b.astype(MXU), preferred_element_type=F32)


def _mm_nt(a, b):
    return lax.dot_general(a.astype(MXU), b.astype(MXU), (((1,), (1,)), ((), ())), preferred_element_type=F32)


def _iota(shape, dim):
    return lax.broadcasted_iota(I32, shape, dim)


def _sigmoid(x):
    return 1.0 / (1.0 + jnp.exp(-x))


def _silu(x):
    return x * _sigmoid(x)


def _resident(shape):
    nd = len(shape)
    return pl.BlockSpec(shape, lambda *_: (0,) * nd, pipeline_mode=pl.Buffered(1))


def _ada_kernel(cond_ref, w_ref, b_ref, o_ref):
    o_ref[0] = _mm(_silu(cond_ref[...]), w_ref[0]) + b_ref[0]


def _ada(cond, w_ada, b_ada):
    L, D, D6 = w_ada.shape
    R = cond.shape[0]
    tn = 1024 if D6 % 1024 == 0 else D
    assert D6 % tn == 0
    return pl.pallas_call(
        _ada_kernel,
        grid=(L, D6 // tn),
        in_specs=[pl.BlockSpec((R, D), lambda l, j: (0, 0)),
                  pl.BlockSpec((1, D, tn), lambda l, j: (l, 0, j)),
                  pl.BlockSpec((1, 1, tn), lambda l, j: (l, 0, j))],
        out_specs=pl.BlockSpec((1, R, tn), lambda l, j: (l, 0, j)),
        out_shape=jax.ShapeDtypeStruct((L, R, D6), F32),
        compiler_params=_cparams(("arbitrary", "arbitrary")),
        name="ada",
    )(cond, w_ada, b_ada.reshape(L, 1, D6))


def _norm_mod(x, g, shift, scale):
    y = x * lax.rsqrt(jnp.mean(x * x, axis=-1, keepdims=True) + EPS) * g
    return y * (1.0 + scale) + shift


def _tile_batch_map(tm, n_per_batch, nb):
    if nb == 1:
        return lambda i: (0, 0, 0)
    assert n_per_batch % tm == 0
    per = n_per_batch // tm
    return lambda i: (i // per, 0, 0)


def _col_chunks(width, step):
    return [(s, min(step, width - s)) for s in range(0, width, step)]


def _tok_rows(D):
    assert D % LANES == 0
    return D // LANES


def _read_token_major(ref, lead, n_tok):
    r = ref.shape[-2] // n_tok
    parts = [ref[lead + (pl.ds(j, n_tok, stride=r), slice(None))] for j in range(r)]
    return jnp.concatenate(parts, axis=1)


def _write_token_major(ref, lead, value):
    n_tok = value.shape[0]
    r = ref.shape[-2] // n_tok
    for j in range(r):
        ref[lead + (pl.ds(j, n_tok, stride=r), slice(None))] = value[:, j * LANES:(j + 1) * LANES]


def _token_major_spec(tm, D):
    return pl.BlockSpec((tm * _tok_rows(D), LANES), lambda i: (i, 0))


def _in_kernel(*refs, has_res, emit_kv):
    refs = list(refs)
    xa_ref = refs.pop(0)
    x = xa_ref[...]
    if has_res:
        xb_ref, modp_ref = refs.pop(0), refs.pop(0)
        x = x + modp_ref[0, 5:6, :] * _read_token_major(xb_ref, (), xa_ref.shape[0])
    mod_ref, g_ref, wa_ref, wb_ref, pa_ref, pb_ref = refs[:6]
    outs = refs[6:]
    if has_res:
        outs.pop(0)[...] = x
    h = _norm_mod(x, g_ref[...], mod_ref[0, 0:1, :], mod_ref[0, 1:2, :]).astype(MXU)
    for s, w in _col_chunks(wa_ref.shape[1], 512):
        pa_ref[:, s:s + w] = jnp.dot(h, wa_ref[:, s:s + w], preferred_element_type=F32)
    for s, w in _col_chunks(wb_ref.shape[1], 512):
        part = jnp.dot(h, wb_ref[:, s:s + w], preferred_element_type=F32)
        pb_ref[:, s:s + w] = part.astype(pb_ref.dtype)
        if emit_kv:
            if s == C_CK:
                outs[0][...] = part
            if s == C_CV:
                outs[1][...] = part


def _in_proj(xa, xb, mod_prev, mod, g, w1a, w1b, n_per_batch, emit_kv):
    T, D = xa.shape
    nb = mod.shape[0]
    tm = min(256, T)
    bmap = _tile_batch_map(tm, n_per_batch, nb)
    tile = lambda w: pl.BlockSpec((tm, w), lambda i: (i, 0))
    mspec = pl.BlockSpec((1, 6, D), bmap)
    has_res = xb is not None
    assert W_C == 512 and C_CK % 512 == 0 and C_CV % 512 == 0
    ins = [xa] + ([xb, mod_prev] if has_res else []) + [mod, g.reshape(1, D), w1a, w1b]
    in_specs = ([tile(D)] + ([_token_major_spec(tm, D), mspec] if has_res else [])
                + [mspec, _resident((1, D)), _resident(w1a.shape), _resident(w1b.shape)])
    outs = ([(w1a.shape[1], F32), (w1b.shape[1], MXU)] + ([(D, F32)] if has_res else [])
            + ([(W_C, F32), (W_C, F32)] if emit_kv else []))
    out = list(pl.pallas_call(
        functools.partial(_in_kernel, has_res=has_res, emit_kv=emit_kv),
        grid=(T // tm,),
        in_specs=in_specs,
        out_specs=tuple(tile(w) for w, _ in outs),
        out_shape=tuple(jax.ShapeDtypeStruct((T, w), dt) for w, dt in outs),
        compiler_params=_cparams(("arbitrary",)),
        name="in_proj",
    )(*ins))
    Pa, Pb = out.pop(0), out.pop(0)
    x = out.pop(0) if has_res else xa
    k, v = out if emit_kv else (None, None)
    return Pa, Pb, x, k, v


def _split3(x):
    p1 = x.astype(MXU)
    r1 = x - p1.astype(F32)
    p2 = r1.astype(MXU)
    p3 = (r1 - p2.astype(F32)).astype(MXU)
    return p1, p2, p3


def _chunk_masks():
    row, col = _iota((SUPER, SUPER), 0), _iota((SUPER, SUPER), 1)
    same = (row // CHUNK) == (col // CHUNK)
    return same, same & (col <= row), same & (col >= row)


def _head_replicate(h):
    return (_iota((LANES, SUPER), 0) == (_iota((LANES, SUPER), 1) % CHUNK) + DK_A * h).astype(MXU)


def _recur_step(q_in, kdec_t, att, v, state, dec_fn, same, fwd, rep):
    o_intra = _mm(att, v)
    kblk = jnp.where(same, jnp.concatenate([kdec_t] * NCH, axis=0), 0.0)
    upd = _mm(kblk, v)
    order = range(NCH) if fwd else range(NCH - 1, -1, -1)
    prev = [None] * NCH
    for c in order:
        prev[c] = state
        state = dec_fn(c) * state + upd[c * CHUNK:(c + 1) * CHUNK]
    q_blk = jnp.where(same, _mm(q_in, rep), 0.0)
    o_inter = _mm(q_blk, jnp.concatenate(prev, axis=0))
    return o_intra + o_inter, state


def _head_norm_gate(o, g, r):
    return o * lax.rsqrt(jnp.mean(o * o, axis=-1, keepdims=True) + EPS) * g * _silu(r)


def _gla_kernel(q_ref, k_ref, v_ref, r_ref, z_ref, wf_ref, bf_ref, wb_ref, bb_ref, gn_ref, s0f_ref, s0b_ref,
                o_ref, sf_ref, sb_ref, acc_ref, *, n_sc):
    same, tri_f, tri_b = _chunk_masks()
    reps = [_head_replicate(h) for h in range(2)]
    lane_head = _iota((1, LANES), 1) // DK_A

    def run(sc, states, fwd):
        off = pl.multiple_of(sc * SUPER, SUPER)
        rows = pl.ds(off, SUPER)
        q = q_ref[rows, :] * (DK_A ** -0.5)
        k = k_ref[rows, :]
        x = _mm(z_ref[rows, :], (wf_ref if fwd else wb_ref)[...]) + (bf_ref if fwd else bb_ref)[...]
        la = (jnp.minimum(x, 0.0) - jnp.log1p(jnp.exp(-jnp.abs(x)))) * (1.0 / GATE_TEMP)
        tri = tri_f if fwd else tri_b
        hi = la.astype(MXU)
        lo = la - hi.astype(F32)
        trim = tri.astype(MXU)
        b = _mm(trim, hi) + _mm(trim, lo)
        last = CHUNK - 1 if fwd else 0
        tot_rows = [b[c * CHUNK + last:c * CHUNK + last + 1, :] for c in range(NCH)]
        b_last = jnp.concatenate([jnp.broadcast_to(t, (CHUNK, LANES)) for t in tot_rows], axis=0)
        tot_t = jnp.transpose(jnp.concatenate(tot_rows + [jnp.zeros((LANES - NCH, LANES), F32)], axis=0))
        q_in = q * jnp.exp(b)
        k_out = k * jnp.exp(-b)
        kdec_t = jnp.transpose(k * jnp.exp(b_last - b))
        outs, new_states = [], []
        for h in range(2):
            head_rows = slice(h * DK_A, (h + 1) * DK_A)
            dec_fn = lambda c, t=tot_t[head_rows]: jnp.exp(jnp.broadcast_to(t[:, c:c + 1], (DK_A, DV_A)))
            qh = jnp.where(lane_head == h, q_in, 0.0)
            att = jnp.where(tri, _mm_nt(qh, k_out), 0.0)
            v = v_ref[rows, h * DV_A:(h + 1) * DV_A]
            o, s = _recur_step(qh, kdec_t[head_rows], att, v, states[h], dec_fn, same, fwd, reps[h])
            outs.append(o)
            new_states.append(s)
        return rows, outs, tuple(new_states)

    sf, sb = _both_directions(run, n_sc, acc_ref, o_ref, gn_ref, r_ref,
                              (s0f_ref[0, 0], s0f_ref[0, 1]), (s0b_ref[0, 0], s0b_ref[0, 1]))
    for h in range(2):
        sf_ref[0, h] = sf[h]
        sb_ref[0, h] = sb[h]


def _both_directions(run, n_sc, acc_ref, o_ref, gn_ref, gate_ref, sf, sb, unroll=1):
    dv = acc_ref.shape[1] // 2

    def park(rows, outs):
        for h in range(2):
            acc_ref[rows, h * dv:(h + 1) * dv] = outs[h]

    def finish(rows, outs, other=None):
        for h in range(2):
            cols = slice(h * dv, (h + 1) * dv)
            total = outs[h] + (acc_ref[rows, cols] if other is None else other[h])
            o_ref[rows, cols] = _head_norm_gate(total, gn_ref[...], gate_ref[rows, cols]).astype(o_ref.dtype)

    if n_sc == 1:
        rows, outs_f, sf = run(0, sf, True)
        _, outs_b, sb = run(0, sb, False)
        finish(rows, outs_f, outs_b)
        return sf, sb
    assert n_sc % 2 == 0

    def body(second):
        def step(i, carry):
            sf, sb = carry
            rows_f, outs_f, sf = run(i, sf, True)
            rows_b, outs_b, sb = run(n_sc - 1 - i, sb, False)
            (finish if second else park)(rows_f, outs_f)
            (finish if second else park)(rows_b, outs_b)
            return sf, sb
        return step

    carry = lax.fori_loop(0, n_sc // 2, body(False), (sf, sb), unroll=unroll)
    return lax.fori_loop(n_sc // 2, n_sc, body(True), carry, unroll=unroll)


def _pair_specs(n, col_q, col_k, col_v, col_r):
    return [pl.BlockSpec((n, LANES), lambda b, p: (b, col_q // LANES + p)),
            pl.BlockSpec((n, LANES), lambda b, p: (b, col_k // LANES + p)),
            pl.BlockSpec((n, 2 * DV_A), lambda b, p: (b, col_v // (2 * DV_A) + p)),
            pl.BlockSpec((n, 2 * DV_A), lambda b, p: (b, col_r // (2 * DV_A) + p))]


def _state_spec():
    return pl.BlockSpec((1, 2, DK_A, DV_A), lambda b, p: (b, p, 0, 0))


def _gla(Pa, Pb, B, n, wf, bf, wb, bb, gn, s0f, s0b):
    T = B * n
    assert n % SUPER == 0
    pair_w = pl.BlockSpec((LANES, LANES), lambda b, p: (0, p))
    pair_b = pl.BlockSpec((1, LANES), lambda b, p: (0, p))
    st_shape = jax.ShapeDtypeStruct((B, H_A, DK_A, DV_A), F32)
    return pl.pallas_call(
        functools.partial(_gla_kernel, n_sc=n // SUPER),
        grid=(B, H_A // 2),
        in_specs=_pair_specs(n, C_AQ, C_AK, C_AV, C_AR)
        + [pl.BlockSpec((n, LANES), lambda b, p: (b, C_Z // LANES)),
           pair_w, pair_b, pair_w, pair_b, _resident((1, DV_A)), _state_spec(), _state_spec()],
        out_specs=(pl.BlockSpec((n, 2 * DV_A), lambda b, p: (b, p)), _state_spec(), _state_spec()),
        out_shape=(jax.ShapeDtypeStruct((T, V_A), MXU), st_shape, st_shape),
        scratch_shapes=[pltpu.VMEM((n, 2 * DV_A), F32)],
        compiler_params=_cparams(("arbitrary", "arbitrary")),
        name="gla",
    )(Pa, Pa, Pb, Pa, Pa, wf, bf, wb, bb, gn, s0f, s0b)


def _ret_kernel(*refs, n_sc, rope):
    if rope:
        (q_ref, k_ref, v_ref, g_ref, ldf_ref, ldb_ref, gn_ref, s0f_ref, s0b_ref, cos_ref, sin_ref,
         o_ref, sf_ref, sb_ref, acc_ref) = refs
    else:
        (q_ref, k_ref, v_ref, g_ref, ldf_ref, ldb_ref, gn_ref, s0f_ref, s0b_ref,
         o_ref, sf_ref, sb_ref, acc_ref) = refs
    same, tri_f, tri_b = _chunk_masks()
    reps = [_head_replicate(h) for h in range(2)]
    lane_head = _iota((1, LANES), 1) // DK_B
    half = DK_B // 2
    first_half = (_iota((1, LANES), 1) % DK_B) < half
    diff = jnp.abs(_iota((SUPER, SUPER), 0) - _iota((SUPER, SUPER), 1)).astype(F32)
    pos = (_iota((SUPER, LANES), 0) % CHUNK).astype(F32)
    decays = {fwd: [jnp.where(tri_f if fwd else tri_b, jnp.exp(ld_ref[0, h:h + 1, :] * diff), 0.0)
                    for h in range(2)]
              for fwd, ld_ref in ((True, ldf_ref), (False, ldb_ref))}

    def rotate(x, rows):
        swapped = jnp.where(first_half, pltpu.roll(x, LANES - half, 1), pltpu.roll(x, half, 1))
        return x * cos_ref[rows, :] + swapped * sin_ref[rows, :]

    def run(sc, states, fwd):
        off = pl.multiple_of(sc * SUPER, SUPER)
        rows = pl.ds(off, SUPER)
        q = q_ref[rows, :]
        k = k_ref[rows, :] * (DK_B ** -0.5)
        if rope:
            q, k = rotate(q, rows), rotate(k, rows)
        ld = (ldf_ref if fwd else ldb_ref)[0]
        lg_lane = jnp.where(lane_head == 0, ld[0:1, :LANES], ld[1:2, :LANES])
        if fwd:
            q_in = q * jnp.exp(lg_lane * (pos + 1.0))
            k_dec = k * jnp.exp(lg_lane * (CHUNK - 1.0 - pos))
        else:
            q_in = q * jnp.exp(lg_lane * (CHUNK - pos))
            k_dec = k * jnp.exp(lg_lane * pos)
        kdec_t = jnp.transpose(k_dec)
        outs, new_states = [], []
        for h in range(2):
            qh = jnp.where(lane_head == h, q, 0.0)
            att = _mm_nt(qh, k) * decays[fwd][h]
            chunk_dec = jnp.exp(ld[h:h + 1, :DV_B] * float(CHUNK))
            v = v_ref[rows, h * DV_B:(h + 1) * DV_B]
            o, s = _recur_step(q_in, kdec_t[h * DK_B:(h + 1) * DK_B], att, v, states[h], lambda c: chunk_dec,
                               same, fwd, reps[h])
            outs.append(o)
            new_states.append(s)
        return rows, outs, tuple(new_states)

    sf, sb = _both_directions(run, n_sc, acc_ref, o_ref, gn_ref, g_ref,
                              (s0f_ref[0, 0], s0f_ref[0, 1]), (s0b_ref[0, 0], s0b_ref[0, 1]), unroll=2)
    for h in range(2):
        sf_ref[0, h] = sf[h]
        sb_ref[0, h] = sb[h]


def _ret(Pa, Pb, B, n, ldf, ldb, gn, s0f, s0b, rope_tabs):
    T = B * n
    rope = rope_tabs is not None
    ld_spec = pl.BlockSpec((1, 2, 2 * LANES), lambda b, p: (p, 0, 0))
    st_shape = jax.ShapeDtypeStruct((B, H_B, DK_B, DV_B), F32)
    ins = [Pa, Pa, Pb, Pa, ldf, ldb, gn, s0f, s0b] + (list(rope_tabs) if rope else [])
    return pl.pallas_call(
        functools.partial(_ret_kernel, n_sc=n // SUPER, rope=rope),
        grid=(B, H_B // 2),
        in_specs=_pair_specs(n, C_BQ, C_BK, C_BV, C_BG)
        + [ld_spec, ld_spec, _resident((1, DV_B)), _state_spec(), _state_spec()]
        + ([_resident((n, LANES))] * 2 if rope else []),
        out_specs=(pl.BlockSpec((n, 2 * DV_B), lambda b, p: (b, p)), _state_spec(), _state_spec()),
        out_shape=(jax.ShapeDtypeStruct((T, V_B), MXU), st_shape, st_shape),
        scratch_shapes=[pltpu.VMEM((n, 2 * DV_B), F32)],
        compiler_params=_cparams(("arbitrary", "arbitrary")),
        name="retention",
    )(*ins)


def _attn_ctx_kernel(q_ref, k_ref, v_ref, o_ref):
    lane_head = _iota((1, LANES), 1) // D_C
    q = q_ref[...] * jnp.asarray(D_C ** -0.5, q_ref.dtype)
    k = k_ref[...]
    v = v_ref[...]
    out = jnp.zeros(q.shape, F32)
    for h in range(2):
        s = _mm_nt(jnp.where(lane_head == h, q, jnp.zeros_like(q)), k)
        e = jnp.exp(s - jnp.max(s, axis=-1, keepdims=True))
        p = e / jnp.sum(e, axis=-1, keepdims=True)
        out = jnp.where(lane_head == h, _mm(p, v), out)
    o_ref[...] = out.astype(o_ref.dtype)


def _attn_ctx(P, B, n):
    spec = lambda col: pl.BlockSpec((n, LANES), lambda b, p: (b, col // LANES + p))
    return pl.pallas_call(
        _attn_ctx_kernel,
        grid=(B, H_C // 2),
        in_specs=[spec(C_CQ), spec(C_CK), spec(C_CV)],
        out_specs=pl.BlockSpec((n, LANES), lambda b, p: (b, p)),
        out_shape=jax.ShapeDtypeStruct((B * n, W_C), MXU),
        compiler_params=_cparams(("arbitrary", "arbitrary")),
        name="attn_ctx",
    )(P, P, P)


def _attn_lat_kernel(q_ref, k_ref, v_ref, kc_ref, vc_ref, bias_ref, o_ref, *, n_blk, key_start_max):
    lane_head = _iota((1, LANES), 1) // D_C
    nq, nk = Q_ROWS * GRID_W, K_ROWS * GRID_W
    kc = kc_ref[0].astype(MXU)
    vc = vc_ref[0].astype(MXU)

    def body(i, carry):
        qrows = pl.ds(pl.multiple_of(i * nq, nq), nq)
        kstart = jnp.clip(i * Q_ROWS - WIN_H // 2, 0, key_start_max)
        krows = pl.ds(pl.multiple_of(kstart * GRID_W, GRID_W), nk)
        variant = jnp.where(i == 0, 0, jnp.where(i == n_blk - 1, 2, 1))
        q = q_ref[qrows, :] * jnp.asarray(D_C ** -0.5, q_ref.dtype)
        k = k_ref[krows, :]
        v = v_ref[krows, :]
        out = jnp.zeros((nq, LANES), F32)
        for h in range(2):
            qh = jnp.where(lane_head == h, q, jnp.zeros_like(q))
            s_win = _mm_nt(qh, k) + bias_ref[h, variant]
            s_ctx = _mm_nt(qh, kc)
            m = jnp.maximum(jnp.max(s_win, axis=-1, keepdims=True), jnp.max(s_ctx, axis=-1, keepdims=True))
            e_win = jnp.exp(s_win - m)
            e_ctx = jnp.exp(s_ctx - m)
            inv = 1.0 / (jnp.sum(e_win, axis=-1, keepdims=True) + jnp.sum(e_ctx, axis=-1, keepdims=True))
            o = (_mm(e_win, v) + _mm(e_ctx, vc)) * inv
            out = jnp.where(lane_head == h, o, out)
        o_ref[qrows, :] = out.astype(o_ref.dtype)
        return carry

    lax.fori_loop(0, n_blk, body, 0, unroll=2)


def _window_bias(rpb, rows):
    n_blk = rows // Q_ROWS
    kh = min(WIN_H, rows)
    qc = np.arange(GRID_W)
    win_c = np.clip(qc - WIN_W // 2, 0, GRID_W - WIN_W)
    kc = np.arange(GRID_W)
    col_ok = (kc[None, :] >= win_c[:, None]) & (kc[None, :] < win_c[:, None] + WIN_W)
    coff = np.clip(kc[None, :] - qc[:, None], -(WIN_W - 1), WIN_W - 1) + WIN_W - 1
    col_pick = (coff[:, :, None] == np.arange(2 * WIN_W - 1)).astype(np.float32)
    exact = lax.Precision.HIGHEST
    tabs = []
    for blk in (0, min(1, n_blk - 1), n_blk - 1):
        r = blk * Q_ROWS + np.arange(Q_ROWS)
        kstart = int(np.clip(blk * Q_ROWS - WIN_H // 2, 0, rows - K_ROWS))
        kr = kstart + np.arange(K_ROWS)
        r0 = np.clip(r - WIN_H // 2, 0, rows - kh)
        row_ok = (kr[None, :] >= r0[:, None]) & (kr[None, :] < r0[:, None] + kh)
        roff = np.clip(kr[None, :] - r[:, None] + WIN_H - 1, 0, 2 * WIN_H - 2)
        ok = row_ok[:, None, :, None] & col_ok[None, :, None, :]
        row_pick = (roff[:, :, None] == np.arange(2 * WIN_H - 1)).astype(np.float32)
        by_row = jnp.einsum('qka,hab->hqkb', row_pick, rpb, precision=exact)
        bias = jnp.einsum('hqkb,cdb->hqckd', by_row, col_pick, precision=exact)
        tab = jnp.where(ok[None], bias, NEG_INF)
        tabs.append(tab.reshape(rpb.shape[0], Q_ROWS * GRID_W, K_ROWS * GRID_W))
    return jnp.stack(tabs, axis=1)


def _attn_lat(P, B, n, k_ctx, v_ctx, bias):
    rows = n // GRID_W
    assert rows % Q_ROWS == 0 and rows >= K_ROWS
    L_ctx = k_ctx.shape[1]
    spec = lambda col: pl.BlockSpec((n, LANES), lambda p, b: (b, col // LANES + p))
    cspec = pl.BlockSpec((1, L_ctx, LANES), lambda p, b: (b, 0, p))
    nq, nk = Q_ROWS * GRID_W, K_ROWS * GRID_W
    return pl.pallas_call(
        functools.partial(_attn_lat_kernel, n_blk=rows // Q_ROWS, key_start_max=rows - K_ROWS),
        grid=(H_C // 2, B),
        in_specs=[spec(C_CQ), spec(C_CK), spec(C_CV), cspec, cspec,
                  pl.BlockSpec((2, 3, nq, nk), lambda p, b: (p, 0, 0, 0))],
        out_specs=pl.BlockSpec((n, LANES), lambda p, b: (b, p)),
        out_shape=jax.ShapeDtypeStruct((B * n, W_C), MXU),
        compiler_params=_cparams(("arbitrary", "arbitrary")),
        name="attn_lat",
    )(P, P, P, k_ctx, v_ctx, bias)


def _merge_kernel(x_ref, mod_ref, g_ref, oa_ref, ob_ref, oc_ref, wm_ref, wa_ref, wb_ref, wc_ref, m_ref):
    D = x_ref.shape[1]
    h = _norm_mod(x_ref[...], g_ref[...], mod_ref[0, 0:1, :], mod_ref[0, 1:2, :]).astype(MXU)
    for s, w in _col_chunks(D, 512):
        acc = None
        for i, (o_ref, wbr_ref) in enumerate(((oa_ref, wa_ref), (ob_ref, wb_ref), (oc_ref, wc_ref))):
            gate = _sigmoid(jnp.dot(h, wm_ref[:, i * D + s:i * D + s + w], preferred_element_type=F32))
            term = gate * jnp.dot(o_ref[...], wbr_ref[:, s:s + w], preferred_element_type=F32)
            acc = term if acc is None else acc + term
        m_ref[:, s:s + w] = acc.astype(m_ref.dtype)


def _merge(x, mod, g, oa, ob, oc, wm, wa, wb, wc, n_per_batch):
    T, D = x.shape
    tm = min(256, T)
    bmap = _tile_batch_map(tm, n_per_batch, mod.shape[0])
    tile = lambda w: pl.BlockSpec((tm, w), lambda i: (i, 0))
    return pl.pallas_call(
        _merge_kernel,
        grid=(T // tm,),
        in_specs=[tile(D), pl.BlockSpec((1, 6, D), bmap), _resident((1, D)), tile(V_A), tile(V_B), tile(W_C),
                  _resident(wm.shape), _resident(wa.shape), _resident(wb.shape), _resident(wc.shape)],
        out_specs=tile(D),
        out_shape=jax.ShapeDtypeStruct((T, D), MXU),
        compiler_params=_cparams(("arbitrary",)),
        name="merge",
    )(x, mod, g.reshape(1, D), oa, ob, oc, wm, wa, wb, wc)


def _post_kernel(x_ref, m_ref, mod_ref, g_ref, wo_ref, wr_ref, x1_ref, h2_ref, aff_ref, acc0_ref):
    acc0_ref[...] = jnp.zeros(acc0_ref.shape, acc0_ref.dtype)
    mod = mod_ref[0]
    x1 = x_ref[...] + mod[2:3, :] * jnp.dot(m_ref[...], wo_ref[...], preferred_element_type=F32)
    x1_ref[...] = x1
    h2 = _norm_mod(x1, g_ref[...], mod[3:4, :], mod[4:5, :])
    _write_token_major(h2_ref, (), h2)
    logits = _mm_nt(wr_ref[...], h2)
    e = jnp.exp(logits - jnp.max(logits, axis=0, keepdims=True))
    aff_ref[...] = e / jnp.sum(e, axis=0, keepdims=True)


def _post(x, merged, mod, g, wo, wr_t, n_per_batch):
    T, D = x.shape
    tm = min(512, T)
    bmap = _tile_batch_map(tm, n_per_batch, mod.shape[0])
    tile = pl.BlockSpec((tm, D), lambda i: (i, 0))
    return pl.pallas_call(
        _post_kernel,
        grid=(T // tm,),
        in_specs=[tile, tile, pl.BlockSpec((1, 6, D), bmap), _resident((1, D)), _resident(wo.shape),
                  _resident(wr_t.shape)],
        out_specs=(tile, _token_major_spec(tm, D), pl.BlockSpec((N_EXPERTS, tm), lambda i: (0, i)),
                   _token_major_spec(tm, D)),
        out_shape=(jax.ShapeDtypeStruct((T, D), F32), jax.ShapeDtypeStruct((T * _tok_rows(D), LANES), F32),
                   jax.ShapeDtypeStruct((N_EXPERTS, T), F32),
                   jax.ShapeDtypeStruct((T * _tok_rows(D), LANES), F32)),
        compiler_params=_cparams(("arbitrary",)),
        name="post",
    )(x, merged, mod, g.reshape(1, D), wo, wr_t)


def _prefix_counts(mask, upper, lower_strict):
    within = _mm(mask, upper)
    row_tot = jnp.broadcast_to(within[:, LANES - 1:LANES], within.shape)
    row_start = _mm(lower_strict, row_tot)
    return row_start + within - mask, row_start, within


def _split_int(x):
    high = jnp.floor(x * (1.0 / 256.0))
    return high, x - high * 256.0


def _select_kernel(aff_ref, idx_ref, gate_ref, *, cap, slot_tile):
    R = aff_ref.shape[1]
    upper = (_iota((LANES, LANES), 0) <= _iota((LANES, LANES), 1)).astype(MXU)
    lower_strict = (_iota((R, R), 1) < _iota((R, R), 0)).astype(MXU)
    lane0 = (_iota((8, LANES), 1) == 0).astype(MXU)
    lane_id = _iota((slot_tile, LANES), 1)
    row_id = _iota((slot_tile, R), 1)

    aff = aff_ref[0]
    bits = pltpu.bitcast(aff, I32)

    def bit_step(j, thr):
        cand = thr | jnp.left_shift(jnp.int32(1), 30 - j)
        cnt = jnp.sum((bits >= cand).astype(I32), axis=(0, 1), keepdims=True)
        return jnp.where(cnt >= cap, cand, thr)

    thr = lax.fori_loop(0, 31, bit_step, jnp.zeros((1, 1), I32))
    gt = (bits > thr).astype(F32)
    eq = (bits == thr).astype(F32)
    need = float(cap) - jnp.sum(gt, axis=(0, 1), keepdims=True)
    eq_rank, _, _ = _prefix_counts(eq, upper, lower_strict)
    sel = gt + eq * (eq_rank < need).astype(F32)
    _, row_start, within = _prefix_counts(sel, upper, lower_strict)
    row_end = row_start + jnp.broadcast_to(within[:, LANES - 1:LANES], within.shape)
    end_hi, end_lo = _split_int(row_end)
    row_end_t = (_mm_nt(lane0, end_hi) * 256.0 + _mm_nt(lane0, end_lo))[0:1, :]
    start_hi, start_lo = _split_int(row_start)
    aff_parts = _split3(aff)

    def tile(t, carry):
        base = t * slot_tile
        slot = (base + _iota((slot_tile, 1), 0)).astype(F32)
        row = jnp.sum((row_end_t <= slot).astype(F32), axis=1, keepdims=True)
        onehot = (row_id.astype(F32) == row).astype(MXU)
        start = _mm(onehot, start_hi) * 256.0 + _mm(onehot, start_lo)
        rank = slot - start
        counts = _mm(onehot, within)
        col = jnp.sum((counts <= rank).astype(F32), axis=1, keepdims=True)
        vals = sum(_mm(onehot, p) for p in aff_parts)
        gate = jnp.sum(jnp.where(lane_id.astype(F32) == col, vals, 0.0), axis=1, keepdims=True)
        token = (row * float(LANES) + col).astype(I32)
        out_rows = pl.ds(pl.multiple_of(base, slot_tile), slot_tile)
        idx_ref[0, out_rows, :] = jnp.broadcast_to(token, (slot_tile, LANES))
        gate_ref[0, out_rows, :] = jnp.broadcast_to(gate, (slot_tile, LANES))
        return carry

    lax.fori_loop(0, cap // slot_tile, tile, 0)


def _select(aff_t, cap):
    E, n = aff_t.shape
    assert n % LANES == 0
    R = n // LANES
    slot_tile = min(512, cap)
    shape = (E, cap, LANES)
    return pl.pallas_call(
        functools.partial(_select_kernel, cap=cap, slot_tile=slot_tile),
        grid=(E,),
        in_specs=[pl.BlockSpec((1, R, LANES), lambda e: (e, 0, 0))],
        out_specs=(pl.BlockSpec((1, cap, LANES), lambda e: (e, 0, 0)),
                   pl.BlockSpec((1, cap, LANES), lambda e: (e, 0, 0))),
        out_shape=(jax.ShapeDtypeStruct(shape, I32), jax.ShapeDtypeStruct(shape, F32)),
        compiler_params=_cparams(("arbitrary",)),
        name="select",
    )(aff_t.reshape(E, R, LANES))


MOE_TILE = 512
MOE_ISSUE_UNROLL = 16


def _moe_kernel(idx_ref, idx_next_ref, h_hbm, gate_ref, wg_ref, wu_ref, wd_ref, acc_in, acc_hbm, xbuf, obuf, sem,
                *, ts, per):
    del acc_in
    tok_rows = xbuf.shape[1] // ts
    t = pl.program_id(1)
    step = pl.program_id(0) * per + t
    n_steps = pl.num_programs(0) * per
    cur, nxt = lax.rem(step, 2), lax.rem(step + 1, 2)
    ring, ring_next, ring_prev = lax.rem(step, 3), lax.rem(step + 1, 3), lax.rem(step + 2, 3)
    first, last = step == 0, step == n_steps - 1
    expert_start, expert_end = t == 0, t == per - 1
    prev_was_expert_start = (t == 1) if per > 1 else True

    def issue_row(ids_ref, kind, buf, i):
        tok = pl.ds(pl.multiple_of(ids_ref[0, 0, i] * tok_rows, tok_rows), tok_rows)
        row = pl.ds(pl.multiple_of(i * tok_rows, tok_rows), tok_rows)
        if kind == 0:
            cp = pltpu.make_async_copy(h_hbm.at[tok, :], xbuf.at[buf, row, :], sem.at[0, buf])
        elif kind == 1:
            cp = pltpu.make_async_copy(acc_hbm.at[tok, :], obuf.at[buf, row, :], sem.at[1, buf])
        else:
            cp = pltpu.make_async_copy(obuf.at[buf, row, :], acc_hbm.at[tok, :], sem.at[2, buf])
        cp.start()

    def issue(ids_ref, kind, buf):
        def body(i, c):
            issue_row(ids_ref, kind, buf, i)
            return c
        lax.fori_loop(0, ts, body, 0, unroll=MOE_ISSUE_UNROLL)

    def wait_all(kind, buf):
        ref = xbuf if kind == 0 else obuf
        pltpu.make_async_copy(ref.at[buf], ref.at[buf], sem.at[kind, buf]).wait()

    @pl.when(first)
    def _():
        issue(idx_ref, 0, cur)

    @pl.when(jnp.logical_and(step >= 2, jnp.logical_not(prev_was_expert_start)))
    def _():
        wait_all(2, ring_next)

    @pl.when(expert_start)
    def _():
        @pl.when(jnp.logical_not(first))
        def _():
            wait_all(2, ring_prev)
        issue(idx_ref, 1, ring)

    wait_all(0, cur)
    wait_all(1, ring)

    @pl.when(jnp.logical_not(last))
    def _():
        issue(idx_next_ref, 0, nxt)

    @pl.when(jnp.logical_not(expert_end))
    def _():
        issue(idx_next_ref, 1, ring_next)

    x = _read_token_major(xbuf, (cur,), ts).astype(MXU)
    hidden = (_silu(jnp.dot(x, wg_ref[0], preferred_element_type=F32))
              * jnp.dot(x, wu_ref[0], preferred_element_type=F32))
    y = jnp.dot(hidden.astype(MXU), wd_ref[0], preferred_element_type=F32)
    g = gate_ref[0][:, :1]
    _write_token_major(obuf, (ring,), _read_token_major(obuf, (ring,), ts) + y * g)
    issue(idx_ref, 2, ring)

    @pl.when(last)
    def _():
        @pl.when(jnp.logical_and(jnp.logical_not(first), jnp.logical_not(expert_start)))
        def _():
            wait_all(2, ring_prev)
        wait_all(2, ring)


def _moe(h2, idx, gate, wg, wu, wd, acc0):
    D = wg.shape[1]
    tok_rows = _tok_rows(D)
    n = h2.shape[0] // tok_rows
    E, cap, _ = gate.shape
    FF = wg.shape[2]
    ts = min(MOE_TILE, cap)
    per = cap // ts
    n_steps = E * per
    idx_blocks = idx.reshape(n_steps, 1, ts)
    wspec = lambda shape: pl.BlockSpec((1,) + shape, lambda e, t: (e, 0, 0))
    ids = lambda shift: pl.BlockSpec((1, 1, ts), lambda e, t: (jnp.minimum(e * per + t + shift, n_steps - 1), 0, 0),
                                     memory_space=pltpu.SMEM)
    return pl.pallas_call(
        functools.partial(_moe_kernel, ts=ts, per=per),
        grid=(E, per),
        in_specs=[ids(0), ids(1),
                  pl.BlockSpec(memory_space=pl.ANY),
                  pl.BlockSpec((1, ts, LANES), lambda e, t: (e, t, 0)),
                  wspec((D, FF)), wspec((D, FF)), wspec((FF, D)),
                  pl.BlockSpec(memory_space=pl.ANY)],
        out_specs=pl.BlockSpec(memory_space=pl.ANY),
        out_shape=jax.ShapeDtypeStruct((n * tok_rows, LANES), F32),
        scratch_shapes=[pltpu.VMEM((2, ts * tok_rows, LANES), F32), pltpu.VMEM((3, ts * tok_rows, LANES), F32),
                        pltpu.SemaphoreType.DMA((3, 3))],
        input_output_aliases={7: 0},
        compiler_params=_cparams(("arbitrary", "arbitrary")),
        name="moe",
    )(idx_blocks, idx_blocks, h2, gate, wg, wu, wd, acc0)


def _final_kernel(xa_ref, xb_ref, mod_ref, g_ref, o_ref):
    x = xa_ref[...] + mod_ref[0, 5:6, :] * _read_token_major(xb_ref, (), xa_ref.shape[0])
    o_ref[...] = x * lax.rsqrt(jnp.mean(x * x, axis=-1, keepdims=True) + EPS) * g_ref[...]


def _final(xa, xb, mod, g, n_per_batch):
    T, D = xa.shape
    tm = min(512, T)
    tile = pl.BlockSpec((tm, D), lambda i: (i, 0))
    return pl.pallas_call(
        _final_kernel,
        grid=(T // tm,),
        in_specs=[tile, _token_major_spec(tm, D),
                  pl.BlockSpec((1, 6, D), _tile_batch_map(tm, n_per_batch, mod.shape[0])), _resident((1, D))],
        out_specs=tile,
        out_shape=jax.ShapeDtypeStruct((T, D), F32),
        compiler_params=_cparams(("arbitrary",)),
        name="final_norm",
    )(xa, xb, mod, g.reshape(1, D))


def _rope_tables(n):
    quarter = DK_B // 4
    t = jnp.arange(n)
    row = (t // GRID_W).astype(F32)
    col = (t % GRID_W).astype(F32)
    inv = ROPE_BASE ** (-jnp.arange(quarter, dtype=F32) / quarter)
    ang = jnp.concatenate([row[:, None] * inv, col[:, None] * inv], axis=-1)
    cos, sin = jnp.cos(ang), jnp.sin(ang)
    cos_t = jnp.concatenate([cos, cos] * (LANES // DK_B), axis=-1)
    sin_t = jnp.concatenate([-sin, sin] * (LANES // DK_B), axis=-1)
    return cos_t, sin_t


def _prep_layer(l, w_in, gla_w_gf, gla_b_gf, gla_w_gb, gla_b_gb, ret_ld_f, ret_ld_b):
    D = w_in.shape[1]
    w = w_in[l]
    s = np.cumsum([0, QK_A, QK_A, V_A, V_A, GATE_RANK, GATE_RANK, QK_B, QK_B, V_B, V_B, W_C, W_C, W_C])
    a_q, a_k, a_v, a_r, a_zf, a_zb, b_q, b_k, b_v, b_g, c_q, c_k, c_v = [w[:, s[i]:s[i + 1]] for i in range(13)]
    zpad = jnp.zeros((D, LANES - 2 * GATE_RANK), w.dtype)
    w1a = jnp.concatenate([a_q, a_k, a_r, b_q, b_k, b_g, a_zf, a_zb, zpad], axis=1).astype(MXU)
    w1b = jnp.concatenate([a_v, b_v, c_q, c_k, c_v], axis=1).astype(MXU)
    wm = w[:, MIX_W:].astype(MXU)
    wf = jnp.zeros((LANES, QK_A), F32).at[:GATE_RANK].set(gla_w_gf[l])
    wb = jnp.zeros((LANES, QK_A), F32).at[GATE_RANK:2 * GATE_RANK].set(gla_w_gb[l])
    ld = lambda v: jnp.broadcast_to(v[l].reshape(H_B // 2, 2, 1), (H_B // 2, 2, 2 * LANES))
    return dict(w1a=w1a, w1b=w1b, wm=wm, wf=wf, bf=gla_b_gf[l].reshape(1, QK_A), wb=wb, bb=gla_b_gb[l].reshape(1, QK_A),
                ldf=ld(ret_ld_f), ldb=ld(ret_ld_b))


def kernel(x_prompt, x_sample, cache_nat_k, cache_nat_v, state_gla_fwd, state_gla_bwd, state_ret_fwd,
           state_ret_bwd, c, c_ctx, norm1, norm2, w_ada, b_ada, w_in, gla_w_gf, gla_b_gf, gla_w_gb, gla_b_gb,
           gla_gn, ret_log_decay_f, ret_log_decay_b, ret_gn, nat_rpb, w_br_a, w_br_b, w_br_c, w_out, w_router,
           w_gate, w_up, w_down, final_norm):
    B, n_ctx, D = x_prompt.shape
    Bd, n_lat, _ = x_sample.shape
    L = w_in.shape[0]
    past = cache_nat_k.shape[2]

    cond = jnp.concatenate([c_ctx[None, :], c], axis=0)
    n_cond = cond.shape[0]
    cond = jnp.pad(cond, ((0, -n_cond % 8), (0, 0)))
    mod_all = _ada(cond, w_ada, b_ada).reshape(L, cond.shape[0], 6, D)
    rope_tabs = _rope_tables(n_lat)
    zero_a = jnp.zeros((B, H_A, DK_A, DV_A), F32)
    zero_b = jnp.zeros((B, H_B, DK_B, DV_B), F32)

    paths = {
        "ctx": dict(x=x_prompt.reshape(B * n_ctx, D), moe=None, B=B, n=n_ctx),
        "lat": dict(x=x_sample.reshape(Bd * n_lat, D), moe=None, B=Bd, n=n_lat),
    }
    outs = dict(nk=[], nv=[], gf=[], gb=[], rf=[], rb=[])
    mod_prev = {}
    for l in range(L):
        w = _prep_layer(l, w_in, gla_w_gf, gla_b_gf, gla_w_gb, gla_b_gb, ret_log_decay_f, ret_log_decay_b)
        wa, wb_, wc = w_br_a[l].astype(MXU), w_br_b[l].astype(MXU), w_br_c[l].astype(MXU)
        wo = w_out[l].astype(MXU)
        wr_t = w_router[l].T.astype(MXU)
        wg, wu, wd = w_gate[l].astype(MXU), w_up[l].astype(MXU), w_down[l].astype(MXU)
        gn_a, gn_b = gla_gn[l].reshape(1, DV_A), ret_gn[l].reshape(1, DV_B)
        bias = _window_bias(nat_rpb[l].astype(F32), n_lat // GRID_W)
        mods = {"ctx": mod_all[l, 0:1], "lat": mod_all[l, 1:n_cond]}
        for name, st in paths.items():
            Bp, n = st["B"], st["n"]
            latent = name == "lat"
            mod = mods[name]
            Pa, Pb, x, nat_k, nat_v = _in_proj(st["x"], st["moe"], mod_prev.get(name), mod, norm1[l],
                                               w["w1a"], w["w1b"], n, emit_kv=not latent)
            if latent:
                sa_f0, sa_b0 = state_gla_fwd[:, l].astype(F32), state_gla_bwd[:, l].astype(F32)
                sb_f0, sb_b0 = state_ret_fwd[:, l].astype(F32), state_ret_bwd[:, l].astype(F32)
            else:
                sa_f0 = sa_b0 = zero_a
                sb_f0 = sb_b0 = zero_b
            o_a, sa_f, sa_b = _gla(Pa, Pb, Bp, n, w["wf"], w["bf"], w["wb"], w["bb"], gn_a, sa_f0, sa_b0)
            o_b, sb_f, sb_b = _ret(Pa, Pb, Bp, n, w["ldf"], w["ldb"], gn_b, sb_f0, sb_b0,
                                   rope_tabs if latent else None)
            if latent:
                k_ctx = cache_nat_k[:, l].astype(F32).reshape(Bp, past, W_C)
                v_ctx = cache_nat_v[:, l].astype(F32).reshape(Bp, past, W_C)
                o_c = _attn_lat(Pb, Bp, n, k_ctx, v_ctx, bias)
            else:
                o_c = _attn_ctx(Pb, Bp, n)
                outs["nk"].append(nat_k.reshape(Bp, n, H_C, D_C))
                outs["nv"].append(nat_v.reshape(Bp, n, H_C, D_C))
                outs["gf"].append(sa_f)
                outs["gb"].append(sa_b)
                outs["rf"].append(sb_f)
                outs["rb"].append(sb_b)
            merged = _merge(x, mod, norm1[l], o_a, o_b, o_c, w["wm"], wa, wb_, wc, n)
            x1, h2, aff_t, acc0 = _post(x, merged, mod, norm2[l], wo, wr_t, n)
            cap = EC_FACTOR * (Bp * n) // N_EXPERTS
            idx, gate = _select(aff_t, cap)
            st["moe"] = _moe(h2, idx[:, :, 0], gate, wg, wu, wd, acc0)
            st["x"] = x1
            mod_prev[name] = mod
    y = {name: _final(st["x"], st["moe"], mod_prev[name], final_norm, st["n"]) for name, st in paths.items()}
    stack = lambda xs: jnp.stack(xs, axis=1)
    return (y["ctx"].reshape(B, n_ctx, D), y["lat"].reshape(Bd, n_lat, D), stack(outs["nk"]), stack(outs["nv"]),
            stack(outs["gf"]), stack(outs["gb"]), stack(outs["rf"]), stack(outs["rb"]))
```

```python
import functools

import numpy as np
import jax
import jax.numpy as jnp
from jax import lax
from jax.experimental import pallas as pl
from jax.experimental.pallas import tpu as pltpu

F32 = jnp.float32
I32 = jnp.int32
MXU = jnp.bfloat16

GRID_W = 64
H_A, DK_A, DV_A = 4, 64, 128
GATE_RANK, GATE_TEMP = 16, 16.0
H_B, DK_B, DV_B = 4, 64, 128
H_C, D_C = 8, 64
WIN_H, WIN_W = 8, 16
CHUNK = 64
N_EXPERTS, EC_FACTOR = 16, 2
ROPE_BASE = 10000.0
EPS = 1e-6
NEG_INF = -1e30

QK_A, V_A = H_A * DK_A, H_A * DV_A
QK_B, V_B = H_B * DK_B, H_B * DV_B
W_C = H_C * D_C
C_AQ, C_AK, C_AR = 0, 256, 512
C_BQ, C_BK, C_BG, C_Z = 1024, 1280, 1536, 2048
C_AV, C_BV, C_CQ, C_CK, C_CV = 0, 512, 1024, 1536, 2048
MIX_W = 4640

LANES = 128
SUPER = 256
NCH = SUPER // CHUNK
Q_ROWS = 4
K_ROWS = Q_ROWS + WIN_H
VMEM_LIMIT = 56 * 1024 * 1024


def _cparams(sem):
    return pltpu.CompilerParams(dimension_semantics=sem, vmem_limit_bytes=VMEM_LIMIT)


def _mm(a, b):
    return jnp.dot(a.astype(MXU), b.astype(MXU), preferred_element_type=F32)


def _mm_nt(a, b):
    return lax.dot_general(a.astype(MXU), b.astype(MXU), (((1,), (1,)), ((), ())), preferred_element_type=F32)


def _iota(shape, dim):
    return lax.broadcasted_iota(I32, shape, dim)


def _sigmoid(x):
    return 1.0 / (1.0 + jnp.exp(-x))


def _silu(x):
    return x * _sigmoid(x)


def _resident(shape):
    nd = len(shape)
    return pl.BlockSpec(shape, lambda *_: (0,) * nd, pipeline_mode=pl.Buffered(1))


def _ada_kernel(cond_ref, w_ref, b_ref, o_ref):
    o_ref[0] = _mm(_silu(cond_ref[...]), w_ref[0]) + b_ref[0]


def _ada(cond, w_ada, b_ada):
    L, D, D6 = w_ada.shape
    R = cond.shape[0]
    tn = 1024 if D6 % 1024 == 0 else D
    assert D6 % tn == 0
    return pl.pallas_call(
        _ada_kernel,
        grid=(L, D6 // tn),
        in_specs=[pl.BlockSpec((R, D), lambda l, j: (0, 0)),
                  pl.BlockSpec((1, D, tn), lambda l, j: (l, 0, j)),
                  pl.BlockSpec((1, 1, tn), lambda l, j: (l, 0, j))],
        out_specs=pl.BlockSpec((1, R, tn), lambda l, j: (l, 0, j)),
        out_shape=jax.ShapeDtypeStruct((L, R, D6), F32),
        compiler_params=_cparams(("arbitrary", "arbitrary")),
        name="ada",
    )(cond, w_ada, b_ada.reshape(L, 1, D6))


def _norm_mod(x, g, shift, scale):
    y = x * lax.rsqrt(jnp.mean(x * x, axis=-1, keepdims=True) + EPS) * g
    return y * (1.0 + scale) + shift


def _tile_batch_map(tm, n_per_batch, nb):
    if nb == 1:
        return lambda i: (0, 0, 0)
    assert n_per_batch % tm == 0
    per = n_per_batch // tm
    return lambda i: (i // per, 0, 0)


def _col_chunks(width, step):
    return [(s, min(step, width - s)) for s in range(0, width, step)]


def _tok_rows(D):
    assert D % LANES == 0
    return D // LANES


def _read_token_major(ref, lead, n_tok):
    r = ref.shape[-2] // n_tok
    parts = [ref[lead + (pl.ds(j, n_tok, stride=r), slice(None))] for j in range(r)]
    return jnp.concatenate(parts, axis=1)


def _write_token_major(ref, lead, value):
    n_tok = value.shape[0]
    r = ref.shape[-2] // n_tok
    for j in range(r):
        ref[lead + (pl.ds(j, n_tok, stride=r), slice(None))] = value[:, j * LANES:(j + 1) * LANES]


def _token_major_spec(tm, D):
    return pl.BlockSpec((tm * _tok_rows(D), LANES), lambda i: (i, 0))


def _in_kernel(*refs, has_res, emit_kv):
    refs = list(refs)
    xa_ref = refs.pop(0)
    x = xa_ref[...]
    if has_res:
        xb_ref, modp_ref = refs.pop(0), refs.pop(0)
        x = x + modp_ref[0, 5:6, :] * _read_token_major(xb_ref, (), xa_ref.shape[0])
    mod_ref, g_ref, wa_ref, wb_ref, pa_ref, pb_ref = refs[:6]
    outs = refs[6:]
    if has_res:
        outs.pop(0)[...] = x
    h = _norm_mod(x, g_ref[...], mod_ref[0, 0:1, :], mod_ref[0, 1:2, :]).astype(MXU)
    for s, w in _col_chunks(wa_ref.shape[1], 512):
        pa_ref[:, s:s + w] = jnp.dot(h, wa_ref[:, s:s + w], preferred_element_type=F32)
    for s, w in _col_chunks(wb_ref.shape[1], 512):
        part = jnp.dot(h, wb_ref[:, s:s + w], preferred_element_type=F32)
        pb_ref[:, s:s + w] = part.astype(pb_ref.dtype)
        if emit_kv:
            if s == C_CK:
                outs[0][...] = part
            if s == C_CV:
                outs[1][...] = part


def _in_proj(xa, xb, mod_prev, mod, g, w1a, w1b, n_per_batch, emit_kv):
    T, D = xa.shape
    nb = mod.shape[0]
    tm = min(256, T)
    bmap = _tile_batch_map(tm, n_per_batch, nb)
    tile = lambda w: pl.BlockSpec((tm, w), lambda i: (i, 0))
    mspec = pl.BlockSpec((1, 6, D), bmap)
    has_res = xb is not None
    assert W_C == 512 and C_CK % 512 == 0 and C_CV % 512 == 0
    ins = [xa] + ([xb, mod_prev] if has_res else []) + [mod, g.reshape(1, D), w1a, w1b]
    in_specs = ([tile(D)] + ([_token_major_spec(tm, D), mspec] if has_res else [])
                + [mspec, _resident((1, D)), _resident(w1a.shape), _resident(w1b.shape)])
    outs = ([(w1a.shape[1], F32), (w1b.shape[1], MXU)] + ([(D, F32)] if has_res else [])
            + ([(W_C, F32), (W_C, F32)] if emit_kv else []))
    out = list(pl.pallas_call(
        functools.partial(_in_kernel, has_res=has_res, emit_kv=emit_kv),
        grid=(T // tm,),
        in_specs=in_specs,
        out_specs=tuple(tile(w) for w, _ in outs),
        out_shape=tuple(jax.ShapeDtypeStruct((T, w), dt) for w, dt in outs),
        compiler_params=_cparams(("arbitrary",)),
        name="in_proj",
    )(*ins))
    Pa, Pb = out.pop(0), out.pop(0)
    x = out.pop(0) if has_res else xa
    k, v = out if emit_kv else (None, None)
    return Pa, Pb, x, k, v


def _split3(x):
    p1 = x.astype(MXU)
    r1 = x - p1.astype(F32)
    p2 = r1.astype(MXU)
    p3 = (r1 - p2.astype(F32)).astype(MXU)
    return p1, p2, p3


def _chunk_masks():
    row, col = _iota((SUPER, SUPER), 0), _iota((SUPER, SUPER), 1)
    same = (row // CHUNK) == (col // CHUNK)
    return same, same & (col <= row), same & (col >= row)


def _head_replicate(h):
    return (_iota((LANES, SUPER), 0) == (_iota((LANES, SUPER), 1) % CHUNK) + DK_A * h).astype(MXU)


def _recur_step(q_in, kdec_t, att, v, state, dec_fn, same, fwd, rep):
    o_intra = _mm(att, v)
    kblk = jnp.where(same, jnp.concatenate([kdec_t] * NCH, axis=0), 0.0)
    upd = _mm(kblk, v)
    order = range(NCH) if fwd else range(NCH - 1, -1, -1)
    prev = [None] * NCH
    for c in order:
        prev[c] = state
        state = dec_fn(c) * state + upd[c * CHUNK:(c + 1) * CHUNK]
    q_blk = jnp.where(same, _mm(q_in, rep), 0.0)
    o_inter = _mm(q_blk, jnp.concatenate(prev, axis=0))
    return o_intra + o_inter, state


def _head_norm_gate(o, g, r):
    return o * lax.rsqrt(jnp.mean(o * o, axis=-1, keepdims=True) + EPS) * g * _silu(r)


def _gla_kernel(q_ref, k_ref, v_ref, r_ref, z_ref, wf_ref, bf_ref, wb_ref, bb_ref, gn_ref, s0f_ref, s0b_ref,
                o_ref, sf_ref, sb_ref, acc_ref, *, n_sc):
    same, tri_f, tri_b = _chunk_masks()
    reps = [_head_replicate(h) for h in range(2)]
    lane_head = _iota((1, LANES), 1) // DK_A

    def run(sc, states, fwd):
        off = pl.multiple_of(sc * SUPER, SUPER)
        rows = pl.ds(off, SUPER)
        q = q_ref[rows, :] * (DK_A ** -0.5)
        k = k_ref[rows, :]
        x = _mm(z_ref[rows, :], (wf_ref if fwd else wb_ref)[...]) + (bf_ref if fwd else bb_ref)[...]
        la = (jnp.minimum(x, 0.0) - jnp.log1p(jnp.exp(-jnp.abs(x)))) * (1.0 / GATE_TEMP)
        tri = tri_f if fwd else tri_b
        hi = la.astype(MXU)
        lo = la - hi.astype(F32)
        trim = tri.astype(MXU)
        b = _mm(trim, hi) + _mm(trim, lo)
        last = CHUNK - 1 if fwd else 0
        tot_rows = [b[c * CHUNK + last:c * CHUNK + last + 1, :] for c in range(NCH)]
        b_last = jnp.concatenate([jnp.broadcast_to(t, (CHUNK, LANES)) for t in tot_rows], axis=0)
        tot_t = jnp.transpose(jnp.concatenate(tot_rows + [jnp.zeros((LANES - NCH, LANES), F32)], axis=0))
        q_in = q * jnp.exp(b)
        k_out = k * jnp.exp(-b)
        kdec_t = jnp.transpose(k * jnp.exp(b_last - b))
        outs, new_states = [], []
        for h in range(2):
            head_rows = slice(h * DK_A, (h + 1) * DK_A)
            dec_fn = lambda c, t=tot_t[head_rows]: jnp.exp(jnp.broadcast_to(t[:, c:c + 1], (DK_A, DV_A)))
            qh = jnp.where(lane_head == h, q_in, 0.0)
            att = jnp.where(tri, _mm_nt(qh, k_out), 0.0)
            v = v_ref[rows, h * DV_A:(h + 1) * DV_A]
            o, s = _recur_step(qh, kdec_t[head_rows], att, v, states[h], dec_fn, same, fwd, reps[h])
            outs.append(o)
            new_states.append(s)
        return rows, outs, tuple(new_states)

    sf, sb = _both_directions(run, n_sc, acc_ref, o_ref, gn_ref, r_ref,
                              (s0f_ref[0, 0], s0f_ref[0, 1]), (s0b_ref[0, 0], s0b_ref[0, 1]))
    for h in range(2):
        sf_ref[0, h] = sf[h]
        sb_ref[0, h] = sb[h]


def _both_directions(run, n_sc, acc_ref, o_ref, gn_ref, gate_ref, sf, sb, unroll=1):
    dv = acc_ref.shape[1] // 2

    def park(rows, outs):
        for h in range(2):
            acc_ref[rows, h * dv:(h + 1) * dv] = outs[h]

    def finish(rows, outs, other=None):
        for h in range(2):
            cols = slice(h * dv, (h + 1) * dv)
            total = outs[h] + (acc_ref[rows, cols] if other is None else other[h])
            o_ref[rows, cols] = _head_norm_gate(total, gn_ref[...], gate_ref[rows, cols]).astype(o_ref.dtype)

    if n_sc == 1:
        rows, outs_f, sf = run(0, sf, True)
        _, outs_b, sb = run(0, sb, False)
        finish(rows, outs_f, outs_b)
        return sf, sb
    assert n_sc % 2 == 0

    def body(second):
        def step(i, carry):
            sf, sb = carry
            rows_f, outs_f, sf = run(i, sf, True)
            rows_b, outs_b, sb = run(n_sc - 1 - i, sb, False)
            (finish if second else park)(rows_f, outs_f)
            (finish if second else park)(rows_b, outs_b)
            return sf, sb
        return step

    carry = lax.fori_loop(0, n_sc // 2, body(False), (sf, sb), unroll=unroll)
    return lax.fori_loop(n_sc // 2, n_sc, body(True), carry, unroll=unroll)


def _pair_specs(n, col_q, col_k, col_v, col_r):
    return [pl.BlockSpec((n, LANES), lambda b, p: (b, col_q // LANES + p)),
            pl.BlockSpec((n, LANES), lambda b, p: (b, col_k // LANES + p)),
            pl.BlockSpec((n, 2 * DV_A), lambda b, p: (b, col_v // (2 * DV_A) + p)),
            pl.BlockSpec((n, 2 * DV_A), lambda b, p: (b, col_r // (2 * DV_A) + p))]


def _state_spec():
    return pl.BlockSpec((1, 2, DK_A, DV_A), lambda b, p: (b, p, 0, 0))


def _gla(Pa, Pb, B, n, wf, bf, wb, bb, gn, s0f, s0b):
    T = B * n
    assert n % SUPER == 0
    pair_w = pl.BlockSpec((LANES, LANES), lambda b, p: (0, p))
    pair_b = pl.BlockSpec((1, LANES), lambda b, p: (0, p))
    st_shape = jax.ShapeDtypeStruct((B, H_A, DK_A, DV_A), F32)
    return pl.pallas_call(
        functools.partial(_gla_kernel, n_sc=n // SUPER),
        grid=(B, H_A // 2),
        in_specs=_pair_specs(n, C_AQ, C_AK, C_AV, C_AR)
        + [pl.BlockSpec((n, LANES), lambda b, p: (b, C_Z // LANES)),
           pair_w, pair_b, pair_w, pair_b, _resident((1, DV_A)), _state_spec(), _state_spec()],
        out_specs=(pl.BlockSpec((n, 2 * DV_A), lambda b, p: (b, p)), _state_spec(), _state_spec()),
        out_shape=(jax.ShapeDtypeStruct((T, V_A), MXU), st_shape, st_shape),
        scratch_shapes=[pltpu.VMEM((n, 2 * DV_A), F32)],
        compiler_params=_cparams(("arbitrary", "arbitrary")),
        name="gla",
    )(Pa, Pa, Pb, Pa, Pa, wf, bf, wb, bb, gn, s0f, s0b)


def _ret_kernel(*refs, n_sc, rope):
    if rope:
        (q_ref, k_ref, v_ref, g_ref, ldf_ref, ldb_ref, gn_ref, s0f_ref, s0b_ref, cos_ref, sin_ref,
         o_ref, sf_ref, sb_ref, acc_ref) = refs
    else:
        (q_ref, k_ref, v_ref, g_ref, ldf_ref, ldb_ref, gn_ref, s0f_ref, s0b_ref,
         o_ref, sf_ref, sb_ref, acc_ref) = refs
    same, tri_f, tri_b = _chunk_masks()
    reps = [_head_replicate(h) for h in range(2)]
    lane_head = _iota((1, LANES), 1) // DK_B
    half = DK_B // 2
    first_half = (_iota((1, LANES), 1) % DK_B) < half
    diff = jnp.abs(_iota((SUPER, SUPER), 0) - _iota((SUPER, SUPER), 1)).astype(F32)
    pos = (_iota((SUPER, LANES), 0) % CHUNK).astype(F32)
    decays = {fwd: [jnp.where(tri_f if fwd else tri_b, jnp.exp(ld_ref[0, h:h + 1, :] * diff), 0.0)
                    for h in range(2)]
              for fwd, ld_ref in ((True, ldf_ref), (False, ldb_ref))}

    def rotate(x, rows):
        swapped = jnp.where(first_half, pltpu.roll(x, LANES - half, 1), pltpu.roll(x, half, 1))
        return x * cos_ref[rows, :] + swapped * sin_ref[rows, :]

    def run(sc, states, fwd):
        off = pl.multiple_of(sc * SUPER, SUPER)
        rows = pl.ds(off, SUPER)
        q = q_ref[rows, :]
        k = k_ref[rows, :] * (DK_B ** -0.5)
        if rope:
            q, k = rotate(q, rows), rotate(k, rows)
        ld = (ldf_ref if fwd else ldb_ref)[0]
        lg_lane = jnp.where(lane_head == 0, ld[0:1, :LANES], ld[1:2, :LANES])
        if fwd:
            q_in = q * jnp.exp(lg_lane * (pos + 1.0))
            k_dec = k * jnp.exp(lg_lane * (CHUNK - 1.0 - pos))
        else:
            q_in = q * jnp.exp(lg_lane * (CHUNK - pos))
            k_dec = k * jnp.exp(lg_lane * pos)
        kdec_t = jnp.transpose(k_dec)
        outs, new_states = [], []
        for h in range(2):
            qh = jnp.where(lane_head == h, q, 0.0)
            att = _mm_nt(qh, k) * decays[fwd][h]
            chunk_dec = jnp.exp(ld[h:h + 1, :DV_B] * float(CHUNK))
            v = v_ref[rows, h * DV_B:(h + 1) * DV_B]
            o, s = _recur_step(q_in, kdec_t[h * DK_B:(h + 1) * DK_B], att, v, states[h], lambda c: chunk_dec,
                               same, fwd, reps[h])
            outs.append(o)
            new_states.append(s)
        return rows, outs, tuple(new_states)

    sf, sb = _both_directions(run, n_sc, acc_ref, o_ref, gn_ref, g_ref,
                              (s0f_ref[0, 0], s0f_ref[0, 1]), (s0b_ref[0, 0], s0b_ref[0, 1]), unroll=2)
    for h in range(2):
        sf_ref[0, h] = sf[h]
        sb_ref[0, h] = sb[h]


def _ret(Pa, Pb, B, n, ldf, ldb, gn, s0f, s0b, rope_tabs):
    T = B * n
    rope = rope_tabs is not None
    ld_spec = pl.BlockSpec((1, 2, 2 * LANES), lambda b, p: (p, 0, 0))
    st_shape = jax.ShapeDtypeStruct((B, H_B, DK_B, DV_B), F32)
    ins = [Pa, Pa, Pb, Pa, ldf, ldb, gn, s0f, s0b] + (list(rope_tabs) if rope else [])
    return pl.pallas_call(
        functools.partial(_ret_kernel, n_sc=n // SUPER, rope=rope),
        grid=(B, H_B // 2),
        in_specs=_pair_specs(n, C_BQ, C_BK, C_BV, C_BG)
        + [ld_spec, ld_spec, _resident((1, DV_B)), _state_spec(), _state_spec()]
        + ([_resident((n, LANES))] * 2 if rope else []),
        out_specs=(pl.BlockSpec((n, 2 * DV_B), lambda b, p: (b, p)), _state_spec(), _state_spec()),
        out_shape=(jax.ShapeDtypeStruct((T, V_B), MXU), st_shape, st_shape),
        scratch_shapes=[pltpu.VMEM((n, 2 * DV_B), F32)],
        compiler_params=_cparams(("arbitrary", "arbitrary")),
        name="retention",
    )(*ins)


def _attn_ctx_kernel(q_ref, k_ref, v_ref, o_ref):
    lane_head = _iota((1, LANES), 1) // D_C
    q = q_ref[...] * jnp.asarray(D_C ** -0.5, q_ref.dtype)
    k = k_ref[...]
    v = v_ref[...]
    out = jnp.zeros(q.shape, F32)
    for h in range(2):
        s = _mm_nt(jnp.where(lane_head == h, q, jnp.zeros_like(q)), k)
        e = jnp.exp(s - jnp.max(s, axis=-1, keepdims=True))
        p = e / jnp.sum(e, axis=-1, keepdims=True)
        out = jnp.where(lane_head == h, _mm(p, v), out)
    o_ref[...] = out.astype(o_ref.dtype)


def _attn_ctx(P, B, n):
    spec = lambda col: pl.BlockSpec((n, LANES), lambda b, p: (b, col // LANES + p))
    return pl.pallas_call(
        _attn_ctx_kernel,
        grid=(B, H_C // 2),
        in_specs=[spec(C_CQ), spec(C_CK), spec(C_CV)],
        out_specs=pl.BlockSpec((n, LANES), lambda b, p: (b, p)),
        out_shape=jax.ShapeDtypeStruct((B * n, W_C), MXU),
        compiler_params=_cparams(("arbitrary", "arbitrary")),
        name="attn_ctx",
    )(P, P, P)


def _attn_lat_kernel(q_ref, k_ref, v_ref, kc_ref, vc_ref, bias_ref, o_ref, *, n_blk, key_start_max):
    lane_head = _iota((1, LANES), 1) // D_C
    nq, nk = Q_ROWS * GRID_W, K_ROWS * GRID_W
    kc = kc_ref[0].astype(MXU)
    vc = vc_ref[0].astype(MXU)

    def body(i, carry):
        qrows = pl.ds(pl.multiple_of(i * nq, nq), nq)
        kstart = jnp.clip(i * Q_ROWS - WIN_H // 2, 0, key_start_max)
        krows = pl.ds(pl.multiple_of(kstart * GRID_W, GRID_W), nk)
        variant = jnp.where(i == 0, 0, jnp.where(i == n_blk - 1, 2, 1))
        q = q_ref[qrows, :] * jnp.asarray(D_C ** -0.5, q_ref.dtype)
        k = k_ref[krows, :]
        v = v_ref[krows, :]
        out = jnp.zeros((nq, LANES), F32)
        for h in range(2):
            qh = jnp.where(lane_head == h, q, jnp.zeros_like(q))
            s_win = _mm_nt(qh, k) + bias_ref[h, variant]
            s_ctx = _mm_nt(qh, kc)
            m = jnp.maximum(jnp.max(s_win, axis=-1, keepdims=True), jnp.max(s_ctx, axis=-1, keepdims=True))
            e_win = jnp.exp(s_win - m)
            e_ctx = jnp.exp(s_ctx - m)
            inv = 1.0 / (jnp.sum(e_win, axis=-1, keepdims=True) + jnp.sum(e_ctx, axis=-1, keepdims=True))
            o = (_mm(e_win, v) + _mm(e_ctx, vc)) * inv
            out = jnp.where(lane_head == h, o, out)
        o_ref[qrows, :] = out.astype(o_ref.dtype)
        return carry

    lax.fori_loop(0, n_blk, body, 0, unroll=2)


def _window_bias(rpb, rows):
    n_blk = rows // Q_ROWS
    kh = min(WIN_H, rows)
    qc = np.arange(GRID_W)
    win_c = np.clip(qc - WIN_W // 2, 0, GRID_W - WIN_W)
    kc = np.arange(GRID_W)
    col_ok = (kc[None, :] >= win_c[:, None]) & (kc[None, :] < win_c[:, None] + WIN_W)
    coff = np.clip(kc[None, :] - qc[:, None], -(WIN_W - 1), WIN_W - 1) + WIN_W - 1
    col_pick = (coff[:, :, None] == np.arange(2 * WIN_W - 1)).astype(np.float32)
    exact = lax.Precision.HIGHEST
    tabs = []
    for blk in (0, min(1, n_blk - 1), n_blk - 1):
        r = blk * Q_ROWS + np.arange(Q_ROWS)
        kstart = int(np.clip(blk * Q_ROWS - WIN_H // 2, 0, rows - K_ROWS))
        kr = kstart + np.arange(K_ROWS)
        r0 = np.clip(r - WIN_H // 2, 0, rows - kh)
        row_ok = (kr[None, :] >= r0[:, None]) & (kr[None, :] < r0[:, None] + kh)
        roff = np.clip(kr[None, :] - r[:, None] + WIN_H - 1, 0, 2 * WIN_H - 2)
        ok = row_ok[:, None, :, None] & col_ok[None, :, None, :]
        row_pick = (roff[:, :, None] == np.arange(2 * WIN_H - 1)).astype(np.float32)
        by_row = jnp.einsum('qka,hab->hqkb', row_pick, rpb, precision=exact)
        bias = jnp.einsum('hqkb,cdb->hqckd', by_row, col_pick, precision=exact)
        tab = jnp.where(ok[None], bias, NEG_INF)
        tabs.append(tab.reshape(rpb.shape[0], Q_ROWS * GRID_W, K_ROWS * GRID_W))
    return jnp.stack(tabs, axis=1)


def _attn_lat(P, B, n, k_ctx, v_ctx, bias):
    rows = n // GRID_W
    assert rows % Q_ROWS == 0 and rows >= K_ROWS
    L_ctx = k_ctx.shape[1]
    spec = lambda col: pl.BlockSpec((n, LANES), lambda p, b: (b, col // LANES + p))
    cspec = pl.BlockSpec((1, L_ctx, LANES), lambda p, b: (b, 0, p))
    nq, nk = Q_ROWS * GRID_W, K_ROWS * GRID_W
    return pl.pallas_call(
        functools.partial(_attn_lat_kernel, n_blk=rows // Q_ROWS, key_start_max=rows - K_ROWS),
        grid=(H_C // 2, B),
        in_specs=[spec(C_CQ), spec(C_CK), spec(C_CV), cspec, cspec,
                  pl.BlockSpec((2, 3, nq, nk), lambda p, b: (p, 0, 0, 0))],
        out_specs=pl.BlockSpec((n, LANES), lambda p, b: (b, p)),
        out_shape=jax.ShapeDtypeStruct((B * n, W_C), MXU),
        compiler_params=_cparams(("arbitrary", "arbitrary")),
        name="attn_lat",
    )(P, P, P, k_ctx, v_ctx, bias)


def _merge_kernel(x_ref, mod_ref, g_ref, oa_ref, ob_ref, oc_ref, wm_ref, wa_ref, wb_ref, wc_ref, m_ref):
    D = x_ref.shape[1]
    h = _norm_mod(x_ref[...], g_ref[...], mod_ref[0, 0:1, :], mod_ref[0, 1:2, :]).astype(MXU)
    for s, w in _col_chunks(D, 512):
        acc = None
        for i, (o_ref, wbr_ref) in enumerate(((oa_ref, wa_ref), (ob_ref, wb_ref), (oc_ref, wc_ref))):
            gate = _sigmoid(jnp.dot(h, wm_ref[:, i * D + s:i * D + s + w], preferred_element_type=F32))
            term = gate * jnp.dot(o_ref[...], wbr_ref[:, s:s + w], preferred_element_type=F32)
            acc = term if acc is None else acc + term
        m_ref[:, s:s + w] = acc.astype(m_ref.dtype)


def _merge(x, mod, g, oa, ob, oc, wm, wa, wb, wc, n_per_batch):
    T, D = x.shape
    tm = min(256, T)
    bmap = _tile_batch_map(tm, n_per_batch, mod.shape[0])
    tile = lambda w: pl.BlockSpec((tm, w), lambda i: (i, 0))
    return pl.pallas_call(
        _merge_kernel,
        grid=(T // tm,),
        in_specs=[tile(D), pl.BlockSpec((1, 6, D), bmap), _resident((1, D)), tile(V_A), tile(V_B), tile(W_C),
                  _resident(wm.shape), _resident(wa.shape), _resident(wb.shape), _resident(wc.shape)],
        out_specs=tile(D),
        out_shape=jax.ShapeDtypeStruct((T, D), MXU),
        compiler_params=_cparams(("arbitrary",)),
        name="merge",
    )(x, mod, g.reshape(1, D), oa, ob, oc, wm, wa, wb, wc)


def _post_kernel(x_ref, m_ref, mod_ref, g_ref, wo_ref, wr_ref, x1_ref, h2_ref, aff_ref, acc0_ref):
    acc0_ref[...] = jnp.zeros(acc0_ref.shape, acc0_ref.dtype)
    mod = mod_ref[0]
    x1 = x_ref[...] + mod[2:3, :] * jnp.dot(m_ref[...], wo_ref[...], preferred_element_type=F32)
    x1_ref[...] = x1
    h2 = _norm_mod(x1, g_ref[...], mod[3:4, :], mod[4:5, :])
    _write_token_major(h2_ref, (), h2)
    logits = _mm_nt(wr_ref[...], h2)
    e = jnp.exp(logits - jnp.max(logits, axis=0, keepdims=True))
    aff_ref[...] = e / jnp.sum(e, axis=0, keepdims=True)


def _post(x, merged, mod, g, wo, wr_t, n_per_batch):
    T, D = x.shape
    tm = min(512, T)
    bmap = _tile_batch_map(tm, n_per_batch, mod.shape[0])
    tile = pl.BlockSpec((tm, D), lambda i: (i, 0))
    return pl.pallas_call(
        _post_kernel,
        grid=(T // tm,),
        in_specs=[tile, tile, pl.BlockSpec((1, 6, D), bmap), _resident((1, D)), _resident(wo.shape),
                  _resident(wr_t.shape)],
        out_specs=(tile, _token_major_spec(tm, D), pl.BlockSpec((N_EXPERTS, tm), lambda i: (0, i)),
                   _token_major_spec(tm, D)),
        out_shape=(jax.ShapeDtypeStruct((T, D), F32), jax.ShapeDtypeStruct((T * _tok_rows(D), LANES), F32),
                   jax.ShapeDtypeStruct((N_EXPERTS, T), F32),
                   jax.ShapeDtypeStruct((T * _tok_rows(D), LANES), F32)),
        compiler_params=_cparams(("arbitrary",)),
        name="post",
    )(x, merged, mod, g.reshape(1, D), wo, wr_t)


def _prefix_counts(mask, upper, lower_strict):
    within = _mm(mask, upper)
    row_tot = jnp.broadcast_to(within[:, LANES - 1:LANES], within.shape)
    row_start = _mm(lower_strict, row_tot)
    return row_start + within - mask, row_start, within


def _split_int(x):
    high = jnp.floor(x * (1.0 / 256.0))
    return high, x - high * 256.0


def _select_kernel(aff_ref, idx_ref, gate_ref, *, cap, slot_tile):
    R = aff_ref.shape[1]
    upper = (_iota((LANES, LANES), 0) <= _iota((LANES, LANES), 1)).astype(MXU)
    lower_strict = (_iota((R, R), 1) < _iota((R, R), 0)).astype(MXU)
    lane0 = (_iota((8, LANES), 1) == 0).astype(MXU)
    lane_id = _iota((slot_tile, LANES), 1)
    row_id = _iota((slot_tile, R), 1)

    aff = aff_ref[0]
    bits = pltpu.bitcast(aff, I32)

    def count_ge(cand):
        return jnp.sum((bits >= cand).astype(I32), axis=(0, 1), keepdims=True)

    def two_bit_step(j, thr):
        lo = jnp.left_shift(jnp.int32(1), 29 - 2 * j)
        c1, c2, c3 = thr | lo, thr | (lo * 2), thr | (lo * 3)
        n1, n2, n3 = count_ge(c1), count_ge(c2), count_ge(c3)
        return jnp.where(n3 >= cap, c3, jnp.where(n2 >= cap, c2, jnp.where(n1 >= cap, c1, thr)))

    thr = lax.fori_loop(0, 15, two_bit_step, jnp.zeros((1, 1), I32))
    thr = jnp.where(count_ge(thr | 1) >= cap, thr | 1, thr)
    gt = (bits > thr).astype(F32)
    eq = (bits == thr).astype(F32)
    need = float(cap) - jnp.sum(gt, axis=(0, 1), keepdims=True)
    eq_rank, _, _ = _prefix_counts(eq, upper, lower_strict)
    sel = gt + eq * (eq_rank < need).astype(F32)
    _, row_start, within = _prefix_counts(sel, upper, lower_strict)
    row_end = row_start + jnp.broadcast_to(within[:, LANES - 1:LANES], within.shape)
    end_hi, end_lo = _split_int(row_end)
    row_end_t = (_mm_nt(lane0, end_hi) * 256.0 + _mm_nt(lane0, end_lo))[0:1, :]
    start_hi, start_lo = _split_int(row_start)
    aff_parts = _split3(aff)

    def tile(t, carry):
        base = t * slot_tile
        slot = (base + _iota((slot_tile, 1), 0)).astype(F32)
        row = jnp.sum((row_end_t <= slot).astype(F32), axis=1, keepdims=True)
        onehot = (row_id.astype(F32) == row).astype(MXU)
        start = _mm(onehot, start_hi) * 256.0 + _mm(onehot, start_lo)
        rank = slot - start
        counts = _mm(onehot, within)
        col = jnp.sum((counts <= rank).astype(F32), axis=1, keepdims=True)
        vals = sum(_mm(onehot, p) for p in aff_parts)
        gate = jnp.sum(jnp.where(lane_id.astype(F32) == col, vals, 0.0), axis=1, keepdims=True)
        token = (row * float(LANES) + col).astype(I32)
        out_rows = pl.ds(pl.multiple_of(base, slot_tile), slot_tile)
        idx_ref[0, out_rows, :] = jnp.broadcast_to(token, (slot_tile, LANES))
        gate_ref[0, out_rows, :] = jnp.broadcast_to(gate, (slot_tile, LANES))
        return carry

    lax.fori_loop(0, cap // slot_tile, tile, 0)


def _select(aff_t, cap):
    E, n = aff_t.shape
    assert n % LANES == 0
    R = n // LANES
    slot_tile = min(512, cap)
    shape = (E, cap, LANES)
    return pl.pallas_call(
        functools.partial(_select_kernel, cap=cap, slot_tile=slot_tile),
        grid=(E,),
        in_specs=[pl.BlockSpec((1, R, LANES), lambda e: (e, 0, 0))],
        out_specs=(pl.BlockSpec((1, cap, LANES), lambda e: (e, 0, 0)),
                   pl.BlockSpec((1, cap, LANES), lambda e: (e, 0, 0))),
        out_shape=(jax.ShapeDtypeStruct(shape, I32), jax.ShapeDtypeStruct(shape, F32)),
        compiler_params=_cparams(("arbitrary",)),
        name="select",
    )(aff_t.reshape(E, R, LANES))


MOE_TILE = 512
MOE_ISSUE_UNROLL = 16


def _moe_kernel(idx_ref, idx_next_ref, h_hbm, gate_ref, wg_ref, wu_ref, wd_ref, acc_in, acc_hbm, xbuf, obuf, sem,
                *, ts, per):
    del acc_in
    tok_rows = xbuf.shape[1] // ts
    t = pl.program_id(1)
    step = pl.program_id(0) * per + t
    n_steps = pl.num_programs(0) * per
    cur, nxt = lax.rem(step, 2), lax.rem(step + 1, 2)
    ring, ring_next, ring_prev = lax.rem(step, 3), lax.rem(step + 1, 3), lax.rem(step + 2, 3)
    first, last = step == 0, step == n_steps - 1
    expert_start, expert_end = t == 0, t == per - 1
    prev_was_expert_start = (t == 1) if per > 1 else True

    def issue_row(ids_ref, kind, buf, i):
        tok = pl.ds(pl.multiple_of(ids_ref[0, 0, i] * tok_rows, tok_rows), tok_rows)
        row = pl.ds(pl.multiple_of(i * tok_rows, tok_rows), tok_rows)
        if kind == 0:
            cp = pltpu.make_async_copy(h_hbm.at[tok, :], xbuf.at[buf, row, :], sem.at[0, buf])
        elif kind == 1:
            cp = pltpu.make_async_copy(acc_hbm.at[tok, :], obuf.at[buf, row, :], sem.at[1, buf])
        else:
            cp = pltpu.make_async_copy(obuf.at[buf, row, :], acc_hbm.at[tok, :], sem.at[2, buf])
        cp.start()

    def issue(ids_ref, kind, buf):
        def body(i, c):
            issue_row(ids_ref, kind, buf, i)
            return c
        lax.fori_loop(0, ts, body, 0, unroll=MOE_ISSUE_UNROLL)

    def wait_all(kind, buf):
        ref = xbuf if kind == 0 else obuf
        pltpu.make_async_copy(ref.at[buf], ref.at[buf], sem.at[kind, buf]).wait()

    @pl.when(first)
    def _():
        issue(idx_ref, 0, cur)

    @pl.when(jnp.logical_and(step >= 2, jnp.logical_not(prev_was_expert_start)))
    def _():
        wait_all(2, ring_next)

    @pl.when(expert_start)
    def _():
        @pl.when(jnp.logical_not(first))
        def _():
            wait_all(2, ring_prev)
        issue(idx_ref, 1, ring)

    wait_all(0, cur)
    wait_all(1, ring)

    @pl.when(jnp.logical_not(last))
    def _():
        issue(idx_next_ref, 0, nxt)

    @pl.when(jnp.logical_not(expert_end))
    def _():
        issue(idx_next_ref, 1, ring_next)

    x = _read_token_major(xbuf, (cur,), ts).astype(MXU)
    hidden = (_silu(jnp.dot(x, wg_ref[0, 0], preferred_element_type=F32))
              * jnp.dot(x, wu_ref[0, 0], preferred_element_type=F32))
    y = jnp.dot(hidden.astype(MXU), wd_ref[0, 0], preferred_element_type=F32)
    g = gate_ref[0][:, :1]
    _write_token_major(obuf, (ring,), _read_token_major(obuf, (ring,), ts) + y * g)
    issue(idx_ref, 2, ring)

    @pl.when(last)
    def _():
        @pl.when(jnp.logical_and(jnp.logical_not(first), jnp.logical_not(expert_start)))
        def _():
            wait_all(2, ring_prev)
        wait_all(2, ring)


def _moe(h2, idx, gate, layer, wg, wu, wd, acc0):
    D, FF = wg.shape[2:]
    tok_rows = _tok_rows(D)
    n = h2.shape[0] // tok_rows
    E, cap, _ = gate.shape
    ts = min(MOE_TILE, cap)
    per = cap // ts
    n_steps = E * per
    idx_blocks = idx.reshape(n_steps, 1, ts)
    wspec = lambda shape: pl.BlockSpec((1, 1) + shape, lambda e, t: (layer, e, 0, 0))
    ids = lambda shift: pl.BlockSpec((1, 1, ts), lambda e, t: (jnp.minimum(e * per + t + shift, n_steps - 1), 0, 0),
                                     memory_space=pltpu.SMEM)
    return pl.pallas_call(
        functools.partial(_moe_kernel, ts=ts, per=per),
        grid=(E, per),
        in_specs=[ids(0), ids(1),
                  pl.BlockSpec(memory_space=pl.ANY),
                  pl.BlockSpec((1, ts, LANES), lambda e, t: (e, t, 0)),
                  wspec((D, FF)), wspec((D, FF)), wspec((FF, D)),
                  pl.BlockSpec(memory_space=pl.ANY)],
        out_specs=pl.BlockSpec(memory_space=pl.ANY),
        out_shape=jax.ShapeDtypeStruct((n * tok_rows, LANES), F32),
        scratch_shapes=[pltpu.VMEM((2, ts * tok_rows, LANES), F32), pltpu.VMEM((3, ts * tok_rows, LANES), F32),
                        pltpu.SemaphoreType.DMA((3, 3))],
        input_output_aliases={7: 0},
        compiler_params=_cparams(("arbitrary", "arbitrary")),
        name="moe",
    )(idx_blocks, idx_blocks, h2, gate, wg, wu, wd, acc0)


def _cast_kernel(g_ref, u_ref, d_ref, go_ref, uo_ref, do_ref):
    go_ref[...] = g_ref[...].astype(go_ref.dtype)
    uo_ref[...] = u_ref[...].astype(uo_ref.dtype)
    do_ref[...] = d_ref[...].astype(do_ref.dtype)


def _cast_experts(w_gate, w_up, w_down):
    L, E, D, FF = w_gate.shape
    halves = 2
    assert D % (8 * halves) == 0 and FF % (8 * halves) == 0
    spec = lambda rows, cols: pl.BlockSpec((1, 1, rows // halves, cols), lambda g, h: (g // E, g % E, h, 0))
    specs = [spec(D, FF), spec(D, FF), spec(FF, D)]
    return pl.pallas_call(
        _cast_kernel,
        grid=(L * E, halves),
        in_specs=specs,
        out_specs=tuple(specs),
        out_shape=tuple(jax.ShapeDtypeStruct(w.shape, MXU) for w in (w_gate, w_up, w_down)),
        compiler_params=_cparams(("arbitrary", "arbitrary")),
        name="cast_experts",
    )(w_gate, w_up, w_down)


def _final_kernel(xa_ref, xb_ref, mod_ref, g_ref, o_ref):
    x = xa_ref[...] + mod_ref[0, 5:6, :] * _read_token_major(xb_ref, (), xa_ref.shape[0])
    o_ref[...] = x * lax.rsqrt(jnp.mean(x * x, axis=-1, keepdims=True) + EPS) * g_ref[...]


def _final(xa, xb, mod, g, n_per_batch):
    T, D = xa.shape
    tm = min(512, T)
    tile = pl.BlockSpec((tm, D), lambda i: (i, 0))
    return pl.pallas_call(
        _final_kernel,
        grid=(T // tm,),
        in_specs=[tile, _token_major_spec(tm, D),
                  pl.BlockSpec((1, 6, D), _tile_batch_map(tm, n_per_batch, mod.shape[0])), _resident((1, D))],
        out_specs=tile,
        out_shape=jax.ShapeDtypeStruct((T, D), F32),
        compiler_params=_cparams(("arbitrary",)),
        name="final_norm",
    )(xa, xb, mod, g.reshape(1, D))


def _rope_tables(n):
    quarter = DK_B // 4
    t = jnp.arange(n)
    row = (t // GRID_W).astype(F32)
    col = (t % GRID_W).astype(F32)
    inv = ROPE_BASE ** (-jnp.arange(quarter, dtype=F32) / quarter)
    ang = jnp.concatenate([row[:, None] * inv, col[:, None] * inv], axis=-1)
    cos, sin = jnp.cos(ang), jnp.sin(ang)
    cos_t = jnp.concatenate([cos, cos] * (LANES // DK_B), axis=-1)
    sin_t = jnp.concatenate([-sin, sin] * (LANES // DK_B), axis=-1)
    return cos_t, sin_t


def _prep_layer(l, w_in, gla_w_gf, gla_b_gf, gla_w_gb, gla_b_gb, ret_ld_f, ret_ld_b):
    D = w_in.shape[1]
    w = w_in[l]
    s = np.cumsum([0, QK_A, QK_A, V_A, V_A, GATE_RANK, GATE_RANK, QK_B, QK_B, V_B, V_B, W_C, W_C, W_C])
    a_q, a_k, a_v, a_r, a_zf, a_zb, b_q, b_k, b_v, b_g, c_q, c_k, c_v = [w[:, s[i]:s[i + 1]] for i in range(13)]
    zpad = jnp.zeros((D, LANES - 2 * GATE_RANK), w.dtype)
    w1a = jnp.concatenate([a_q, a_k, a_r, b_q, b_k, b_g, a_zf, a_zb, zpad], axis=1).astype(MXU)
    w1b = jnp.concatenate([a_v, b_v, c_q, c_k, c_v], axis=1).astype(MXU)
    wm = w[:, MIX_W:].astype(MXU)
    wf = jnp.zeros((LANES, QK_A), F32).at[:GATE_RANK].set(gla_w_gf[l])
    wb = jnp.zeros((LANES, QK_A), F32).at[GATE_RANK:2 * GATE_RANK].set(gla_w_gb[l])
    ld = lambda v: jnp.broadcast_to(v[l].reshape(H_B // 2, 2, 1), (H_B // 2, 2, 2 * LANES))
    return dict(w1a=w1a, w1b=w1b, wm=wm, wf=wf, bf=gla_b_gf[l].reshape(1, QK_A), wb=wb, bb=gla_b_gb[l].reshape(1, QK_A),
                ldf=ld(ret_ld_f), ldb=ld(ret_ld_b))


def kernel(x_prompt, x_sample, cache_nat_k, cache_nat_v, state_gla_fwd, state_gla_bwd, state_ret_fwd,
           state_ret_bwd, c, c_ctx, norm1, norm2, w_ada, b_ada, w_in, gla_w_gf, gla_b_gf, gla_w_gb, gla_b_gb,
           gla_gn, ret_log_decay_f, ret_log_decay_b, ret_gn, nat_rpb, w_br_a, w_br_b, w_br_c, w_out, w_router,
           w_gate, w_up, w_down, final_norm):
    B, n_ctx, D = x_prompt.shape
    Bd, n_lat, _ = x_sample.shape
    L = w_in.shape[0]
    past = cache_nat_k.shape[2]

    cond = jnp.concatenate([c_ctx[None, :], c], axis=0)
    n_cond = cond.shape[0]
    cond = jnp.pad(cond, ((0, -n_cond % 8), (0, 0)))
    mod_all = _ada(cond, w_ada, b_ada).reshape(L, cond.shape[0], 6, D)
    rope_tabs = _rope_tables(n_lat)
    wg_all, wu_all, wd_all = _cast_experts(w_gate, w_up, w_down)
    zero_a =jnp.zeros((B, H_A, DK_A, DV_A), F32)
    zero_b = jnp.zeros((B, H_B, DK_B, DV_B), F32)

    paths = {
        "ctx": dict(x=x_prompt.reshape(B * n_ctx, D), moe=None, B=B, n=n_ctx),
        "lat": dict(x=x_sample.reshape(Bd * n_lat, D), moe=None, B=Bd, n=n_lat),
    }
    outs = dict(nk=[], nv=[], gf=[], gb=[], rf=[], rb=[])
    mod_prev = {}
    for l in range(L):
        w = _prep_layer(l, w_in, gla_w_gf, gla_b_gf, gla_w_gb, gla_b_gb, ret_log_decay_f, ret_log_decay_b)
        wa, wb_, wc = w_br_a[l].astype(MXU), w_br_b[l].astype(MXU), w_br_c[l].astype(MXU)
        wo = w_out[l].astype(MXU)
        wr_t = w_router[l].T.astype(MXU)
        gn_a, gn_b = gla_gn[l].reshape(1, DV_A), ret_gn[l].reshape(1, DV_B)
        bias = _window_bias(nat_rpb[l].astype(F32), n_lat // GRID_W)
        mods = {"ctx": mod_all[l, 0:1], "lat": mod_all[l, 1:n_cond]}
        for name, st in paths.items():
            Bp, n = st["B"], st["n"]
            latent = name == "lat"
            mod = mods[name]
            Pa, Pb, x, nat_k, nat_v = _in_proj(st["x"], st["moe"], mod_prev.get(name), mod, norm1[l],
                                               w["w1a"], w["w1b"], n, emit_kv=not latent)
            if latent:
                sa_f0, sa_b0 = state_gla_fwd[:, l].astype(F32), state_gla_bwd[:, l].astype(F32)
                sb_f0, sb_b0 = state_ret_fwd[:, l].astype(F32), state_ret_bwd[:, l].astype(F32)
            else:
                sa_f0 = sa_b0 = zero_a
                sb_f0 = sb_b0 = zero_b
            o_a, sa_f, sa_b = _gla(Pa, Pb, Bp, n, w["wf"], w["bf"], w["wb"], w["bb"], gn_a, sa_f0, sa_b0)
            o_b, sb_f, sb_b = _ret(Pa, Pb, Bp, n, w["ldf"], w["ldb"], gn_b, sb_f0, sb_b0,
                                   rope_tabs if latent else None)
            if latent:
                k_ctx = cache_nat_k[:, l].astype(F32).reshape(Bp, past, W_C)
                v_ctx = cache_nat_v[:, l].astype(F32).reshape(Bp, past, W_C)
                o_c = _attn_lat(Pb, Bp, n, k_ctx, v_ctx, bias)
            else:
                o_c = _attn_ctx(Pb, Bp, n)
                outs["nk"].append(nat_k.reshape(Bp, n, H_C, D_C))
                outs["nv"].append(nat_v.reshape(Bp, n, H_C, D_C))
                outs["gf"].append(sa_f)
                outs["gb"].append(sa_b)
                outs["rf"].append(sb_f)
                outs["rb"].append(sb_b)
            merged = _merge(x, mod, norm1[l], o_a, o_b, o_c, w["wm"], wa, wb_, wc, n)
            x1, h2, aff_t, acc0 = _post(x, merged, mod, norm2[l], wo, wr_t, n)
            cap = EC_FACTOR * (Bp * n) // N_EXPERTS
            idx, gate = _select(aff_t, cap)
            st["moe"] = _moe(h2, idx[:, :, 0], gate, l, wg_all, wu_all, wd_all, acc0)
            st["x"] = x1
            mod_prev[name] = mod
    y = {name: _final(st["x"], st["moe"], mod_prev[name], final_norm, st["n"]) for name, st in paths.items()}
    stack = lambda xs: jnp.stack(xs, axis=1)
    return (y["ctx"].reshape(B, n_ctx, D), y["lat"].reshape(Bd, n_lat, D), stack(outs["nk"]), stack(outs["nv"]),
            stack(outs["gf"]), stack(outs["gb"]), stack(outs["rf"]), stack(outs["rb"]))
```

```python
import functools

import numpy as np
import jax
import jax.numpy as jnp
from jax import lax
from jax.experimental import pallas as pl
from jax.experimental.pallas import tpu as pltpu

F32 = jnp.float32
I32 = jnp.int32
MXU = jnp.bfloat16

GRID_W = 64
H_A, DK_A, DV_A = 4, 64, 128
GATE_RANK, GATE_TEMP = 16, 16.0
H_B, DK_B, DV_B = 4, 64, 128
H_C, D_C = 8, 64
WIN_H, WIN_W = 8, 16
CHUNK = 64
N_EXPERTS, EC_FACTOR = 16, 2
ROPE_BASE = 10000.0
EPS = 1e-6
NEG_INF = -1e30

QK_A, V_A = H_A * DK_A, H_A * DV_A
QK_B, V_B = H_B * DK_B, H_B * DV_B
W_C = H_C * D_C
C_AQ, C_AK, C_AR = 0, 256, 512
C_BQ, C_BK, C_BG, C_Z = 1024, 1280, 1536, 2048
C_AV, C_BV, C_CQ, C_CK, C_CV = 0, 512, 1024, 1536, 2048
MIX_W = 4640

LANES = 128
SUPER = 256
NCH = SUPER // CHUNK
Q_ROWS = 4
K_ROWS = Q_ROWS + WIN_H
VMEM_LIMIT = 56 * 1024 * 1024


def _cparams(sem):
    return pltpu.CompilerParams(dimension_semantics=sem, vmem_limit_bytes=VMEM_LIMIT)


def _mm(a, b):
    return jnp.dot(a.astype(MXU), b.astype(MXU), preferred_element_type=F32)


def _mm_nt(a, b):
    return lax.dot_general(a.astype(MXU), b.astype(MXU), (((1,), (1,)), ((), ())), preferred_element_type=F32)


def _iota(shape, dim):
    return lax.broadcasted_iota(I32, shape, dim)


def _sigmoid(x):
    return 1.0 / (1.0 + jnp.exp(-x))


def _silu(x):
    return x * _sigmoid(x)


def _resident(shape):
    nd = len(shape)
    return pl.BlockSpec(shape, lambda *_: (0,) * nd, pipeline_mode=pl.Buffered(1))


def _ada_kernel(cond_ref, w_ref, b_ref, o_ref):
    o_ref[0] = _mm(_silu(cond_ref[...]), w_ref[0]) + b_ref[0]


def _ada(cond, w_ada, b_ada):
    L, D, D6 = w_ada.shape
    R = cond.shape[0]
    tn = 1024 if D6 % 1024 == 0 else D
    assert D6 % tn == 0
    return pl.pallas_call(
        _ada_kernel,
        grid=(L, D6 // tn),
        in_specs=[pl.BlockSpec((R, D), lambda l, j: (0, 0)),
                  pl.BlockSpec((1, D, tn), lambda l, j: (l, 0, j)),
                  pl.BlockSpec((1, 1, tn), lambda l, j: (l, 0, j))],
        out_specs=pl.BlockSpec((1, R, tn), lambda l, j: (l, 0, j)),
        out_shape=jax.ShapeDtypeStruct((L, R, D6), F32),
        compiler_params=_cparams(("arbitrary", "arbitrary")),
        name="ada",
    )(cond, w_ada, b_ada.reshape(L, 1, D6))


def _norm_mod(x, g, shift, scale):
    y = x * lax.rsqrt(jnp.mean(x * x, axis=-1, keepdims=True) + EPS) * g
    return y * (1.0 + scale) + shift


def _tile_batch_map(tm, n_per_batch, nb):
    if nb == 1:
        return lambda i: (0, 0, 0)
    assert n_per_batch % tm == 0
    per = n_per_batch // tm
    return lambda i: (i // per, 0, 0)


def _col_chunks(width, step):
    return [(s, min(step, width - s)) for s in range(0, width, step)]


def _tok_rows(D):
    assert D % LANES == 0
    return D // LANES


def _read_token_major(ref, lead, n_tok):
    r = ref.shape[-2] // n_tok
    parts = [ref[lead + (pl.ds(j, n_tok, stride=r), slice(None))] for j in range(r)]
    return jnp.concatenate(parts, axis=1)


def _write_token_major(ref, lead, value):
    n_tok = value.shape[0]
    r = ref.shape[-2] // n_tok
    for j in range(r):
        ref[lead + (pl.ds(j, n_tok, stride=r), slice(None))] = value[:, j * LANES:(j + 1) * LANES]


def _token_major_spec(tm, D):
    return pl.BlockSpec((tm * _tok_rows(D), LANES), lambda i: (i, 0))


def _in_kernel(*refs, has_res, emit_kv):
    refs = list(refs)
    xa_ref = refs.pop(0)
    x = xa_ref[...]
    if has_res:
        xb_ref, modp_ref = refs.pop(0), refs.pop(0)
        x = x + modp_ref[0, 5:6, :] * _read_token_major(xb_ref, (), xa_ref.shape[0])
    mod_ref, g_ref, wa_ref, wb_ref, pa_ref, pb_ref = refs[:6]
    outs = refs[6:]
    if has_res:
        outs.pop(0)[...] = x
    h = _norm_mod(x, g_ref[...], mod_ref[0, 0:1, :], mod_ref[0, 1:2, :]).astype(MXU)
    for s, w in _col_chunks(wa_ref.shape[1], 512):
        pa_ref[:, s:s + w] = jnp.dot(h, wa_ref[:, s:s + w], preferred_element_type=F32)
    for s, w in _col_chunks(wb_ref.shape[1], 512):
        part = jnp.dot(h, wb_ref[:, s:s + w], preferred_element_type=F32)
        pb_ref[:, s:s + w] = part.astype(pb_ref.dtype)
        if emit_kv:
            if s == C_CK:
                outs[0][...] = part
            if s == C_CV:
                outs[1][...] = part


def _in_proj(xa, xb, mod_prev, mod, g, w1a, w1b, n_per_batch, emit_kv):
    T, D = xa.shape
    nb = mod.shape[0]
    has_res = xb is not None
    tm = min(256 if has_res else 512, T)
    bmap = _tile_batch_map(tm, n_per_batch, nb)
    tile = lambda w: pl.BlockSpec((tm, w), lambda i: (i, 0))
    mspec = pl.BlockSpec((1, 6, D), bmap)
    assert W_C == 512 and C_CK % 512 == 0 and C_CV % 512 == 0
    ins = [xa] + ([xb, mod_prev] if has_res else []) + [mod, g.reshape(1, D), w1a, w1b]
    in_specs = ([tile(D)] + ([_token_major_spec(tm, D), mspec] if has_res else [])
                + [mspec, _resident((1, D)), _resident(w1a.shape), _resident(w1b.shape)])
    outs = ([(w1a.shape[1], F32), (w1b.shape[1], MXU)] + ([(D, F32)] if has_res else [])
            + ([(W_C, F32), (W_C, F32)] if emit_kv else []))
    out = list(pl.pallas_call(
        functools.partial(_in_kernel, has_res=has_res, emit_kv=emit_kv),
        grid=(T // tm,),
        in_specs=in_specs,
        out_specs=tuple(tile(w) for w, _ in outs),
        out_shape=tuple(jax.ShapeDtypeStruct((T, w), dt) for w, dt in outs),
        compiler_params=_cparams(("arbitrary",)),
        name="in_proj",
    )(*ins))
    Pa, Pb = out.pop(0), out.pop(0)
    x = out.pop(0) if has_res else xa
    k, v = out if emit_kv else (None, None)
    return Pa, Pb, x, k, v


def _split3(x):
    p1 = x.astype(MXU)
    r1 = x - p1.astype(F32)
    p2 = r1.astype(MXU)
    p3 = (r1 - p2.astype(F32)).astype(MXU)
    return p1, p2, p3


def _chunk_masks():
    row, col = _iota((SUPER, SUPER), 0), _iota((SUPER, SUPER), 1)
    same = (row // CHUNK) == (col // CHUNK)
    return same, same & (col <= row), same & (col >= row)


def _head_replicate(h):
    return (_iota((LANES, SUPER), 0) == (_iota((LANES, SUPER), 1) % CHUNK) + DK_A * h).astype(MXU)


def _recur_step(q_in, kdec_t, att, v, state, dec_fn, same, fwd, rep):
    o_intra = _mm(att, v)
    kblk = jnp.where(same, jnp.concatenate([kdec_t] * NCH, axis=0), 0.0)
    upd = _mm(kblk, v)
    order = range(NCH) if fwd else range(NCH - 1, -1, -1)
    prev = [None] * NCH
    for c in order:
        prev[c] = state
        state = dec_fn(c) * state + upd[c * CHUNK:(c + 1) * CHUNK]
    q_blk = jnp.where(same, _mm(q_in, rep), 0.0)
    o_inter = _mm(q_blk, jnp.concatenate(prev, axis=0))
    return o_intra + o_inter, state


def _head_norm_gate(o, g, r):
    return o * lax.rsqrt(jnp.mean(o * o, axis=-1, keepdims=True) + EPS) * g * _silu(r)


def _gla_kernel(q_ref, k_ref, v_ref, r_ref, z_ref, wf_ref, bf_ref, wb_ref, bb_ref, gn_ref, s0f_ref, s0b_ref,
                o_ref, sf_ref, sb_ref, acc_ref, *, n_sc):
    same, tri_f, tri_b = _chunk_masks()
    reps = [_head_replicate(h) for h in range(2)]
    lane_head = _iota((1, LANES), 1) // DK_A

    def run(sc, states, fwd):
        off = pl.multiple_of(sc * SUPER, SUPER)
        rows = pl.ds(off, SUPER)
        q = q_ref[rows, :] * (DK_A ** -0.5)
        k = k_ref[rows, :]
        x = _mm(z_ref[rows, :], (wf_ref if fwd else wb_ref)[...]) + (bf_ref if fwd else bb_ref)[...]
        la = (jnp.minimum(x, 0.0) - jnp.log1p(jnp.exp(-jnp.abs(x)))) * (1.0 / GATE_TEMP)
        tri = tri_f if fwd else tri_b
        hi = la.astype(MXU)
        lo = la - hi.astype(F32)
        trim = tri.astype(MXU)
        b = _mm(trim, hi) + _mm(trim, lo)
        last = CHUNK - 1 if fwd else 0
        tot_rows = [b[c * CHUNK + last:c * CHUNK + last + 1, :] for c in range(NCH)]
        b_last = jnp.concatenate([jnp.broadcast_to(t, (CHUNK, LANES)) for t in tot_rows], axis=0)
        tot_t = jnp.transpose(jnp.concatenate(tot_rows + [jnp.zeros((LANES - NCH, LANES), F32)], axis=0))
        q_in = q * jnp.exp(b)
        k_out = k * jnp.exp(-b)
        kdec_t = jnp.transpose(k * jnp.exp(b_last - b))
        outs, new_states = [], []
        for h in range(2):
            head_rows = slice(h * DK_A, (h + 1) * DK_A)
            dec_fn = lambda c, t=tot_t[head_rows]: jnp.exp(jnp.broadcast_to(t[:, c:c + 1], (DK_A, DV_A)))
            qh = jnp.where(lane_head == h, q_in, 0.0)
            att = jnp.where(tri, _mm_nt(qh, k_out), 0.0)
            v = v_ref[rows, h * DV_A:(h + 1) * DV_A]
            o, s = _recur_step(qh, kdec_t[head_rows], att, v, states[h], dec_fn, same, fwd, reps[h])
            outs.append(o)
            new_states.append(s)
        return rows, outs, tuple(new_states)

    sf, sb = _both_directions(run, n_sc, acc_ref, o_ref, gn_ref, r_ref,
                              (s0f_ref[0, 0], s0f_ref[0, 1]), (s0b_ref[0, 0], s0b_ref[0, 1]))
    for h in range(2):
        sf_ref[0, h] = sf[h]
        sb_ref[0, h] = sb[h]


def _both_directions(run, n_sc, acc_ref, o_ref, gn_ref, gate_ref, sf, sb, unroll=1):
    dv = acc_ref.shape[1] // 2

    def park(rows, outs):
        for h in range(2):
            acc_ref[rows, h * dv:(h + 1) * dv] = outs[h]

    def finish(rows, outs, other=None):
        for h in range(2):
            cols = slice(h * dv, (h + 1) * dv)
            total = outs[h] + (acc_ref[rows, cols] if other is None else other[h])
            o_ref[rows, cols] = _head_norm_gate(total, gn_ref[...], gate_ref[rows, cols]).astype(o_ref.dtype)

    if n_sc == 1:
        rows, outs_f, sf = run(0, sf, True)
        _, outs_b, sb = run(0, sb, False)
        finish(rows, outs_f, outs_b)
        return sf, sb
    assert n_sc % 2 == 0

    def body(second):
        def step(i, carry):
            sf, sb = carry
            rows_f, outs_f, sf = run(i, sf, True)
            rows_b, outs_b, sb = run(n_sc - 1 - i, sb, False)
            (finish if second else park)(rows_f, outs_f)
            (finish if second else park)(rows_b, outs_b)
            return sf, sb
        return step

    carry = lax.fori_loop(0, n_sc // 2, body(False), (sf, sb), unroll=unroll)
    return lax.fori_loop(n_sc // 2, n_sc, body(True), carry, unroll=unroll)


def _pair_specs(n, col_q, col_k, col_v, col_r):
    return [pl.BlockSpec((n, LANES), lambda b, p: (b, col_q // LANES + p)),
            pl.BlockSpec((n, LANES), lambda b, p: (b, col_k // LANES + p)),
            pl.BlockSpec((n, 2 * DV_A), lambda b, p: (b, col_v // (2 * DV_A) + p)),
            pl.BlockSpec((n, 2 * DV_A), lambda b, p: (b, col_r // (2 * DV_A) + p))]


def _state_spec():
    return pl.BlockSpec((1, 2, DK_A, DV_A), lambda b, p: (b, p, 0, 0))


def _gla(Pa, Pb, B, n, wf, bf, wb, bb, gn, s0f, s0b):
    T = B * n
    assert n % SUPER == 0
    pair_w = pl.BlockSpec((LANES, LANES), lambda b, p: (0, p))
    pair_b = pl.BlockSpec((1, LANES), lambda b, p: (0, p))
    st_shape = jax.ShapeDtypeStruct((B, H_A, DK_A, DV_A), F32)
    return pl.pallas_call(
        functools.partial(_gla_kernel, n_sc=n // SUPER),
        grid=(B, H_A // 2),
        in_specs=_pair_specs(n, C_AQ, C_AK, C_AV, C_AR)
        + [pl.BlockSpec((n, LANES), lambda b, p: (b, C_Z // LANES)),
           pair_w, pair_b, pair_w, pair_b, _resident((1, DV_A)), _state_spec(), _state_spec()],
        out_specs=(pl.BlockSpec((n, 2 * DV_A), lambda b, p: (b, p)), _state_spec(), _state_spec()),
        out_shape=(jax.ShapeDtypeStruct((T, V_A), MXU), st_shape, st_shape),
        scratch_shapes=[pltpu.VMEM((n, 2 * DV_A), F32)],
        compiler_params=_cparams(("arbitrary", "arbitrary")),
        name="gla",
    )(Pa, Pa, Pb, Pa, Pa, wf, bf, wb, bb, gn, s0f, s0b)


def _ret_kernel(*refs, n_sc, rope):
    if rope:
        (q_ref, k_ref, v_ref, g_ref, ldf_ref, ldb_ref, gn_ref, s0f_ref, s0b_ref, cos_ref, sin_ref,
         o_ref, sf_ref, sb_ref, acc_ref) = refs
    else:
        (q_ref, k_ref, v_ref, g_ref, ldf_ref, ldb_ref, gn_ref, s0f_ref, s0b_ref,
         o_ref, sf_ref, sb_ref, acc_ref) = refs
    same, tri_f, tri_b = _chunk_masks()
    reps = [_head_replicate(h) for h in range(2)]
    lane_head = _iota((1, LANES), 1) // DK_B
    half = DK_B // 2
    first_half = (_iota((1, LANES), 1) % DK_B) < half
    diff = jnp.abs(_iota((SUPER, SUPER), 0) - _iota((SUPER, SUPER), 1)).astype(F32)
    pos = (_iota((SUPER, LANES), 0) % CHUNK).astype(F32)
    decays = {fwd: [jnp.where(tri_f if fwd else tri_b, jnp.exp(ld_ref[0, h:h + 1, :] * diff), 0.0)
                    for h in range(2)]
              for fwd, ld_ref in ((True, ldf_ref), (False, ldb_ref))}

    def rotate(x, rows):
        swapped = jnp.where(first_half, pltpu.roll(x, LANES - half, 1), pltpu.roll(x, half, 1))
        return x * cos_ref[rows, :] + swapped * sin_ref[rows, :]

    def run(sc, states, fwd):
        off = pl.multiple_of(sc * SUPER, SUPER)
        rows = pl.ds(off, SUPER)
        q = q_ref[rows, :]
        k = k_ref[rows, :] * (DK_B ** -0.5)
        if rope:
            q, k = rotate(q, rows), rotate(k, rows)
        ld = (ldf_ref if fwd else ldb_ref)[0]
        lg_lane = jnp.where(lane_head == 0, ld[0:1, :LANES], ld[1:2, :LANES])
        if fwd:
            q_in = q * jnp.exp(lg_lane * (pos + 1.0))
            k_dec = k * jnp.exp(lg_lane * (CHUNK - 1.0 - pos))
        else:
            q_in = q * jnp.exp(lg_lane * (CHUNK - pos))
            k_dec = k * jnp.exp(lg_lane * pos)
        kdec_t = jnp.transpose(k_dec)
        outs, new_states = [], []
        for h in range(2):
            qh = jnp.where(lane_head == h, q, 0.0)
            att = _mm_nt(qh, k) * decays[fwd][h]
            chunk_dec = jnp.exp(ld[h:h + 1, :DV_B] * float(CHUNK))
            v = v_ref[rows, h * DV_B:(h + 1) * DV_B]
            o, s = _recur_step(q_in, kdec_t[h * DK_B:(h + 1) * DK_B], att, v, states[h], lambda c: chunk_dec,
                               same, fwd, reps[h])
            outs.append(o)
            new_states.append(s)
        return rows, outs, tuple(new_states)

    sf, sb = _both_directions(run, n_sc, acc_ref, o_ref, gn_ref, g_ref,
                              (s0f_ref[0, 0], s0f_ref[0, 1]), (s0b_ref[0, 0], s0b_ref[0, 1]), unroll=2)
    for h in range(2):
        sf_ref[0, h] = sf[h]
        sb_ref[0, h] = sb[h]


def _ret(Pa, Pb, B, n, ldf, ldb, gn, s0f, s0b, rope_tabs):
    T = B * n
    rope = rope_tabs is not None
    ld_spec = pl.BlockSpec((1, 2, 2 * LANES), lambda b, p: (p, 0, 0))
    st_shape = jax.ShapeDtypeStruct((B, H_B, DK_B, DV_B), F32)
    ins = [Pa, Pa, Pb, Pa, ldf, ldb, gn, s0f, s0b] + (list(rope_tabs) if rope else [])
    return pl.pallas_call(
        functools.partial(_ret_kernel, n_sc=n // SUPER, rope=rope),
        grid=(B, H_B // 2),
        in_specs=_pair_specs(n, C_BQ, C_BK, C_BV, C_BG)
        + [ld_spec, ld_spec, _resident((1, DV_B)), _state_spec(), _state_spec()]
        + ([_resident((n, LANES))] * 2 if rope else []),
        out_specs=(pl.BlockSpec((n, 2 * DV_B), lambda b, p: (b, p)), _state_spec(), _state_spec()),
        out_shape=(jax.ShapeDtypeStruct((T, V_B), MXU), st_shape, st_shape),
        scratch_shapes=[pltpu.VMEM((n, 2 * DV_B), F32)],
        compiler_params=_cparams(("arbitrary", "arbitrary")),
        name="retention",
    )(*ins)


def _attn_ctx_kernel(q_ref, k_ref, v_ref, o_ref):
    lane_head = _iota((1, LANES), 1) // D_C
    q = q_ref[...] * jnp.asarray(D_C ** -0.5, q_ref.dtype)
    k = k_ref[...]
    v = v_ref[...]
    out = jnp.zeros(q.shape, F32)
    for h in range(2):
        s = _mm_nt(jnp.where(lane_head == h, q, jnp.zeros_like(q)), k)
        e = jnp.exp(s - jnp.max(s, axis=-1, keepdims=True))
        p = e / jnp.sum(e, axis=-1, keepdims=True)
        out = jnp.where(lane_head == h, _mm(p, v), out)
    o_ref[...] = out.astype(o_ref.dtype)


def _attn_ctx(P, B, n):
    spec = lambda col: pl.BlockSpec((n, LANES), lambda b, p: (b, col // LANES + p))
    return pl.pallas_call(
        _attn_ctx_kernel,
        grid=(B, H_C // 2),
        in_specs=[spec(C_CQ), spec(C_CK), spec(C_CV)],
        out_specs=pl.BlockSpec((n, LANES), lambda b, p: (b, p)),
        out_shape=jax.ShapeDtypeStruct((B * n, W_C), MXU),
        compiler_params=_cparams(("arbitrary", "arbitrary")),
        name="attn_ctx",
    )(P, P, P)


def _attn_lat_kernel(q_ref, k_ref, v_ref, kc_ref, vc_ref, bias_ref, o_ref, *, n_blk, key_start_max):
    lane_head = _iota((1, LANES), 1) // D_C
    nq, nk = Q_ROWS * GRID_W, K_ROWS * GRID_W
    kc = kc_ref[0].astype(MXU)
    vc = vc_ref[0].astype(MXU)

    def body(i, carry):
        qrows = pl.ds(pl.multiple_of(i * nq, nq), nq)
        kstart = jnp.clip(i * Q_ROWS - WIN_H // 2, 0, key_start_max)
        krows = pl.ds(pl.multiple_of(kstart * GRID_W, GRID_W), nk)
        variant = jnp.where(i == 0, 0, jnp.where(i == n_blk - 1, 2, 1))
        q = q_ref[qrows, :] * jnp.asarray(D_C ** -0.5, q_ref.dtype)
        k = k_ref[krows, :]
        v = v_ref[krows, :]
        out = jnp.zeros((nq, LANES), F32)
        for h in range(2):
            qh = jnp.where(lane_head == h, q, jnp.zeros_like(q))
            s_win = _mm_nt(qh, k) + bias_ref[h, variant]
            s_ctx = _mm_nt(qh, kc)
            m = jnp.maximum(jnp.max(s_win, axis=-1, keepdims=True), jnp.max(s_ctx, axis=-1, keepdims=True))
            e_win = jnp.exp(s_win - m)
            e_ctx = jnp.exp(s_ctx - m)
            inv = 1.0 / (jnp.sum(e_win, axis=-1, keepdims=True) + jnp.sum(e_ctx, axis=-1, keepdims=True))
            o = (_mm(e_win, v) + _mm(e_ctx, vc)) * inv
            out = jnp.where(lane_head == h, o, out)
        o_ref[qrows, :] = out.astype(o_ref.dtype)
        return carry

    lax.fori_loop(0, n_blk, body, 0, unroll=2)


def _window_bias(rpb, rows):
    n_blk = rows // Q_ROWS
    kh = min(WIN_H, rows)
    qc = np.arange(GRID_W)
    win_c = np.clip(qc - WIN_W // 2, 0, GRID_W - WIN_W)
    kc = np.arange(GRID_W)
    col_ok = (kc[None, :] >= win_c[:, None]) & (kc[None, :] < win_c[:, None] + WIN_W)
    coff = np.clip(kc[None, :] - qc[:, None], -(WIN_W - 1), WIN_W - 1) + WIN_W - 1
    col_pick = (coff[:, :, None] == np.arange(2 * WIN_W - 1)).astype(np.float32)
    exact = lax.Precision.HIGHEST
    tabs = []
    for blk in (0, min(1, n_blk - 1), n_blk - 1):
        r = blk * Q_ROWS + np.arange(Q_ROWS)
        kstart = int(np.clip(blk * Q_ROWS - WIN_H // 2, 0, rows - K_ROWS))
        kr = kstart + np.arange(K_ROWS)
        r0 = np.clip(r - WIN_H // 2, 0, rows - kh)
        row_ok = (kr[None, :] >= r0[:, None]) & (kr[None, :] < r0[:, None] + kh)
        roff = np.clip(kr[None, :] - r[:, None] + WIN_H - 1, 0, 2 * WIN_H - 2)
        ok = row_ok[:, None, :, None] & col_ok[None, :, None, :]
        row_pick = (roff[:, :, None] == np.arange(2 * WIN_H - 1)).astype(np.float32)
        by_row = jnp.einsum('qka,hab->hqkb', row_pick, rpb, precision=exact)
        bias = jnp.einsum('hqkb,cdb->hqckd', by_row, col_pick, precision=exact)
        tab = jnp.where(ok[None], bias, NEG_INF)
        tabs.append(tab.reshape(rpb.shape[0], Q_ROWS * GRID_W, K_ROWS * GRID_W))
    return jnp.stack(tabs, axis=1)


def _attn_lat(P, B, n, k_ctx, v_ctx, bias):
    rows = n // GRID_W
    assert rows % Q_ROWS == 0 and rows >= K_ROWS
    L_ctx = k_ctx.shape[1]
    spec = lambda col: pl.BlockSpec((n, LANES), lambda p, b: (b, col // LANES + p))
    cspec = pl.BlockSpec((1, L_ctx, LANES), lambda p, b: (b, 0, p))
    nq, nk = Q_ROWS * GRID_W, K_ROWS * GRID_W
    return pl.pallas_call(
        functools.partial(_attn_lat_kernel, n_blk=rows // Q_ROWS, key_start_max=rows - K_ROWS),
        grid=(H_C // 2, B),
        in_specs=[spec(C_CQ), spec(C_CK), spec(C_CV), cspec, cspec,
                  pl.BlockSpec((2, 3, nq, nk), lambda p, b: (p, 0, 0, 0))],
        out_specs=pl.BlockSpec((n, LANES), lambda p, b: (b, p)),
        out_shape=jax.ShapeDtypeStruct((B * n, W_C), MXU),
        compiler_params=_cparams(("arbitrary", "arbitrary")),
        name="attn_lat",
    )(P, P, P, k_ctx, v_ctx, bias)


def _merge_kernel(x_ref, mod_ref, g_ref, oa_ref, ob_ref, oc_ref, wm_ref, wa_ref, wb_ref, wc_ref, m_ref):
    D = x_ref.shape[1]
    h = _norm_mod(x_ref[...], g_ref[...], mod_ref[0, 0:1, :], mod_ref[0, 1:2, :]).astype(MXU)
    for s, w in _col_chunks(D, 512):
        acc = None
        for i, (o_ref, wbr_ref) in enumerate(((oa_ref, wa_ref), (ob_ref, wb_ref), (oc_ref, wc_ref))):
            gate = _sigmoid(jnp.dot(h, wm_ref[:, i * D + s:i * D + s + w], preferred_element_type=F32))
            term = gate * jnp.dot(o_ref[...], wbr_ref[:, s:s + w], preferred_element_type=F32)
            acc = term if acc is None else acc + term
        m_ref[:, s:s + w] = acc.astype(m_ref.dtype)


def _merge(x, mod, g, oa, ob, oc, wm, wa, wb, wc, n_per_batch):
    T, D = x.shape
    tm = min(512, T)
    bmap = _tile_batch_map(tm, n_per_batch, mod.shape[0])
    tile = lambda w: pl.BlockSpec((tm, w), lambda i: (i, 0))
    return pl.pallas_call(
        _merge_kernel,
        grid=(T // tm,),
        in_specs=[tile(D), pl.BlockSpec((1, 6, D), bmap), _resident((1, D)), tile(V_A), tile(V_B), tile(W_C),
                  _resident(wm.shape), _resident(wa.shape), _resident(wb.shape), _resident(wc.shape)],
        out_specs=tile(D),
        out_shape=jax.ShapeDtypeStruct((T, D), MXU),
        compiler_params=_cparams(("arbitrary",)),
        name="merge",
    )(x, mod, g.reshape(1, D), oa, ob, oc, wm, wa, wb, wc)


def _post_kernel(x_ref, m_ref, mod_ref, g_ref, wo_ref, wr_ref, x1_ref, h2_ref, aff_ref, acc0_ref):
    acc0_ref[...] = jnp.zeros(acc0_ref.shape, acc0_ref.dtype)
    mod = mod_ref[0]
    x1 = x_ref[...] + mod[2:3, :] * jnp.dot(m_ref[...], wo_ref[...], preferred_element_type=F32)
    x1_ref[...] = x1
    h2 = _norm_mod(x1, g_ref[...], mod[3:4, :], mod[4:5, :])
    _write_token_major(h2_ref, (), h2)
    logits = _mm_nt(wr_ref[...], h2)
    e = jnp.exp(logits - jnp.max(logits, axis=0, keepdims=True))
    aff_ref[...] = e / jnp.sum(e, axis=0, keepdims=True)


def _post(x, merged, mod, g, wo, wr_t, n_per_batch):
    T, D = x.shape
    tm = min(512, T)
    bmap = _tile_batch_map(tm, n_per_batch, mod.shape[0])
    tile = pl.BlockSpec((tm, D), lambda i: (i, 0))
    return pl.pallas_call(
        _post_kernel,
        grid=(T // tm,),
        in_specs=[tile, tile, pl.BlockSpec((1, 6, D), bmap), _resident((1, D)), _resident(wo.shape),
                  _resident(wr_t.shape)],
        out_specs=(tile, _token_major_spec(tm, D), pl.BlockSpec((N_EXPERTS, tm), lambda i: (0, i)),
                   _token_major_spec(tm, D)),
        out_shape=(jax.ShapeDtypeStruct((T, D), F32), jax.ShapeDtypeStruct((T * _tok_rows(D), LANES), F32),
                   jax.ShapeDtypeStruct((N_EXPERTS, T), F32),
                   jax.ShapeDtypeStruct((T * _tok_rows(D), LANES), F32)),
        compiler_params=_cparams(("arbitrary",)),
        name="post",
    )(x, merged, mod, g.reshape(1, D), wo, wr_t)


def _prefix_counts(mask, upper, lower_strict):
    within = _mm(mask, upper)
    row_tot = jnp.broadcast_to(within[:, LANES - 1:LANES], within.shape)
    row_start = _mm(lower_strict, row_tot)
    return row_start + within - mask, row_start, within


def _split_int(x):
    high = jnp.floor(x * (1.0 / 256.0))
    return high, x - high * 256.0


def _select_kernel(aff_ref, idx_ref, gate_ref, *, cap, slot_tile):
    R = aff_ref.shape[1]
    upper = (_iota((LANES, LANES), 0) <= _iota((LANES, LANES), 1)).astype(MXU)
    lower_strict = (_iota((R, R), 1) < _iota((R, R), 0)).astype(MXU)
    lane0 = (_iota((8, LANES), 1) == 0).astype(MXU)
    lane_id = _iota((slot_tile, LANES), 1)
    row_id = _iota((slot_tile, R), 1)

    aff = aff_ref[0]
    bits = pltpu.bitcast(aff, I32)

    def count_ge(cand):
        return jnp.sum((bits >= cand).astype(I32), axis=(0, 1), keepdims=True)

    def two_bit_step(j, thr):
        lo = jnp.left_shift(jnp.int32(1), 29 - 2 * j)
        c1, c2, c3 = thr | lo, thr | (lo * 2), thr | (lo * 3)
        n1, n2, n3 = count_ge(c1), count_ge(c2), count_ge(c3)
        return jnp.where(n3 >= cap, c3, jnp.where(n2 >= cap, c2, jnp.where(n1 >= cap, c1, thr)))

    thr = lax.fori_loop(0, 15, two_bit_step, jnp.zeros((1, 1), I32))
    thr = jnp.where(count_ge(thr | 1) >= cap, thr | 1, thr)
    gt = (bits > thr).astype(F32)
    eq = (bits == thr).astype(F32)
    need = float(cap) - jnp.sum(gt, axis=(0, 1), keepdims=True)
    eq_rank, _, _ = _prefix_counts(eq, upper, lower_strict)
    sel = gt + eq * (eq_rank < need).astype(F32)
    _, row_start, within = _prefix_counts(sel, upper, lower_strict)
    row_end = row_start + jnp.broadcast_to(within[:, LANES - 1:LANES], within.shape)
    end_hi, end_lo = _split_int(row_end)
    row_end_t = (_mm_nt(lane0, end_hi) * 256.0 + _mm_nt(lane0, end_lo))[0:1, :]
    start_hi, start_lo = _split_int(row_start)
    aff_parts = _split3(aff)

    def tile(t, carry):
        base = t * slot_tile
        slot = (base + _iota((slot_tile, 1), 0)).astype(F32)
        row = jnp.sum((row_end_t <= slot).astype(F32), axis=1, keepdims=True)
        onehot = (row_id.astype(F32) == row).astype(MXU)
        start = _mm(onehot, start_hi) * 256.0 + _mm(onehot, start_lo)
        rank = slot - start
        counts = _mm(onehot, within)
        col = jnp.sum((counts <= rank).astype(F32), axis=1, keepdims=True)
        vals = sum(_mm(onehot, p) for p in aff_parts)
        gate = jnp.sum(jnp.where(lane_id.astype(F32) == col, vals, 0.0), axis=1, keepdims=True)
        token = (row * float(LANES) + col).astype(I32)
        out_rows = pl.ds(pl.multiple_of(base, slot_tile), slot_tile)
        idx_ref[0, out_rows, :] = jnp.broadcast_to(token, (slot_tile, LANES))
        gate_ref[0, out_rows, :] = jnp.broadcast_to(gate, (slot_tile, LANES))
        return carry

    lax.fori_loop(0, cap // slot_tile, tile, 0)


def _select(aff_t, cap):
    E, n = aff_t.shape
    assert n % LANES == 0
    R = n // LANES
    slot_tile = min(512, cap)
    shape = (E, cap, LANES)
    return pl.pallas_call(
        functools.partial(_select_kernel, cap=cap, slot_tile=slot_tile),
        grid=(E,),
        in_specs=[pl.BlockSpec((1, R, LANES), lambda e: (e, 0, 0))],
        out_specs=(pl.BlockSpec((1, cap, LANES), lambda e: (e, 0, 0)),
                   pl.BlockSpec((1, cap, LANES), lambda e: (e, 0, 0))),
        out_shape=(jax.ShapeDtypeStruct(shape, I32), jax.ShapeDtypeStruct(shape, F32)),
        compiler_params=_cparams(("arbitrary",)),
        name="select",
    )(aff_t.reshape(E, R, LANES))


MOE_TILE = 512
MOE_ISSUE_UNROLL = 16


def _moe_kernel(idx_ref, idx_next_ref, h_hbm, gate_ref, wg_ref, wu_ref, wd_ref, acc_in, acc_hbm, xbuf, obuf, sem,
                *, ts, per):
    del acc_in
    tok_rows = xbuf.shape[1] // ts
    t = pl.program_id(1)
    step = pl.program_id(0) * per + t
    n_steps = pl.num_programs(0) * per
    cur, nxt = lax.rem(step, 2), lax.rem(step + 1, 2)
    ring, ring_next, ring_prev = lax.rem(step, 3), lax.rem(step + 1, 3), lax.rem(step + 2, 3)
    first, last = step == 0, step == n_steps - 1
    expert_start, expert_end = t == 0, t == per - 1
    prev_was_expert_start = (t == 1) if per > 1 else True

    def issue_row(ids_ref, kind, buf, i):
        tok = pl.ds(pl.multiple_of(ids_ref[0, 0, i] * tok_rows, tok_rows), tok_rows)
        row = pl.ds(pl.multiple_of(i * tok_rows, tok_rows), tok_rows)
        if kind == 0:
            cp = pltpu.make_async_copy(h_hbm.at[tok, :], xbuf.at[buf, row, :], sem.at[0, buf])
        elif kind == 1:
            cp = pltpu.make_async_copy(acc_hbm.at[tok, :], obuf.at[buf, row, :], sem.at[1, buf])
        else:
            cp = pltpu.make_async_copy(obuf.at[buf, row, :], acc_hbm.at[tok, :], sem.at[2, buf])
        cp.start()

    def issue(ids_ref, kind, buf):
        def body(i, c):
            issue_row(ids_ref, kind, buf, i)
            return c
        lax.fori_loop(0, ts, body, 0, unroll=MOE_ISSUE_UNROLL)

    def wait_all(kind, buf):
        ref = xbuf if kind == 0 else obuf
        pltpu.make_async_copy(ref.at[buf], ref.at[buf], sem.at[kind, buf]).wait()

    @pl.when(first)
    def _():
        issue(idx_ref, 0, cur)

    @pl.when(jnp.logical_and(step >= 2, jnp.logical_not(prev_was_expert_start)))
    def _():
        wait_all(2, ring_next)

    @pl.when(expert_start)
    def _():
        @pl.when(jnp.logical_not(first))
        def _():
            wait_all(2, ring_prev)
        issue(idx_ref, 1, ring)

    wait_all(0, cur)
    wait_all(1, ring)

    @pl.when(jnp.logical_not(last))
    def _():
        issue(idx_next_ref, 0, nxt)

    @pl.when(jnp.logical_not(expert_end))
    def _():
        issue(idx_next_ref, 1, ring_next)

    x = _read_token_major(xbuf, (cur,), ts).astype(MXU)
    hidden = (_silu(jnp.dot(x, wg_ref[0, 0], preferred_element_type=F32))
              * jnp.dot(x, wu_ref[0, 0], preferred_element_type=F32))
    y = jnp.dot(hidden.astype(MXU), wd_ref[0, 0], preferred_element_type=F32)
    g = gate_ref[0][:, :1]
    _write_token_major(obuf, (ring,), _read_token_major(obuf, (ring,), ts) + y * g)
    issue(idx_ref, 2, ring)

    @pl.when(last)
    def _():
        @pl.when(jnp.logical_and(jnp.logical_not(first), jnp.logical_not(expert_start)))
        def _():
            wait_all(2, ring_prev)
        wait_all(2, ring)


def _moe(h2, idx, gate, layer, wg, wu, wd, acc0):
    D, FF = wg.shape[2:]
    tok_rows = _tok_rows(D)
    n = h2.shape[0] // tok_rows
    E, cap, _ = gate.shape
    ts = min(MOE_TILE, cap)
    per = cap // ts
    n_steps = E * per
    idx_blocks = idx.reshape(n_steps, 1, ts)
    wspec = lambda shape: pl.BlockSpec((1, 1) + shape, lambda e, t: (layer, e, 0, 0))
    ids = lambda shift: pl.BlockSpec((1, 1, ts), lambda e, t: (jnp.minimum(e * per + t + shift, n_steps - 1), 0, 0),
                                     memory_space=pltpu.SMEM)
    return pl.pallas_call(
        functools.partial(_moe_kernel, ts=ts, per=per),
        grid=(E, per),
        in_specs=[ids(0), ids(1),
                  pl.BlockSpec(memory_space=pl.ANY),
                  pl.BlockSpec((1, ts, LANES), lambda e, t: (e, t, 0)),
                  wspec((D, FF)), wspec((D, FF)), wspec((FF, D)),
                  pl.BlockSpec(memory_space=pl.ANY)],
        out_specs=pl.BlockSpec(memory_space=pl.ANY),
        out_shape=jax.ShapeDtypeStruct((n * tok_rows, LANES), F32),
        scratch_shapes=[pltpu.VMEM((2, ts * tok_rows, LANES), F32), pltpu.VMEM((3, ts * tok_rows, LANES), F32),
                        pltpu.SemaphoreType.DMA((3, 3))],
        input_output_aliases={7: 0},
        compiler_params=_cparams(("arbitrary", "arbitrary")),
        name="moe",
    )(idx_blocks, idx_blocks, h2, gate, wg, wu, wd, acc0)


def _cast_kernel(g_ref, u_ref, d_ref, go_ref, uo_ref, do_ref):
    go_ref[...] = g_ref[...].astype(go_ref.dtype)
    uo_ref[...] = u_ref[...].astype(uo_ref.dtype)
    do_ref[...] = d_ref[...].astype(do_ref.dtype)


def _cast_experts(w_gate, w_up, w_down):
    L, E, D, FF = w_gate.shape
    halves = 2
    assert D % (8 * halves) == 0 and FF % (8 * halves) == 0
    spec = lambda rows, cols: pl.BlockSpec((1, 1, rows // halves, cols), lambda g, h: (g // E, g % E, h, 0))
    specs = [spec(D, FF), spec(D, FF), spec(FF, D)]
    return pl.pallas_call(
        _cast_kernel,
        grid=(L * E, halves),
        in_specs=specs,
        out_specs=tuple(specs),
        out_shape=tuple(jax.ShapeDtypeStruct(w.shape, MXU) for w in (w_gate, w_up, w_down)),
        compiler_params=_cparams(("arbitrary", "arbitrary")),
        name="cast_experts",
    )(w_gate, w_up, w_down)


def _final_kernel(xa_ref, xb_ref, mod_ref, g_ref, o_ref):
    x = xa_ref[...] + mod_ref[0, 5:6, :] * _read_token_major(xb_ref, (), xa_ref.shape[0])
    o_ref[...] = x * lax.rsqrt(jnp.mean(x * x, axis=-1, keepdims=True) + EPS) * g_ref[...]


def _final(xa, xb, mod, g, n_per_batch):
    T, D = xa.shape
    tm = min(512, T)
    tile = pl.BlockSpec((tm, D), lambda i: (i, 0))
    return pl.pallas_call(
        _final_kernel,
        grid=(T // tm,),
        in_specs=[tile, _token_major_spec(tm, D),
                  pl.BlockSpec((1, 6, D), _tile_batch_map(tm, n_per_batch, mod.shape[0])), _resident((1, D))],
        out_specs=tile,
        out_shape=jax.ShapeDtypeStruct((T, D), F32),
        compiler_params=_cparams(("arbitrary",)),
        name="final_norm",
    )(xa, xb, mod, g.reshape(1, D))


def _rope_tables(n):
    quarter = DK_B // 4
    t = jnp.arange(n)
    row = (t // GRID_W).astype(F32)
    col = (t % GRID_W).astype(F32)
    inv = ROPE_BASE ** (-jnp.arange(quarter, dtype=F32) / quarter)
    ang = jnp.concatenate([row[:, None] * inv, col[:, None] * inv], axis=-1)
    cos, sin = jnp.cos(ang), jnp.sin(ang)
    cos_t = jnp.concatenate([cos, cos] * (LANES // DK_B), axis=-1)
    sin_t = jnp.concatenate([-sin, sin] * (LANES // DK_B), axis=-1)
    return cos_t, sin_t


def _prep_layer(l, w_in, gla_w_gf, gla_b_gf, gla_w_gb, gla_b_gb, ret_ld_f, ret_ld_b):
    D = w_in.shape[1]
    w = w_in[l]
    s = np.cumsum([0, QK_A, QK_A, V_A, V_A, GATE_RANK, GATE_RANK, QK_B, QK_B, V_B, V_B, W_C, W_C, W_C])
    a_q, a_k, a_v, a_r, a_zf, a_zb, b_q, b_k, b_v, b_g, c_q, c_k, c_v = [w[:, s[i]:s[i + 1]] for i in range(13)]
    zpad = jnp.zeros((D, LANES - 2 * GATE_RANK), w.dtype)
    w1a = jnp.concatenate([a_q, a_k, a_r, b_q, b_k, b_g, a_zf, a_zb, zpad], axis=1).astype(MXU)
    w1b = jnp.concatenate([a_v, b_v, c_q, c_k, c_v], axis=1).astype(MXU)
    wm = w[:, MIX_W:].astype(MXU)
    wf = jnp.zeros((LANES, QK_A), F32).at[:GATE_RANK].set(gla_w_gf[l])
    wb = jnp.zeros((LANES, QK_A), F32).at[GATE_RANK:2 * GATE_RANK].set(gla_w_gb[l])
    ld = lambda v: jnp.broadcast_to(v[l].reshape(H_B // 2, 2, 1), (H_B // 2, 2, 2 * LANES))
    return dict(w1a=w1a, w1b=w1b, wm=wm, wf=wf, bf=gla_b_gf[l].reshape(1, QK_A), wb=wb, bb=gla_b_gb[l].reshape(1, QK_A),
                ldf=ld(ret_ld_f), ldb=ld(ret_ld_b))


def kernel(x_prompt, x_sample, cache_nat_k, cache_nat_v, state_gla_fwd, state_gla_bwd, state_ret_fwd,
           state_ret_bwd, c, c_ctx, norm1, norm2, w_ada, b_ada, w_in, gla_w_gf, gla_b_gf, gla_w_gb, gla_b_gb,
           gla_gn, ret_log_decay_f, ret_log_decay_b, ret_gn, nat_rpb, w_br_a, w_br_b, w_br_c, w_out, w_router,
           w_gate, w_up, w_down, final_norm):
    B, n_ctx, D = x_prompt.shape
    Bd, n_lat, _ = x_sample.shape
    L = w_in.shape[0]
    past = cache_nat_k.shape[2]

    cond = jnp.concatenate([c_ctx[None, :], c], axis=0)
    n_cond = cond.shape[0]
    cond = jnp.pad(cond, ((0, -n_cond % 8), (0, 0)))
    mod_all = _ada(cond, w_ada, b_ada).reshape(L, cond.shape[0], 6, D)
    rope_tabs = _rope_tables(n_lat)
    wg_all, wu_all, wd_all = _cast_experts(w_gate, w_up, w_down)
    zero_a =jnp.zeros((B, H_A, DK_A, DV_A), F32)
    zero_b = jnp.zeros((B, H_B, DK_B, DV_B), F32)

    paths = {
        "ctx": dict(x=x_prompt.reshape(B * n_ctx, D), moe=None, B=B, n=n_ctx),
        "lat": dict(x=x_sample.reshape(Bd * n_lat, D), moe=None, B=Bd, n=n_lat),
    }
    outs = dict(nk=[], nv=[], gf=[], gb=[], rf=[], rb=[])
    mod_prev = {}
    for l in range(L):
        w = _prep_layer(l, w_in, gla_w_gf, gla_b_gf, gla_w_gb, gla_b_gb, ret_log_decay_f, ret_log_decay_b)
        wa, wb_, wc = w_br_a[l].astype(MXU), w_br_b[l].astype(MXU), w_br_c[l].astype(MXU)
        wo = w_out[l].astype(MXU)
        wr_t = w_router[l].T.astype(MXU)
        gn_a, gn_b = gla_gn[l].reshape(1, DV_A), ret_gn[l].reshape(1, DV_B)
        bias = _window_bias(nat_rpb[l].astype(F32), n_lat // GRID_W)
        mods = {"ctx": mod_all[l, 0:1], "lat": mod_all[l, 1:n_cond]}
        for name, st in paths.items():
            Bp, n = st["B"], st["n"]
            latent = name == "lat"
            mod = mods[name]
            Pa, Pb, x, nat_k, nat_v = _in_proj(st["x"], st["moe"], mod_prev.get(name), mod, norm1[l],
                                               w["w1a"], w["w1b"], n, emit_kv=not latent)
            if latent:
                sa_f0, sa_b0 = state_gla_fwd[:, l].astype(F32), state_gla_bwd[:, l].astype(F32)
                sb_f0, sb_b0 = state_ret_fwd[:, l].astype(F32), state_ret_bwd[:, l].astype(F32)
            else:
                sa_f0 = sa_b0 = zero_a
                sb_f0 = sb_b0 = zero_b
            o_a, sa_f, sa_b = _gla(Pa, Pb, Bp, n, w["wf"], w["bf"], w["wb"], w["bb"], gn_a, sa_f0, sa_b0)
            o_b, sb_f, sb_b = _ret(Pa, Pb, Bp, n, w["ldf"], w["ldb"], gn_b, sb_f0, sb_b0,
                                   rope_tabs if latent else None)
            if latent:
                k_ctx = cache_nat_k[:, l].astype(F32).reshape(Bp, past, W_C)
                v_ctx = cache_nat_v[:, l].astype(F32).reshape(Bp, past, W_C)
                o_c = _attn_lat(Pb, Bp, n, k_ctx, v_ctx, bias)
            else:
                o_c = _attn_ctx(Pb, Bp, n)
                outs["nk"].append(nat_k.reshape(Bp, n, H_C, D_C))
                outs["nv"].append(nat_v.reshape(Bp, n, H_C, D_C))
                outs["gf"].append(sa_f)
                outs["gb"].append(sa_b)
                outs["rf"].append(sb_f)
                outs["rb"].append(sb_b)
            merged = _merge(x, mod, norm1[l], o_a, o_b, o_c, w["wm"], wa, wb_, wc, n)
            x1, h2, aff_t, acc0 = _post(x, merged, mod, norm2[l], wo, wr_t, n)
            cap = EC_FACTOR * (Bp * n) // N_EXPERTS
            idx, gate = _select(aff_t, cap)
            st["moe"] = _moe(h2, idx[:, :, 0], gate, l, wg_all, wu_all, wd_all, acc0)
            st["x"] = x1
            mod_prev[name] = mod
    y = {name: _final(st["x"], st["moe"], mod_prev[name], final_norm, st["n"]) for name, st in paths.items()}
    stack = lambda xs: jnp.stack(xs, axis=1)
    return (y["ctx"].reshape(B, n_ctx, D), y["lat"].reshape(Bd, n_lat, D), stack(outs["nk"]), stack(outs["nv"]),
            stack(outs["gf"]), stack(outs["gb"]), stack(outs["rf"]), stack(outs["rb"]))
```

```python
import functools

import numpy as np
import jax
import jax.numpy as jnp
from jax import lax
from jax.experimental import pallas as pl
from jax.experimental.pallas import tpu as pltpu

F32 = jnp.float32
I32 = jnp.int32
MXU = jnp.bfloat16

GRID_W = 64
H_A, DK_A, DV_A = 4, 64, 128
GATE_RANK, GATE_TEMP = 16, 16.0
H_B, DK_B, DV_B = 4, 64, 128
H_C, D_C = 8, 64
WIN_H, WIN_W = 8, 16
CHUNK = 64
N_EXPERTS, EC_FACTOR = 16, 2
ROPE_BASE = 10000.0
EPS = 1e-6
NEG_INF = -1e30

QK_A, V_A = H_A * DK_A, H_A * DV_A
QK_B, V_B = H_B * DK_B, H_B * DV_B
W_C = H_C * D_C
C_AQ, C_AK, C_AR = 0, 256, 512
C_BQ, C_BK, C_BG, C_Z = 1024, 1280, 1536, 2048
C_AV, C_BV, C_CQ, C_CK, C_CV = 0, 512, 1024, 1536, 2048
MIX_W = 4640

LANES = 128
SUPER = 256
NCH = SUPER // CHUNK
Q_ROWS = 4
K_ROWS = Q_ROWS + WIN_H
VMEM_LIMIT = 56 * 1024 * 1024


def _cparams(sem):
    return pltpu.CompilerParams(dimension_semantics=sem, vmem_limit_bytes=VMEM_LIMIT)


def _mm(a, b):
    return jnp.dot(a.astype(MXU), b.astype(MXU), preferred_element_type=F32)


def _mm_nt(a, b):
    return lax.dot_general(a.astype(MXU), b.astype(MXU), (((1,), (1,)), ((), ())), preferred_element_type=F32)


def _iota(shape, dim):
    return lax.broadcasted_iota(I32, shape, dim)


def _sigmoid(x):
    return 1.0 / (1.0 + jnp.exp(-x))


def _silu(x):
    return x * _sigmoid(x)


def _resident(shape):
    nd = len(shape)
    return pl.BlockSpec(shape, lambda *_: (0,) * nd, pipeline_mode=pl.Buffered(1))


def _ada_kernel(cond_ref, w_ref, b_ref, o_ref):
    o_ref[0] = _mm(_silu(cond_ref[...]), w_ref[0]) + b_ref[0]


def _ada(cond, w_ada, b_ada):
    L, D, D6 = w_ada.shape
    R = cond.shape[0]
    tn = 1024 if D6 % 1024 == 0 else D
    assert D6 % tn == 0
    return pl.pallas_call(
        _ada_kernel,
        grid=(L, D6 // tn),
        in_specs=[pl.BlockSpec((R, D), lambda l, j: (0, 0)),
                  pl.BlockSpec((1, D, tn), lambda l, j: (l, 0, j)),
                  pl.BlockSpec((1, 1, tn), lambda l, j: (l, 0, j))],
        out_specs=pl.BlockSpec((1, R, tn), lambda l, j: (l, 0, j)),
        out_shape=jax.ShapeDtypeStruct((L, R, D6), F32),
        compiler_params=_cparams(("arbitrary", "arbitrary")),
        name="ada",
    )(cond, w_ada, b_ada.reshape(L, 1, D6))


def _norm_mod(x, g, shift, scale):
    y = x * lax.rsqrt(jnp.mean(x * x, axis=-1, keepdims=True) + EPS) * g
    return y * (1.0 + scale) + shift


def _tile_batch_map(tm, n_per_batch, nb):
    if nb == 1:
        return lambda i: (0, 0, 0)
    assert n_per_batch % tm == 0
    per = n_per_batch // tm
    return lambda i: (i // per, 0, 0)


def _col_chunks(width, step):
    return [(s, min(step, width - s)) for s in range(0, width, step)]


def _tok_rows(D):
    assert D % LANES == 0
    return D // LANES


def _read_token_major(ref, lead, n_tok):
    r = ref.shape[-2] // n_tok
    parts = [ref[lead + (pl.ds(j, n_tok, stride=r), slice(None))] for j in range(r)]
    return jnp.concatenate(parts, axis=1)


def _write_token_major(ref, lead, value):
    n_tok = value.shape[0]
    r = ref.shape[-2] // n_tok
    for j in range(r):
        ref[lead + (pl.ds(j, n_tok, stride=r), slice(None))] = value[:, j * LANES:(j + 1) * LANES]


def _token_major_spec(tm, D):
    return pl.BlockSpec((tm * _tok_rows(D), LANES), lambda i: (i, 0))


def _in_kernel(*refs, has_res, emit_kv):
    refs = list(refs)
    xa_ref = refs.pop(0)
    x = xa_ref[...]
    if has_res:
        xb_ref, modp_ref = refs.pop(0), refs.pop(0)
        x = x + modp_ref[0, 5:6, :] * _read_token_major(xb_ref, (), xa_ref.shape[0])
    mod_ref, g_ref, wa_ref, wb_ref, pa_ref, pb_ref = refs[:6]
    outs = refs[6:]
    if has_res:
        outs.pop(0)[...] = x
    h = _norm_mod(x, g_ref[...], mod_ref[0, 0:1, :], mod_ref[0, 1:2, :]).astype(MXU)
    for s, w in _col_chunks(wa_ref.shape[1], 512):
        pa_ref[:, s:s + w] = jnp.dot(h, wa_ref[:, s:s + w], preferred_element_type=F32)
    for s, w in _col_chunks(wb_ref.shape[1], 512):
        part = jnp.dot(h, wb_ref[:, s:s + w], preferred_element_type=F32)
        pb_ref[:, s:s + w] = part.astype(pb_ref.dtype)
        if emit_kv:
            if s == C_CK:
                outs[0][...] = part
            if s == C_CV:
                outs[1][...] = part


def _in_proj(xa, xb, mod_prev, mod, g, w1a, w1b, n_per_batch, emit_kv):
    T, D = xa.shape
    nb = mod.shape[0]
    has_res = xb is not None
    tm = min(256 if has_res else 512, T)
    bmap = _tile_batch_map(tm, n_per_batch, nb)
    tile = lambda w: pl.BlockSpec((tm, w), lambda i: (i, 0))
    mspec = pl.BlockSpec((1, 6, D), bmap)
    assert W_C == 512 and C_CK % 512 == 0 and C_CV % 512 == 0
    ins = [xa] + ([xb, mod_prev] if has_res else []) + [mod, g.reshape(1, D), w1a, w1b]
    in_specs = ([tile(D)] + ([_token_major_spec(tm, D), mspec] if has_res else [])
                + [mspec, _resident((1, D)), _resident(w1a.shape), _resident(w1b.shape)])
    outs = ([(w1a.shape[1], F32), (w1b.shape[1], MXU)] + ([(D, F32)] if has_res else [])
            + ([(W_C, F32), (W_C, F32)] if emit_kv else []))
    out = list(pl.pallas_call(
        functools.partial(_in_kernel, has_res=has_res, emit_kv=emit_kv),
        grid=(T // tm,),
        in_specs=in_specs,
        out_specs=tuple(tile(w) for w, _ in outs),
        out_shape=tuple(jax.ShapeDtypeStruct((T, w), dt) for w, dt in outs),
        compiler_params=_cparams(("arbitrary",)),
        name="in_proj",
    )(*ins))
    Pa, Pb = out.pop(0), out.pop(0)
    x = out.pop(0) if has_res else xa
    k, v = out if emit_kv else (None, None)
    return Pa, Pb, x, k, v


def _split3(x):
    p1 = x.astype(MXU)
    r1 = x - p1.astype(F32)
    p2 = r1.astype(MXU)
    p3 = (r1 - p2.astype(F32)).astype(MXU)
    return p1, p2, p3


def _chunk_masks():
    row, col = _iota((SUPER, SUPER), 0), _iota((SUPER, SUPER), 1)
    same = (row // CHUNK) == (col // CHUNK)
    return same, same & (col <= row), same & (col >= row)


def _head_replicate(h):
    return (_iota((LANES, SUPER), 0) == (_iota((LANES, SUPER), 1) % CHUNK) + DK_A * h).astype(MXU)


def _recur_step(q_in, kdec_t, att, v, state, dec_fn, same, fwd, rep):
    o_intra = _mm(att, v)
    kblk = jnp.where(same, jnp.concatenate([kdec_t] * NCH, axis=0), 0.0)
    upd = _mm(kblk, v)
    order = range(NCH) if fwd else range(NCH - 1, -1, -1)
    prev = [None] * NCH
    for c in order:
        prev[c] = state
        state = dec_fn(c) * state + upd[c * CHUNK:(c + 1) * CHUNK]
    q_blk = jnp.where(same, _mm(q_in, rep), 0.0)
    o_inter = _mm(q_blk, jnp.concatenate(prev, axis=0))
    return o_intra + o_inter, state


def _head_norm_gate(o, g, r):
    return o * lax.rsqrt(jnp.mean(o * o, axis=-1, keepdims=True) + EPS) * g * _silu(r)


def _gla_kernel(q_ref, k_ref, v_ref, r_ref, z_ref, wf_ref, bf_ref, wb_ref, bb_ref, gn_ref, s0f_ref, s0b_ref,
                o_ref, sf_ref, sb_ref, acc_ref, *, n_sc):
    same, tri_f, tri_b = _chunk_masks()
    reps = [_head_replicate(h) for h in range(2)]
    lane_head = _iota((1, LANES), 1) // DK_A

    def run(sc, states, fwd):
        off = pl.multiple_of(sc * SUPER, SUPER)
        rows = pl.ds(off, SUPER)
        q = q_ref[rows, :] * (DK_A ** -0.5)
        k = k_ref[rows, :]
        x = _mm(z_ref[rows, :], (wf_ref if fwd else wb_ref)[...]) + (bf_ref if fwd else bb_ref)[...]
        la = (jnp.minimum(x, 0.0) - jnp.log1p(jnp.exp(-jnp.abs(x)))) * (1.0 / GATE_TEMP)
        tri = tri_f if fwd else tri_b
        hi = la.astype(MXU)
        lo = la - hi.astype(F32)
        trim = tri.astype(MXU)
        b = _mm(trim, hi) + _mm(trim, lo)
        last = CHUNK - 1 if fwd else 0
        tot_rows = [b[c * CHUNK + last:c * CHUNK + last + 1, :] for c in range(NCH)]
        b_last = jnp.concatenate([jnp.broadcast_to(t, (CHUNK, LANES)) for t in tot_rows], axis=0)
        tot_t = jnp.transpose(jnp.concatenate(tot_rows + [jnp.zeros((LANES - NCH, LANES), F32)], axis=0))
        q_in = q * jnp.exp(b)
        k_out = k * jnp.exp(-b)
        kdec_t = jnp.transpose(k * jnp.exp(b_last - b))
        outs, new_states = [], []
        for h in range(2):
            head_rows = slice(h * DK_A, (h + 1) * DK_A)
            dec_fn = lambda c, t=tot_t[head_rows]: jnp.exp(jnp.broadcast_to(t[:, c:c + 1], (DK_A, DV_A)))
            qh = jnp.where(lane_head == h, q_in, 0.0)
            att = jnp.where(tri, _mm_nt(qh, k_out), 0.0)
            v = v_ref[rows, h * DV_A:(h + 1) * DV_A]
            o, s = _recur_step(qh, kdec_t[head_rows], att, v, states[h], dec_fn, same, fwd, reps[h])
            outs.append(o)
            new_states.append(s)
        return rows, outs, tuple(new_states)

    sf, sb = _both_directions(run, n_sc, acc_ref, o_ref, gn_ref, r_ref,
                              (s0f_ref[0, 0], s0f_ref[0, 1]), (s0b_ref[0, 0], s0b_ref[0, 1]))
    for h in range(2):
        sf_ref[0, h] = sf[h]
        sb_ref[0, h] = sb[h]


def _both_directions(run, n_sc, acc_ref, o_ref, gn_ref, gate_ref, sf, sb, unroll=1):
    dv = acc_ref.shape[1] // 2

    def park(rows, outs):
        for h in range(2):
            acc_ref[rows, h * dv:(h + 1) * dv] = outs[h]

    def finish(rows, outs, other=None):
        for h in range(2):
            cols = slice(h * dv, (h + 1) * dv)
            total = outs[h] + (acc_ref[rows, cols] if other is None else other[h])
            o_ref[rows, cols] = _head_norm_gate(total, gn_ref[...], gate_ref[rows, cols]).astype(o_ref.dtype)

    if n_sc == 1:
        rows, outs_f, sf = run(0, sf, True)
        _, outs_b, sb = run(0, sb, False)
        finish(rows, outs_f, outs_b)
        return sf, sb
    assert n_sc % 2 == 0

    def body(second):
        def step(i, carry):
            sf, sb = carry
            rows_f, outs_f, sf = run(i, sf, True)
            rows_b, outs_b, sb = run(n_sc - 1 - i, sb, False)
            (finish if second else park)(rows_f, outs_f)
            (finish if second else park)(rows_b, outs_b)
            return sf, sb
        return step

    carry = lax.fori_loop(0, n_sc // 2, body(False), (sf, sb), unroll=unroll)
    return lax.fori_loop(n_sc // 2, n_sc, body(True), carry, unroll=unroll)


def _pair_specs(n, col_q, col_k, col_v, col_r):
    return [pl.BlockSpec((n, LANES), lambda b, p: (b, col_q // LANES + p)),
            pl.BlockSpec((n, LANES), lambda b, p: (b, col_k // LANES + p)),
            pl.BlockSpec((n, 2 * DV_A), lambda b, p: (b, col_v // (2 * DV_A) + p)),
            pl.BlockSpec((n, 2 * DV_A), lambda b, p: (b, col_r // (2 * DV_A) + p))]


def _state_spec():
    return pl.BlockSpec((1, 2, DK_A, DV_A), lambda b, p: (b, p, 0, 0))


def _gla(Pa, Pb, B, n, wf, bf, wb, bb, gn, s0f, s0b):
    T = B * n
    assert n % SUPER == 0
    pair_w = pl.BlockSpec((LANES, LANES), lambda b, p: (0, p))
    pair_b = pl.BlockSpec((1, LANES), lambda b, p: (0, p))
    st_shape = jax.ShapeDtypeStruct((B, H_A, DK_A, DV_A), F32)
    return pl.pallas_call(
        functools.partial(_gla_kernel, n_sc=n // SUPER),
        grid=(B, H_A // 2),
        in_specs=_pair_specs(n, C_AQ, C_AK, C_AV, C_AR)
        + [pl.BlockSpec((n, LANES), lambda b, p: (b, C_Z // LANES)),
           pair_w, pair_b, pair_w, pair_b, _resident((1, DV_A)), _state_spec(), _state_spec()],
        out_specs=(pl.BlockSpec((n, 2 * DV_A), lambda b, p: (b, p)), _state_spec(), _state_spec()),
        out_shape=(jax.ShapeDtypeStruct((T, V_A), MXU), st_shape, st_shape),
        scratch_shapes=[pltpu.VMEM((n, 2 * DV_A), F32)],
        compiler_params=_cparams(("arbitrary", "arbitrary")),
        name="gla",
    )(Pa, Pa, Pb, Pa, Pa, wf, bf, wb, bb, gn, s0f, s0b)


def _ret_kernel(*refs, n_sc, rope):
    if rope:
        (q_ref, k_ref, v_ref, g_ref, ldf_ref, ldb_ref, gn_ref, s0f_ref, s0b_ref, cos_ref, sin_ref,
         o_ref, sf_ref, sb_ref, acc_ref) = refs
    else:
        (q_ref, k_ref, v_ref, g_ref, ldf_ref, ldb_ref, gn_ref, s0f_ref, s0b_ref,
         o_ref, sf_ref, sb_ref, acc_ref) = refs
    same, tri_f, tri_b = _chunk_masks()
    reps = [_head_replicate(h) for h in range(2)]
    lane_head = _iota((1, LANES), 1) // DK_B
    half = DK_B // 2
    first_half = (_iota((1, LANES), 1) % DK_B) < half
    diff = jnp.abs(_iota((SUPER, SUPER), 0) - _iota((SUPER, SUPER), 1)).astype(F32)
    pos = (_iota((SUPER, LANES), 0) % CHUNK).astype(F32)
    decays = {fwd: [jnp.where(tri_f if fwd else tri_b, jnp.exp(ld_ref[0, h:h + 1, :] * diff), 0.0)
                    for h in range(2)]
              for fwd, ld_ref in ((True, ldf_ref), (False, ldb_ref))}

    def rotate(x, rows):
        swapped = jnp.where(first_half, pltpu.roll(x, LANES - half, 1), pltpu.roll(x, half, 1))
        return x * cos_ref[rows, :] + swapped * sin_ref[rows, :]

    def run(sc, states, fwd):
        off = pl.multiple_of(sc * SUPER, SUPER)
        rows = pl.ds(off, SUPER)
        q = q_ref[rows, :]
        k = k_ref[rows, :] * (DK_B ** -0.5)
        if rope:
            q, k = rotate(q, rows), rotate(k, rows)
        ld = (ldf_ref if fwd else ldb_ref)[0]
        lg_lane = jnp.where(lane_head == 0, ld[0:1, :LANES], ld[1:2, :LANES])
        if fwd:
            q_in = q * jnp.exp(lg_lane * (pos + 1.0))
            k_dec = k * jnp.exp(lg_lane * (CHUNK - 1.0 - pos))
        else:
            q_in = q * jnp.exp(lg_lane * (CHUNK - pos))
            k_dec = k * jnp.exp(lg_lane * pos)
        kdec_t = jnp.transpose(k_dec)
        outs, new_states = [], []
        for h in range(2):
            qh = jnp.where(lane_head == h, q, 0.0)
            att = _mm_nt(qh, k) * decays[fwd][h]
            chunk_dec = jnp.exp(ld[h:h + 1, :DV_B] * float(CHUNK))
            v = v_ref[rows, h * DV_B:(h + 1) * DV_B]
            o, s = _recur_step(q_in, kdec_t[h * DK_B:(h + 1) * DK_B], att, v, states[h], lambda c: chunk_dec,
                               same, fwd, reps[h])
            outs.append(o)
            new_states.append(s)
        return rows, outs, tuple(new_states)

    sf, sb = _both_directions(run, n_sc, acc_ref, o_ref, gn_ref, g_ref,
                              (s0f_ref[0, 0], s0f_ref[0, 1]), (s0b_ref[0, 0], s0b_ref[0, 1]), unroll=2)
    for h in range(2):
        sf_ref[0, h] = sf[h]
        sb_ref[0, h] = sb[h]


def _ret(Pa, Pb, B, n, ldf, ldb, gn, s0f, s0b, rope_tabs):
    T = B * n
    rope = rope_tabs is not None
    ld_spec = pl.BlockSpec((1, 2, 2 * LANES), lambda b, p: (p, 0, 0))
    st_shape = jax.ShapeDtypeStruct((B, H_B, DK_B, DV_B), F32)
    ins = [Pa, Pa, Pb, Pa, ldf, ldb, gn, s0f, s0b] + (list(rope_tabs) if rope else [])
    return pl.pallas_call(
        functools.partial(_ret_kernel, n_sc=n // SUPER, rope=rope),
        grid=(B, H_B // 2),
        in_specs=_pair_specs(n, C_BQ, C_BK, C_BV, C_BG)
        + [ld_spec, ld_spec, _resident((1, DV_B)), _state_spec(), _state_spec()]
        + ([_resident((n, LANES))] * 2 if rope else []),
        out_specs=(pl.BlockSpec((n, 2 * DV_B), lambda b, p: (b, p)), _state_spec(), _state_spec()),
        out_shape=(jax.ShapeDtypeStruct((T, V_B), MXU), st_shape, st_shape),
        scratch_shapes=[pltpu.VMEM((n, 2 * DV_B), F32)],
        compiler_params=_cparams(("arbitrary", "arbitrary")),
        name="retention",
    )(*ins)


def _attn_ctx_kernel(q_ref, k_ref, v_ref, o_ref):
    lane_head = _iota((1, LANES), 1) // D_C
    q = q_ref[...] * jnp.asarray(D_C ** -0.5, q_ref.dtype)
    k = k_ref[...]
    v = v_ref[...]
    out = jnp.zeros(q.shape, F32)
    for h in range(2):
        s = _mm_nt(jnp.where(lane_head == h, q, jnp.zeros_like(q)), k)
        e = jnp.exp(s - jnp.max(s, axis=-1, keepdims=True))
        p = e / jnp.sum(e, axis=-1, keepdims=True)
        out = jnp.where(lane_head == h, _mm(p, v), out)
    o_ref[...] = out.astype(o_ref.dtype)


def _attn_ctx(P, B, n):
    spec = lambda col: pl.BlockSpec((n, LANES), lambda b, p: (b, col // LANES + p))
    return pl.pallas_call(
        _attn_ctx_kernel,
        grid=(B, H_C // 2),
        in_specs=[spec(C_CQ), spec(C_CK), spec(C_CV)],
        out_specs=pl.BlockSpec((n, LANES), lambda b, p: (b, p)),
        out_shape=jax.ShapeDtypeStruct((B * n, W_C), MXU),
        compiler_params=_cparams(("arbitrary", "arbitrary")),
        name="attn_ctx",
    )(P, P, P)


def _attn_lat_kernel(q_ref, k_ref, v_ref, kc_ref, vc_ref, bias_ref, o_ref, *, n_blk, key_start_max):
    lane_head = _iota((1, LANES), 1) // D_C
    nq, nk = Q_ROWS * GRID_W, K_ROWS * GRID_W
    kc = kc_ref[0].astype(MXU)
    vc = vc_ref[0].astype(MXU)

    def body(i, carry):
        qrows = pl.ds(pl.multiple_of(i * nq, nq), nq)
        kstart = jnp.clip(i * Q_ROWS - WIN_H // 2, 0, key_start_max)
        krows = pl.ds(pl.multiple_of(kstart * GRID_W, GRID_W), nk)
        variant = jnp.where(i == 0, 0, jnp.where(i == n_blk - 1, 2, 1))
        q = q_ref[qrows, :] * jnp.asarray(D_C ** -0.5, q_ref.dtype)
        k = k_ref[krows, :]
        v = v_ref[krows, :]
        out = jnp.zeros((nq, LANES), F32)
        for h in range(2):
            qh = jnp.where(lane_head == h, q, jnp.zeros_like(q))
            s_win = _mm_nt(qh, k) + bias_ref[h, variant]
            s_ctx = _mm_nt(qh, kc)
            m = jnp.maximum(jnp.max(s_win, axis=-1, keepdims=True), jnp.max(s_ctx, axis=-1, keepdims=True))
            e_win = jnp.exp(s_win - m)
            e_ctx = jnp.exp(s_ctx - m)
            inv = 1.0 / (jnp.sum(e_win, axis=-1, keepdims=True) + jnp.sum(e_ctx, axis=-1, keepdims=True))
            o = (_mm(e_win, v) + _mm(e_ctx, vc)) * inv
            out = jnp.where(lane_head == h, o, out)
        o_ref[qrows, :] = out.astype(o_ref.dtype)
        return carry

    lax.fori_loop(0, n_blk, body, 0, unroll=2)


def _window_bias(rpb, rows):
    n_blk = rows // Q_ROWS
    kh = min(WIN_H, rows)
    qc = np.arange(GRID_W)
    win_c = np.clip(qc - WIN_W // 2, 0, GRID_W - WIN_W)
    kc = np.arange(GRID_W)
    col_ok = (kc[None, :] >= win_c[:, None]) & (kc[None, :] < win_c[:, None] + WIN_W)
    coff = np.clip(kc[None, :] - qc[:, None], -(WIN_W - 1), WIN_W - 1) + WIN_W - 1
    col_pick = (coff[:, :, None] == np.arange(2 * WIN_W - 1)).astype(np.float32)
    exact = lax.Precision.HIGHEST
    tabs = []
    for blk in (0, min(1, n_blk - 1), n_blk - 1):
        r = blk * Q_ROWS + np.arange(Q_ROWS)
        kstart = int(np.clip(blk * Q_ROWS - WIN_H // 2, 0, rows - K_ROWS))
        kr = kstart + np.arange(K_ROWS)
        r0 = np.clip(r - WIN_H // 2, 0, rows - kh)
        row_ok = (kr[None, :] >= r0[:, None]) & (kr[None, :] < r0[:, None] + kh)
        roff = np.clip(kr[None, :] - r[:, None] + WIN_H - 1, 0, 2 * WIN_H - 2)
        ok = row_ok[:, None, :, None] & col_ok[None, :, None, :]
        row_pick = (roff[:, :, None] == np.arange(2 * WIN_H - 1)).astype(np.float32)
        by_row = jnp.einsum('qka,hab->hqkb', row_pick, rpb, precision=exact)
        bias = jnp.einsum('hqkb,cdb->hqckd', by_row, col_pick, precision=exact)
        tab = jnp.where(ok[None], bias, NEG_INF)
        tabs.append(tab.reshape(rpb.shape[0], Q_ROWS * GRID_W, K_ROWS * GRID_W))
    return jnp.stack(tabs, axis=1)


def _attn_lat(P, B, n, k_ctx, v_ctx, bias):
    rows = n // GRID_W
    assert rows % Q_ROWS == 0 and rows >= K_ROWS
    L_ctx = k_ctx.shape[1]
    spec = lambda col: pl.BlockSpec((n, LANES), lambda p, b: (b, col // LANES + p))
    cspec = pl.BlockSpec((1, L_ctx, LANES), lambda p, b: (b, 0, p))
    nq, nk = Q_ROWS * GRID_W, K_ROWS * GRID_W
    return pl.pallas_call(
        functools.partial(_attn_lat_kernel, n_blk=rows // Q_ROWS, key_start_max=rows - K_ROWS),
        grid=(H_C // 2, B),
        in_specs=[spec(C_CQ), spec(C_CK), spec(C_CV), cspec, cspec,
                  pl.BlockSpec((2, 3, nq, nk), lambda p, b: (p, 0, 0, 0))],
        out_specs=pl.BlockSpec((n, LANES), lambda p, b: (b, p)),
        out_shape=jax.ShapeDtypeStruct((B * n, W_C), MXU),
        compiler_params=_cparams(("arbitrary", "arbitrary")),
        name="attn_lat",
    )(P, P, P, k_ctx, v_ctx, bias)


def _merge_kernel(x_ref, mod_ref, g_ref, oa_ref, ob_ref, oc_ref, wm_ref, wa_ref, wb_ref, wc_ref, m_ref):
    D = x_ref.shape[1]
    h = _norm_mod(x_ref[...], g_ref[...], mod_ref[0, 0:1, :], mod_ref[0, 1:2, :]).astype(MXU)
    for s, w in _col_chunks(D, 512):
        acc = None
        for i, (o_ref, wbr_ref) in enumerate(((oa_ref, wa_ref), (ob_ref, wb_ref), (oc_ref, wc_ref))):
            gate = _sigmoid(jnp.dot(h, wm_ref[:, i * D + s:i * D + s + w], preferred_element_type=F32))
            term = gate * jnp.dot(o_ref[...], wbr_ref[:, s:s + w], preferred_element_type=F32)
            acc = term if acc is None else acc + term
        m_ref[:, s:s + w] = acc.astype(m_ref.dtype)


def _merge(x, mod, g, oa, ob, oc, wm, wa, wb, wc, n_per_batch):
    T, D = x.shape
    tm = min(512, T)
    bmap = _tile_batch_map(tm, n_per_batch, mod.shape[0])
    tile = lambda w: pl.BlockSpec((tm, w), lambda i: (i, 0))
    return pl.pallas_call(
        _merge_kernel,
        grid=(T // tm,),
        in_specs=[tile(D), pl.BlockSpec((1, 6, D), bmap), _resident((1, D)), tile(V_A), tile(V_B), tile(W_C),
                  _resident(wm.shape), _resident(wa.shape), _resident(wb.shape), _resident(wc.shape)],
        out_specs=tile(D),
        out_shape=jax.ShapeDtypeStruct((T, D), MXU),
        compiler_params=_cparams(("arbitrary",)),
        name="merge",
    )(x, mod, g.reshape(1, D), oa, ob, oc, wm, wa, wb, wc)


def _post_kernel(x_ref, m_ref, mod_ref, g_ref, wo_ref, wr_ref, x1_ref, h2_ref, aff_ref, acc0_ref):
    acc0_ref[...] = jnp.zeros(acc0_ref.shape, acc0_ref.dtype)
    mod = mod_ref[0]
    x1 = x_ref[...] + mod[2:3, :] * jnp.dot(m_ref[...], wo_ref[...], preferred_element_type=F32)
    x1_ref[...] = x1
    h2 = _norm_mod(x1, g_ref[...], mod[3:4, :], mod[4:5, :])
    _write_token_major(h2_ref, (), h2)
    logits = _mm_nt(wr_ref[...], h2)
    e = jnp.exp(logits - jnp.max(logits, axis=0, keepdims=True))
    aff_ref[...] = e / jnp.sum(e, axis=0, keepdims=True)


def _post(x, merged, mod, g, wo, wr_t, n_per_batch):
    T, D = x.shape
    tm = min(512, T)
    bmap = _tile_batch_map(tm, n_per_batch, mod.shape[0])
    tile = pl.BlockSpec((tm, D), lambda i: (i, 0))
    return pl.pallas_call(
        _post_kernel,
        grid=(T // tm,),
        in_specs=[tile, tile, pl.BlockSpec((1, 6, D), bmap), _resident((1, D)), _resident(wo.shape),
                  _resident(wr_t.shape)],
        out_specs=(tile, _token_major_spec(tm, D), pl.BlockSpec((N_EXPERTS, tm), lambda i: (0, i)),
                   _token_major_spec(tm, D)),
        out_shape=(jax.ShapeDtypeStruct((T, D), F32), jax.ShapeDtypeStruct((T * _tok_rows(D), LANES), F32),
                   jax.ShapeDtypeStruct((N_EXPERTS, T), F32),
                   jax.ShapeDtypeStruct((T * _tok_rows(D), LANES), F32)),
        compiler_params=_cparams(("arbitrary",)),
        name="post",
    )(x, merged, mod, g.reshape(1, D), wo, wr_t)


def _prefix_counts(mask, upper, lower_strict):
    within = _mm(mask, upper)
    row_tot = jnp.broadcast_to(within[:, LANES - 1:LANES], within.shape)
    row_start = _mm(lower_strict, row_tot)
    return row_start + within - mask, row_start, within


def _split_int(x):
    high = jnp.floor(x * (1.0 / 256.0))
    return high, x - high * 256.0


def _select_kernel(aff_ref, idx_ref, gate_ref, *, cap, slot_tile):
    R = aff_ref.shape[1]
    upper = (_iota((LANES, LANES), 0) <= _iota((LANES, LANES), 1)).astype(MXU)
    lower_strict = (_iota((R, R), 1) < _iota((R, R), 0)).astype(MXU)
    lane0 = (_iota((8, LANES), 1) == 0).astype(MXU)
    lane_id = _iota((slot_tile, LANES), 1)
    row_id = _iota((slot_tile, R), 1)

    aff = aff_ref[0]
    bits = pltpu.bitcast(aff, I32)

    def count_ge(cand):
        return jnp.sum((bits >= cand).astype(I32), axis=(0, 1), keepdims=True)

    def two_bit_step(j, thr):
        lo = jnp.left_shift(jnp.int32(1), 29 - 2 * j)
        c1, c2, c3 = thr | lo, thr | (lo * 2), thr | (lo * 3)
        n1, n2, n3 = count_ge(c1), count_ge(c2), count_ge(c3)
        return jnp.where(n3 >= cap, c3, jnp.where(n2 >= cap, c2, jnp.where(n1 >= cap, c1, thr)))

    thr = lax.fori_loop(0, 15, two_bit_step, jnp.zeros((1, 1), I32))
    thr = jnp.where(count_ge(thr | 1) >= cap, thr | 1, thr)
    gt = (bits > thr).astype(F32)
    eq = (bits == thr).astype(F32)
    need = float(cap) - jnp.sum(gt, axis=(0, 1), keepdims=True)
    eq_rank, _, _ = _prefix_counts(eq, upper, lower_strict)
    sel = gt + eq * (eq_rank < need).astype(F32)
    _, row_start, within = _prefix_counts(sel, upper, lower_strict)
    row_end = row_start + jnp.broadcast_to(within[:, LANES - 1:LANES], within.shape)
    end_hi, end_lo = _split_int(row_end)
    row_end_t = (_mm_nt(lane0, end_hi) * 256.0 + _mm_nt(lane0, end_lo))[0:1, :]
    start_hi, start_lo = _split_int(row_start)
    aff_parts = _split3(aff)

    def tile(t, carry):
        base = t * slot_tile
        slot = (base + _iota((slot_tile, 1), 0)).astype(F32)
        row = jnp.sum((row_end_t <= slot).astype(F32), axis=1, keepdims=True)
        onehot = (row_id.astype(F32) == row).astype(MXU)
        start = _mm(onehot, start_hi) * 256.0 + _mm(onehot, start_lo)
        rank = slot - start
        counts = _mm(onehot, within)
        col = jnp.sum((counts <= rank).astype(F32), axis=1, keepdims=True)
        vals = sum(_mm(onehot, p) for p in aff_parts)
        gate = jnp.sum(jnp.where(lane_id.astype(F32) == col, vals, 0.0), axis=1, keepdims=True)
        token = (row * float(LANES) + col).astype(I32)
        out_rows = pl.ds(pl.multiple_of(base, slot_tile), slot_tile)
        idx_ref[0, out_rows, :] = jnp.broadcast_to(token, (slot_tile, LANES))
        gate_ref[0, out_rows, :] = jnp.broadcast_to(gate, (slot_tile, LANES))
        return carry

    lax.fori_loop(0, cap // slot_tile, tile, 0)


def _select(aff_t, cap):
    E, n = aff_t.shape
    assert n % LANES == 0
    R = n // LANES
    slot_tile = min(512, cap)
    shape = (E, cap, LANES)
    return pl.pallas_call(
        functools.partial(_select_kernel, cap=cap, slot_tile=slot_tile),
        grid=(E,),
        in_specs=[pl.BlockSpec((1, R, LANES), lambda e: (e, 0, 0))],
        out_specs=(pl.BlockSpec((1, cap, LANES), lambda e: (e, 0, 0)),
                   pl.BlockSpec((1, cap, LANES), lambda e: (e, 0, 0))),
        out_shape=(jax.ShapeDtypeStruct(shape, I32), jax.ShapeDtypeStruct(shape, F32)),
        compiler_params=_cparams(("arbitrary",)),
        name="select",
    )(aff_t.reshape(E, R, LANES))


MOE_TILE = 512
MOE_ISSUE_UNROLL = 16


def _moe_kernel(idx_ref, idx_next_ref, h_hbm, gate_ref, wg_ref, wu_ref, wd_ref, acc_in, acc_hbm, xbuf, obuf, sem,
                *, ts, per):
    del acc_in
    tok_rows = xbuf.shape[1] // ts
    t = pl.program_id(1)
    step = pl.program_id(0) * per + t
    n_steps = pl.num_programs(0) * per
    cur, nxt = lax.rem(step, 2), lax.rem(step + 1, 2)
    ring, ring_next, ring_prev = lax.rem(step, 3), lax.rem(step + 1, 3), lax.rem(step + 2, 3)
    first, last = step == 0, step == n_steps - 1
    expert_start, expert_end = t == 0, t == per - 1
    prev_was_expert_start = (t == 1) if per > 1 else True

    def issue_row(ids_ref, kind, buf, i, priority):
        tok = pl.ds(pl.multiple_of(ids_ref[0, 0, i] * tok_rows, tok_rows), tok_rows)
        row = pl.ds(pl.multiple_of(i * tok_rows, tok_rows), tok_rows)
        if kind == 0:
            cp = pltpu.make_async_copy(h_hbm.at[tok, :], xbuf.at[buf, row, :], sem.at[0, buf])
        elif kind == 1:
            cp = pltpu.make_async_copy(acc_hbm.at[tok, :], obuf.at[buf, row, :], sem.at[1, buf])
        else:
            cp = pltpu.make_async_copy(obuf.at[buf, row, :], acc_hbm.at[tok, :], sem.at[2, buf])
        cp.start(priority=priority)

    def issue(ids_ref, kind, buf):
        def body(i, c):
            issue_row(ids_ref, kind, buf, 2 * i, 0)
            issue_row(ids_ref, kind, buf, 2 * i + 1, 1)
            return c
        lax.fori_loop(0, ts // 2, body, 0, unroll=MOE_ISSUE_UNROLL // 2)

    def wait_all(kind, buf):
        ref = xbuf if kind == 0 else obuf
        pltpu.make_async_copy(ref.at[buf], ref.at[buf], sem.at[kind, buf]).wait()

    @pl.when(first)
    def _():
        issue(idx_ref, 0, cur)

    @pl.when(jnp.logical_and(step >= 2, jnp.logical_not(prev_was_expert_start)))
    def _():
        wait_all(2, ring_next)

    @pl.when(expert_start)
    def _():
        @pl.when(jnp.logical_not(first))
        def _():
            wait_all(2, ring_prev)
        issue(idx_ref, 1, ring)

    wait_all(0, cur)
    wait_all(1, ring)

    @pl.when(jnp.logical_not(last))
    def _():
        issue(idx_next_ref, 0, nxt)

    @pl.when(jnp.logical_not(expert_end))
    def _():
        issue(idx_next_ref, 1, ring_next)

    x = _read_token_major(xbuf, (cur,), ts).astype(MXU)
    hidden = (_silu(jnp.dot(x, wg_ref[0, 0], preferred_element_type=F32))
              * jnp.dot(x, wu_ref[0, 0], preferred_element_type=F32))
    y = jnp.dot(hidden.astype(MXU), wd_ref[0, 0], preferred_element_type=F32)
    g = gate_ref[0][:, :1]
    _write_token_major(obuf, (ring,), _read_token_major(obuf, (ring,), ts) + y * g)
    issue(idx_ref, 2, ring)

    @pl.when(last)
    def _():
        @pl.when(jnp.logical_and(jnp.logical_not(first), jnp.logical_not(expert_start)))
        def _():
            wait_all(2, ring_prev)
        wait_all(2, ring)


def _moe(h2, idx, gate, layer, wg, wu, wd, acc0):
    D, FF = wg.shape[2:]
    tok_rows = _tok_rows(D)
    n = h2.shape[0] // tok_rows
    E, cap, _ = gate.shape
    ts = min(MOE_TILE, cap)
    per = cap // ts
    n_steps = E * per
    idx_blocks = idx.reshape(n_steps, 1, ts)
    wspec = lambda shape: pl.BlockSpec((1, 1) + shape, lambda e, t: (layer, e, 0, 0))
    ids = lambda shift: pl.BlockSpec((1, 1, ts), lambda e, t: (jnp.minimum(e * per + t + shift, n_steps - 1), 0, 0),
                                     memory_space=pltpu.SMEM)
    return pl.pallas_call(
        functools.partial(_moe_kernel, ts=ts, per=per),
        grid=(E, per),
        in_specs=[ids(0), ids(1),
                  pl.BlockSpec(memory_space=pl.ANY),
                  pl.BlockSpec((1, ts, LANES), lambda e, t: (e, t, 0)),
                  wspec((D, FF)), wspec((D, FF)), wspec((FF, D)),
                  pl.BlockSpec(memory_space=pl.ANY)],
        out_specs=pl.BlockSpec(memory_space=pl.ANY),
        out_shape=jax.ShapeDtypeStruct((n * tok_rows, LANES), F32),
        scratch_shapes=[pltpu.VMEM((2, ts * tok_rows, LANES), F32), pltpu.VMEM((3, ts * tok_rows, LANES), F32),
                        pltpu.SemaphoreType.DMA((3, 3))],
        input_output_aliases={7: 0},
        compiler_params=_cparams(("arbitrary", "arbitrary")),
        name="moe",
    )(idx_blocks, idx_blocks, h2, gate, wg, wu, wd, acc0)


def _cast_kernel(g_ref, u_ref, d_ref, go_ref, uo_ref, do_ref):
    go_ref[...] = g_ref[...].astype(go_ref.dtype)
    uo_ref[...] = u_ref[...].astype(uo_ref.dtype)
    do_ref[...] = d_ref[...].astype(do_ref.dtype)


def _cast_experts(w_gate, w_up, w_down):
    L, E, D, FF = w_gate.shape
    halves = 2
    assert D % (8 * halves) == 0 and FF % (8 * halves) == 0
    spec = lambda rows, cols: pl.BlockSpec((1, 1, rows // halves, cols), lambda g, h: (g // E, g % E, h, 0))
    specs = [spec(D, FF), spec(D, FF), spec(FF, D)]
    return pl.pallas_call(
        _cast_kernel,
        grid=(L * E, halves),
        in_specs=specs,
        out_specs=tuple(specs),
        out_shape=tuple(jax.ShapeDtypeStruct(w.shape, MXU) for w in (w_gate, w_up, w_down)),
        compiler_params=_cparams(("arbitrary", "arbitrary")),
        name="cast_experts",
    )(w_gate, w_up, w_down)


def _final_kernel(xa_ref, xb_ref, mod_ref, g_ref, o_ref):
    x = xa_ref[...] + mod_ref[0, 5:6, :] * _read_token_major(xb_ref, (), xa_ref.shape[0])
    o_ref[...] = x * lax.rsqrt(jnp.mean(x * x, axis=-1, keepdims=True) + EPS) * g_ref[...]


def _final(xa, xb, mod, g, n_per_batch):
    T, D = xa.shape
    tm = min(512, T)
    tile = pl.BlockSpec((tm, D), lambda i: (i, 0))
    return pl.pallas_call(
        _final_kernel,
        grid=(T // tm,),
        in_specs=[tile, _token_major_spec(tm, D),
                  pl.BlockSpec((1, 6, D), _tile_batch_map(tm, n_per_batch, mod.shape[0])), _resident((1, D))],
        out_specs=tile,
        out_shape=jax.ShapeDtypeStruct((T, D), F32),
        compiler_params=_cparams(("arbitrary",)),
        name="final_norm",
    )(xa, xb, mod, g.reshape(1, D))


def _rope_tables(n):
    quarter = DK_B // 4
    t = jnp.arange(n)
    row = (t // GRID_W).astype(F32)
    col = (t % GRID_W).astype(F32)
    inv = ROPE_BASE ** (-jnp.arange(quarter, dtype=F32) / quarter)
    ang = jnp.concatenate([row[:, None] * inv, col[:, None] * inv], axis=-1)
    cos, sin = jnp.cos(ang), jnp.sin(ang)
    cos_t = jnp.concatenate([cos, cos] * (LANES // DK_B), axis=-1)
    sin_t = jnp.concatenate([-sin, sin] * (LANES // DK_B), axis=-1)
    return cos_t, sin_t


def _prep_layer(l, w_in, gla_w_gf, gla_b_gf, gla_w_gb, gla_b_gb, ret_ld_f, ret_ld_b):
    D = w_in.shape[1]
    w = w_in[l]
    s = np.cumsum([0, QK_A, QK_A, V_A, V_A, GATE_RANK, GATE_RANK, QK_B, QK_B, V_B, V_B, W_C, W_C, W_C])
    a_q, a_k, a_v, a_r, a_zf, a_zb, b_q, b_k, b_v, b_g, c_q, c_k, c_v = [w[:, s[i]:s[i + 1]] for i in range(13)]
    zpad = jnp.zeros((D, LANES - 2 * GATE_RANK), w.dtype)
    w1a = jnp.concatenate([a_q, a_k, a_r, b_q, b_k, b_g, a_zf, a_zb, zpad], axis=1).astype(MXU)
    w1b = jnp.concatenate([a_v, b_v, c_q, c_k, c_v], axis=1).astype(MXU)
    wm = w[:, MIX_W:].astype(MXU)
    wf = jnp.zeros((LANES, QK_A), F32).at[:GATE_RANK].set(gla_w_gf[l])
    wb = jnp.zeros((LANES, QK_A), F32).at[GATE_RANK:2 * GATE_RANK].set(gla_w_gb[l])
    ld = lambda v: jnp.broadcast_to(v[l].reshape(H_B // 2, 2, 1), (H_B // 2, 2, 2 * LANES))
    return dict(w1a=w1a, w1b=w1b, wm=wm, wf=wf, bf=gla_b_gf[l].reshape(1, QK_A), wb=wb, bb=gla_b_gb[l].reshape(1, QK_A),
                ldf=ld(ret_ld_f), ldb=ld(ret_ld_b))


def kernel(x_prompt, x_sample, cache_nat_k, cache_nat_v, state_gla_fwd, state_gla_bwd, state_ret_fwd,
           state_ret_bwd, c, c_ctx, norm1, norm2, w_ada, b_ada, w_in, gla_w_gf, gla_b_gf, gla_w_gb, gla_b_gb,
           gla_gn, ret_log_decay_f, ret_log_decay_b, ret_gn, nat_rpb, w_br_a, w_br_b, w_br_c, w_out, w_router,
           w_gate, w_up, w_down, final_norm):
    B, n_ctx, D = x_prompt.shape
    Bd, n_lat, _ = x_sample.shape
    L = w_in.shape[0]
    past = cache_nat_k.shape[2]

    cond = jnp.concatenate([c_ctx[None, :], c], axis=0)
    n_cond = cond.shape[0]
    cond = jnp.pad(cond, ((0, -n_cond % 8), (0, 0)))
    mod_all = _ada(cond, w_ada, b_ada).reshape(L, cond.shape[0], 6, D)
    rope_tabs = _rope_tables(n_lat)
    wg_all, wu_all, wd_all = _cast_experts(w_gate, w_up, w_down)
    zero_a =jnp.zeros((B, H_A, DK_A, DV_A), F32)
    zero_b = jnp.zeros((B, H_B, DK_B, DV_B), F32)

    paths = {
        "ctx": dict(x=x_prompt.reshape(B * n_ctx, D), moe=None, B=B, n=n_ctx),
        "lat": dict(x=x_sample.reshape(Bd * n_lat, D), moe=None, B=Bd, n=n_lat),
    }
    outs = dict(nk=[], nv=[], gf=[], gb=[], rf=[], rb=[])
    mod_prev = {}
    for l in range(L):
        w = _prep_layer(l, w_in, gla_w_gf, gla_b_gf, gla_w_gb, gla_b_gb, ret_log_decay_f, ret_log_decay_b)
        wa, wb_, wc = w_br_a[l].astype(MXU), w_br_b[l].astype(MXU), w_br_c[l].astype(MXU)
        wo = w_out[l].astype(MXU)
        wr_t = w_router[l].T.astype(MXU)
        gn_a, gn_b = gla_gn[l].reshape(1, DV_A), ret_gn[l].reshape(1, DV_B)
        bias = _window_bias(nat_rpb[l].astype(F32), n_lat // GRID_W)
        mods = {"ctx": mod_all[l, 0:1], "lat": mod_all[l, 1:n_cond]}
        for name, st in paths.items():
            Bp, n = st["B"], st["n"]
            latent = name == "lat"
            mod = mods[name]
            Pa, Pb, x, nat_k, nat_v = _in_proj(st["x"], st["moe"], mod_prev.get(name), mod, norm1[l],
                                               w["w1a"], w["w1b"], n, emit_kv=not latent)
            if latent:
                sa_f0, sa_b0 = state_gla_fwd[:, l].astype(F32), state_gla_bwd[:, l].astype(F32)
                sb_f0, sb_b0 = state_ret_fwd[:, l].astype(F32), state_ret_bwd[:, l].astype(F32)
            else:
                sa_f0 = sa_b0 = zero_a
                sb_f0 = sb_b0 = zero_b
            o_a, sa_f, sa_b = _gla(Pa, Pb, Bp, n, w["wf"], w["bf"], w["wb"], w["bb"], gn_a, sa_f0, sa_b0)
            o_b, sb_f, sb_b = _ret(Pa, Pb, Bp, n, w["ldf"], w["ldb"], gn_b, sb_f0, sb_b0,
                                   rope_tabs if latent else None)
            if latent:
                k_ctx = cache_nat_k[:, l].astype(F32).reshape(Bp, past, W_C)
                v_ctx = cache_nat_v[:, l].astype(F32).reshape(Bp, past, W_C)
                o_c = _attn_lat(Pb, Bp, n, k_ctx, v_ctx, bias)
            else:
                o_c = _attn_ctx(Pb, Bp, n)
                outs["nk"].append(nat_k.reshape(Bp, n, H_C, D_C))
                outs["nv"].append(nat_v.reshape(Bp, n, H_C, D_C))
                outs["gf"].append(sa_f)
                outs["gb"].append(sa_b)
                outs["rf"].append(sb_f)
                outs["rb"].append(sb_b)
            merged = _merge(x, mod, norm1[l], o_a, o_b, o_c, w["wm"], wa, wb_, wc, n)
            x1, h2, aff_t, acc0 = _post(x, merged, mod, norm2[l], wo, wr_t, n)
            cap = EC_FACTOR * (Bp * n) // N_EXPERTS
            idx, gate = _select(aff_t, cap)
            st["moe"] = _moe(h2, idx[:, :, 0], gate, l, wg_all, wu_all, wd_all, acc0)
            st["x"] = x1
            mod_prev[name] = mod
    y = {name: _final(st["x"], st["moe"], mod_prev[name], final_norm, st["n"]) for name, st in paths.items()}
    stack = lambda xs: jnp.stack(xs, axis=1)
    return (y["ctx"].reshape(B, n_ctx, D), y["lat"].reshape(Bd, n_lat, D), stack(outs["nk"]), stack(outs["nv"]),
            stack(outs["gf"]), stack(outs["gb"]), stack(outs["rf"]), stack(outs["rb"]))
```
